```python
import jax, jax.numpy as jnp
from jax import lax
import numpy as np

D_MODEL = 1024
BATCH = 8
SEQ = 2048
DEPTH = 2
DEC_BATCH = 128
DEC_SEQ = 4
PAST_LEN = 16384
PAGE_SIZE = 128

N_MIXERS = 2
N_CONV_LAYERS = (DEPTH + 1) // 2
N_RET_LAYERS = DEPTH // 2
CONV_WIDTH = 3
N_HEADS = 4
QK_DIM = D_MODEL
HEAD_DK = QK_DIM // N_HEADS
V_DIM = 2 * D_MODEL
HEAD_DV = V_DIM // N_HEADS
RET_IN = 2 * QK_DIM + 2 * V_DIM
CHUNK = 128
D_FF = -(-8 * D_MODEL // (3 * 256)) * 256
RMS_EPS = 1e-6
GN_EPS = 1e-6
ROPE_BASE = 10000.0

kernel_name = "hybrid_shortconv_retention_decode_step"


def rmsnorm(x, g):
    xf = x.astype(jnp.float32)
    y = xf * lax.rsqrt(jnp.mean(xf * xf, axis=-1, keepdims=True) + RMS_EPS)
    return (y * g.astype(jnp.float32)).astype(x.dtype)


def swiglu(x, w_gate, w_up, w_down):
    return (jax.nn.silu(x @ w_gate) * (x @ w_up)) @ w_down


def short_conv_mixer(x, buf, w_in, w_conv, w_out):
    L = x.shape[1]
    b, c, h = jnp.split(x @ w_in, 3, axis=-1)
    u = c * h
    full = jnp.concatenate([buf.astype(u.dtype), u], axis=1)
    y = sum(w_conv[j] * full[:, j:j + L] for j in range(CONV_WIDTH))
    out = (b * y) @ w_out
    return out, full[:, -(CONV_WIDTH - 1):]


def rotary(x, pos):
    half = x.shape[-1] // 2
    inv = ROPE_BASE ** (-jnp.arange(half, dtype=jnp.float32) / half)
    ang = pos.astype(jnp.float32)[:, None] * inv[None, :]
    cos, sin = jnp.cos(ang), jnp.sin(ang)
    xf = x.astype(jnp.float32)
    x1, x2 = xf[..., :half], xf[..., half:]
    return jnp.concatenate([x1 * cos - x2 * sin, x1 * sin + x2 * cos], axis=-1).astype(x.dtype)


def retention_chunkwise(q, k, v, s0):
    B, H, L, _ = q.shape
    C = L if L <= CHUNK else CHUNK
    n = L // C
    dt = q.dtype
    log_g = jnp.log(1.0 - 2.0 ** (-5.0 - jnp.arange(H, dtype=jnp.float32)))
    idx = jnp.arange(C, dtype=jnp.float32)
    diff = idx[:, None] - idx[None, :]
    decay = jnp.where(diff[None] >= 0, jnp.exp(jnp.maximum(diff, 0.0)[None] * log_g[:, None, None]), 0.0).astype(dt)
    cross_w = jnp.exp((idx[None] + 1.0) * log_g[:, None]).astype(dt)
    state_w = jnp.exp((C - 1.0 - idx[None]) * log_g[:, None]).astype(dt)
    chunk_decay = jnp.exp(C * log_g).astype(dt)

    def to_chunks(t):
        return jnp.moveaxis(t.reshape(B, H, n, C, t.shape[-1]), 2, 0)

    def step(S, qkv):
        qc, kc, vc = qkv
        scores = jnp.einsum('bhid,bhjd->bhij', qc, kc) * decay
        inner = jnp.einsum('bhij,bhjv->bhiv', scores, vc)
        cross = jnp.einsum('bhid,bhdv->bhiv', qc, S) * cross_w[:, :, None]
        S_new = S * chunk_decay[:, None, None] + jnp.einsum('bhjd,bhjv->bhdv', kc * state_w[:, :, None], vc)
        return S_new, inner + cross

    S, o = lax.scan(step, s0.astype(dt), (to_chunks(q), to_chunks(k), to_chunks(v)))
    o = jnp.moveaxis(o, 0, 2).reshape(B, H, L, v.shape[-1])
    return o, S


def retention_mixer(x, s0, pos, w_in, gn_g, w_out):
    B, L, _ = x.shape
    z = x @ w_in
    q, k, v, g = jnp.split(z, [QK_DIM, 2 * QK_DIM, 2 * QK_DIM + V_DIM], axis=-1)
    q = q.reshape(B, L, N_HEADS, HEAD_DK).transpose(0, 2, 1, 3)
    k = k.reshape(B, L, N_HEADS, HEAD_DK).transpose(0, 2, 1, 3)
    v = v.reshape(B, L, N_HEADS, HEAD_DV).transpose(0, 2, 1, 3)
    q = rotary(q, pos)
    k = rotary(k, pos) * (HEAD_DK ** -0.5)
    o, S = retention_chunkwise(q, k, v, s0)
    of = o.astype(jnp.float32)
    mu = jnp.mean(of, axis=-1, keepdims=True)
    var = jnp.mean(jnp.square(of - mu), axis=-1, keepdims=True)
    of = (of - mu) * lax.rsqrt(var + GN_EPS)
    of = of.transpose(0, 2, 1, 3).reshape(B, L, V_DIM) * gn_g.astype(jnp.float32)
    out = (jax.nn.silu(g) * of.astype(x.dtype)) @ w_out
    return out, S


def trunk(x, conv_bufs, ret_states, pos, norm_mix, norm_ffn, conv_w_in, conv_w, conv_w_out,
          ret_w_in, ret_gn, ret_w_out, ffn_w_gate, ffn_w_up, ffn_w_down, final_norm):
    new_conv, new_ret = [], []
    for i in range(DEPTH):
        j = i // N_MIXERS
        h = rmsnorm(x, norm_mix[i])
        if i % N_MIXERS == 0:
            m, st = short_conv_mixer(h, conv_bufs[j], conv_w_in[j], conv_w[j], conv_w_out[j])
            new_conv.append(st)
        else:
            m, st = retention_mixer(h, ret_states[j], pos, ret_w_in[j], ret_gn[j], ret_w_out[j])
            new_ret.append(st)
        x = x + m
        x = x + swiglu(rmsnorm(x, norm_ffn[i]), ffn_w_gate[i], ffn_w_up[i], ffn_w_down[i])
    return rmsnorm(x, final_norm), jnp.stack(new_conv), jnp.stack(new_ret)


def setup_inputs(seed: int = 0) -> dict:
    key = jax.random.key(seed)
    ks = jax.random.split(key, 16)
    nrm = jax.random.normal
    D = D_MODEL
    return {
        "x_prompt": nrm(ks[0], (BATCH, SEQ, D), jnp.float32),
        "x_sample": nrm(ks[1], (DEC_BATCH, DEC_SEQ, D), jnp.float32),
        "state_conv": nrm(ks[2], (N_CONV_LAYERS, DEC_BATCH, CONV_WIDTH - 1, D), jnp.float32),
        "state_ret": 0.5 * nrm(ks[3], (N_RET_LAYERS, DEC_BATCH, N_HEADS, HEAD_DK, HEAD_DV), jnp.float32),
        "norm_mix": 1.0 + 0.02 * nrm(ks[4], (DEPTH, D), jnp.float32),
        "norm_ffn": 1.0 + 0.02 * nrm(ks[5], (DEPTH, D), jnp.float32),
        "conv_w_in": nrm(ks[6], (N_CONV_LAYERS, D, 3 * D), jnp.float32) * D ** -0.5,
        "conv_w": nrm(ks[7], (N_CONV_LAYERS, CONV_WIDTH, D), jnp.float32) * CONV_WIDTH ** -0.5,
        "conv_w_out": nrm(ks[8], (N_CONV_LAYERS, D, D), jnp.float32) * D ** -0.5,
        "ret_w_in": nrm(ks[9], (N_RET_LAYERS, D, RET_IN), jnp.float32) * D ** -0.5,
        "ret_gn": 1.0 + 0.02 * nrm(ks[10], (N_RET_LAYERS, V_DIM), jnp.float32),
        "ret_w_out": nrm(ks[11], (N_RET_LAYERS, V_DIM, D), jnp.float32) * V_DIM ** -0.5,
        "ffn_w_gate": nrm(ks[12], (DEPTH, D, D_FF), jnp.float32) * D ** -0.5,
        "ffn_w_up": nrm(ks[13], (DEPTH, D, D_FF), jnp.float32) * D ** -0.5,
        "ffn_w_down": nrm(ks[14], (DEPTH, D_FF, D), jnp.float32) * D_FF ** -0.5,
        "final_norm": 1.0 + 0.02 * nrm(ks[15], (D,), jnp.float32),
    }


def reference(x_prompt, x_sample, state_conv, state_ret, norm_mix, norm_ffn, conv_w_in, conv_w,
              conv_w_out, ret_w_in, ret_gn, ret_w_out, ffn_w_gate, ffn_w_up, ffn_w_down, final_norm):
    Bp, Lp, _ = x_prompt.shape
    Ls = x_sample.shape[1]
    conv0 = jnp.zeros((N_CONV_LAYERS, Bp, CONV_WIDTH - 1, D_MODEL), x_prompt.dtype)
    ret0 = jnp.zeros((N_RET_LAYERS, Bp, N_HEADS, HEAD_DK, HEAD_DV), x_prompt.dtype)
    pos_p = jnp.arange(Lp, dtype=jnp.float32)
    pos_s = PAST_LEN + jnp.arange(Ls, dtype=jnp.float32)
    w = (norm_mix, norm_ffn, conv_w_in, conv_w, conv_w_out, ret_w_in, ret_gn, ret_w_out,
         ffn_w_gate, ffn_w_up, ffn_w_down, final_norm)
    y_prompt, conv_prompt, ret_prompt = trunk(x_prompt, conv0, ret0, pos_p, *w)
    y_sample, conv_sample, ret_sample = trunk(x_sample, state_conv, state_ret, pos_s, *w)
    return (y_prompt, y_sample, conv_prompt, conv_sample, ret_prompt, ret_sample)
```

```python
import functools
import math

import jax
import jax.numpy as jnp
from jax import lax
from jax.experimental import pallas as pl
from jax.experimental.pallas import tpu as pltpu

F32 = jnp.float32
BF16 = jnp.bfloat16

N_MIXERS = 2
CONV_WIDTH = 3
N_HEADS = 4
CHUNK = 128
PAST_LEN = 16384
RMS_EPS = 1e-6
GN_EPS = 1e-6
ROPE_BASE = 10000.0

V7X_VMEM_LIMIT_CAP = 60000 * 1024
SUBLANES = 8
LANES = 128
ROW_TILE = 512
FF_CHUNK = 256


def _nbytes(shape, dtype):
    return math.prod(shape) * jnp.dtype(dtype).itemsize


def _vmem_limit(buffers, temporaries):
    need = sum(_nbytes(s, d) * n for s, d, n in buffers) + temporaries
    return min(int(need * 1.25) + (4 << 20), V7X_VMEM_LIMIT_CAP)


def _const_spec(shape):
    return pl.BlockSpec(shape, lambda *_: (0,) * len(shape), pipeline_mode=pl.Buffered(1))


def _rmsnorm(x, g):
    ms = jnp.mean(x * x, axis=-1, keepdims=True)
    return (x * lax.rsqrt(ms + RMS_EPS)) * g


def _dot(a, b):
    return jnp.dot(a, b, preferred_element_type=F32)


def _head_log_decay(h):
    return math.log(1.0 - 2.0 ** (-5.0 - h))


def _ffn_kernel(x_ref, g_ref, wg_ref, wu_ref, wd_ref, fin_ref, o_ref, act_ref, *, final_norm):
    x = x_ref[...]
    n = _rmsnorm(x, g_ref[...]).astype(BF16)
    d_ff = wg_ref.shape[1]
    for j in range(d_ff // FF_CHUNK):
        cols = slice(j * FF_CHUNK, (j + 1) * FF_CHUNK)
        gate = _dot(n, wg_ref[:, cols])
        up = _dot(n, wu_ref[:, cols])
        act_ref[:, cols] = (jax.nn.silu(gate) * up).astype(BF16)
    y = x + _dot(act_ref[...], wd_ref[...])
    if final_norm:
        y = _rmsnorm(y, fin_ref[...])
    o_ref[...] = y


def _ffn(x, g, wg, wu, wd, fin, *, final_norm):
    m, d = x.shape
    f = wg.shape[1]
    tm = ROW_TILE
    assert m % tm == 0 and f % FF_CHUNK == 0
    row = pl.BlockSpec((tm, d), lambda i: (i, 0))
    vmem = _vmem_limit(
        [((tm, d), F32, 4), ((d, f), BF16, 2), ((f, d), BF16, 1), ((tm, f), BF16, 1)],
        temporaries=6 * _nbytes((tm, d), F32))
    return pl.pallas_call(
        functools.partial(_ffn_kernel, final_norm=final_norm),
        grid=(m // tm,),
        in_specs=[row, _const_spec((1, d)), _const_spec((d, f)), _const_spec((d, f)),
                  _const_spec((f, d)), _const_spec((1, d))],
        out_specs=row,
        out_shape=jax.ShapeDtypeStruct((m, d), F32),
        scratch_shapes=[pltpu.VMEM((tm, f), BF16)],
        compiler_params=pltpu.CompilerParams(dimension_semantics=("arbitrary",), vmem_limit_bytes=vmem),
        name="ffn_final" if final_norm else "ffn",
    )(x, g, wg, wu, wd, fin)


def _conv_front(x, g_ref, win_ref):
    d = x.shape[1]
    hn = _rmsnorm(x, g_ref[...]).astype(BF16)
    b = _dot(hn, win_ref[:, 0:d])
    c = _dot(hn, win_ref[:, d:2 * d])
    h = _dot(hn, win_ref[:, 2 * d:3 * d])
    return b, c * h


def _conv_seq_kernel(x_ref, g_ref, win_ref, wc_ref, wout_ref, o_ref, st_ref, ubuf_ref):
    tm = x_ref.shape[1]
    x = x_ref[0]
    b, u = _conv_front(x, g_ref, win_ref)

    @pl.when(pl.program_id(1) == 0)
    def _():
        ubuf_ref[0:SUBLANES, :] = jnp.zeros((SUBLANES, x.shape[1]), F32)

    ubuf_ref[SUBLANES:SUBLANES + tm, :] = u
    y = (wc_ref[0:1, :] * ubuf_ref[SUBLANES - 2:SUBLANES - 2 + tm, :]
         + wc_ref[1:2, :] * ubuf_ref[SUBLANES - 1:SUBLANES - 1 + tm, :]
         + wc_ref[2:3, :] * u)
    ubuf_ref[0:SUBLANES, :] = ubuf_ref[tm:tm + SUBLANES, :]
    o_ref[0] = x + _dot((b * y).astype(BF16), wout_ref[...])
    st_ref[0] = u[tm - (CONV_WIDTH - 1):, :]


def _conv_mixer_seq(x, g, win, wc, wout):
    bsz, seq, d = x.shape
    tm = ROW_TILE
    assert seq % tm == 0 and wc.shape[0] == CONV_WIDTH == 3
    row = pl.BlockSpec((1, tm, d), lambda b, t: (b, t, 0))
    vmem = _vmem_limit(
        [((tm, d), F32, 4), ((d, 3 * d), BF16, 1), ((d, d), BF16, 1), ((tm + SUBLANES, d), F32, 1)],
        temporaries=10 * _nbytes((tm, d), F32))
    return pl.pallas_call(
        _conv_seq_kernel,
        grid=(bsz, seq // tm),
        in_specs=[row, _const_spec((1, d)), _const_spec((d, 3 * d)), _const_spec((CONV_WIDTH, d)),
                  _const_spec((d, d))],
        out_specs=[row, pl.BlockSpec((1, CONV_WIDTH - 1, d), lambda b, t: (b, 0, 0))],
        out_shape=[jax.ShapeDtypeStruct((bsz, seq, d), F32),
                   jax.ShapeDtypeStruct((bsz, CONV_WIDTH - 1, d), F32)],
        scratch_shapes=[pltpu.VMEM((tm + SUBLANES, d), F32)],
        compiler_params=pltpu.CompilerParams(dimension_semantics=("arbitrary", "arbitrary"),
                                             vmem_limit_bytes=vmem),
        name="conv_mixer_seq",
    )(x, g, win, wc, wout)


def _conv_step_kernel(x_ref, pre_ref, g_ref, win_ref, wc_ref, wout_ref, o_ref, u_ref, *, seq):
    x = x_ref[...]
    m = x.shape[0]
    b, u = _conv_front(x, g_ref, win_ref)
    pre = pre_ref[...]
    t = lax.rem(lax.broadcasted_iota(jnp.int32, (m, 1), 0), seq)
    back1 = jnp.where(t >= 1, pltpu.roll(u, 1, axis=0), pltpu.roll(pre, m - 1, axis=0))
    back2 = jnp.where(t >= 2, pltpu.roll(u, 2, axis=0), pre)
    y = wc_ref[0:1, :] * back2 + wc_ref[1:2, :] * back1 + wc_ref[2:3, :] * u
    o_ref[...] = x + _dot((b * y).astype(BF16), wout_ref[...])
    u_ref[...] = u


def _conv_mixer_step(x, buf, g, win, wc, wout, *, seq):
    m, d = x.shape
    taps = CONV_WIDTH - 1
    assert wc.shape[0] == CONV_WIDTH == 3 and seq >= taps and m % seq == 0
    pre = jnp.pad(buf, ((0, 0), (0, seq - taps), (0, 0))).reshape(m, d)
    full = pl.BlockSpec((m, d), lambda i: (0, 0))
    vmem = _vmem_limit(
        [((m, d), F32, 8), ((d, 3 * d), BF16, 1), ((d, d), BF16, 1)],
        temporaries=12 * _nbytes((m, d), F32))
    y, u = pl.pallas_call(
        functools.partial(_conv_step_kernel, seq=seq),
        grid=(1,),
        in_specs=[full, full, _const_spec((1, d)), _const_spec((d, 3 * d)),
                  _const_spec((CONV_WIDTH, d)), _const_spec((d, d))],
        out_specs=[full, full],
        out_shape=[jax.ShapeDtypeStruct((m, d), F32), jax.ShapeDtypeStruct((m, d), F32)],
        compiler_params=pltpu.CompilerParams(dimension_semantics=("arbitrary",), vmem_limit_bytes=vmem),
        name="conv_mixer_step",
    )(x, pre, g, win, wc, wout)
    return y, u.reshape(m // seq, seq, d)[:, seq - taps:, :]


def _ret_proj_kernel(x_ref, g_ref, win_ref, inv_ref, q_ref, k_ref, v_ref, sg_ref, *, seq, pos0):
    tm, d = x_ref.shape
    qk = q_ref.shape[1]
    vd = v_ref.shape[1]
    dk = qk // N_HEADS
    half = dk // 2
    hn = _rmsnorm(x_ref[...], g_ref[...]).astype(BF16)

    row = lax.broadcasted_iota(jnp.int32, (tm, half), 0)
    if seq % tm == 0:
        pos = row + lax.rem(pl.program_id(0) * tm, seq)
    else:
        assert tm % seq == 0
        pos = lax.rem(row, seq)
    ang = (pos + pos0).astype(F32) * inv_ref[...]
    cos, sin = jnp.cos(ang), jnp.sin(ang)

    def rotated(col0, out_ref, scale):
        for h in range(N_HEADS):
            c0 = col0 + h * dk
            z = _dot(hn, win_ref[:, c0:c0 + dk])
            x1, x2 = z[:, :half], z[:, half:]
            out_ref[:, h * dk:h * dk + half] = ((x1 * cos - x2 * sin) * scale).astype(out_ref.dtype)
            out_ref[:, h * dk + half:(h + 1) * dk] = ((x1 * sin + x2 * cos) * scale).astype(out_ref.dtype)

    rotated(0, q_ref, 1.0)
    rotated(qk, k_ref, dk ** -0.5)
    dv = vd // N_HEADS
    for h in range(N_HEADS):
        c0 = 2 * qk + h * dv
        v_ref[:, h * dv:(h + 1) * dv] = _dot(hn, win_ref[:, c0:c0 + dv]).astype(v_ref.dtype)
    for h in range(N_HEADS):
        c0 = 2 * qk + vd + h * dv
        sg_ref[:, h * dv:(h + 1) * dv] = jax.nn.silu(_dot(hn, win_ref[:, c0:c0 + dv]))


def _ret_proj(x, g, win, inv, *, seq, pos0, qv_dtype):
    m, d = x.shape
    ret_in = win.shape[1]
    qk, vd = ret_in // 6, ret_in // 3
    tm = ROW_TILE
    assert m % tm == 0 and (seq % tm == 0 or tm % seq == 0)
    row = lambda n: pl.BlockSpec((tm, n), lambda i: (i, 0))
    vmem = _vmem_limit(
        [((tm, d), F32, 2), ((d, ret_in), BF16, 1), ((tm, qk), F32, 4), ((tm, vd), F32, 4)],
        temporaries=8 * _nbytes((tm, d), F32))
    return pl.pallas_call(
        functools.partial(_ret_proj_kernel, seq=seq, pos0=pos0),
        grid=(m // tm,),
        in_specs=[row(d), _const_spec((1, d)), _const_spec((d, ret_in)), _const_spec(inv.shape)],
        out_specs=[row(qk), row(qk), row(vd), row(vd)],
        out_shape=[jax.ShapeDtypeStruct((m, qk), qv_dtype), jax.ShapeDtypeStruct((m, qk), F32),
                   jax.ShapeDtypeStruct((m, vd), qv_dtype), jax.ShapeDtypeStruct((m, vd), F32)],
        compiler_params=pltpu.CompilerParams(dimension_semantics=("arbitrary",), vmem_limit_bytes=vmem),
        name="ret_proj",
    )(x, g, win, inv)


def _ret_seq_kernel(q_ref, k_ref, v_ref, o_ref, s_ref):
    tm, qk = q_ref.shape
    dk = qk // N_HEADS
    dv = v_ref.shape[1] // N_HEADS

    @pl.when(pl.program_id(1) == 0)
    def _():
        s_ref[...] = jnp.zeros(s_ref.shape, F32)

    i = lax.broadcasted_iota(jnp.int32, (CHUNK, CHUNK), 0)
    j = lax.broadcasted_iota(jnp.int32, (CHUNK, CHUNK), 1)
    diff = (i - j).astype(F32)
    idx = lax.broadcasted_iota(jnp.int32, (CHUNK, 1), 0).astype(F32)
    for h in range(N_HEADS):
        lg = _head_log_decay(h)
        decay = jnp.where(diff >= 0, jnp.exp(jnp.maximum(diff, 0.0) * lg), 0.0)
        cross_w = jnp.exp((idx + 1.0) * lg)
        state_w = jnp.exp((CHUNK - 1.0 - idx) * lg)
        chunk_decay = math.exp(CHUNK * lg)
        for c in range(tm // CHUNK):
            rows = slice(c * CHUNK, (c + 1) * CHUNK)
            qc = q_ref[rows, h * dk:(h + 1) * dk]
            kc = k_ref[rows, h * dk:(h + 1) * dk]
            vc = v_ref[rows, h * dv:(h + 1) * dv]
            s = s_ref[0, h]
            scores = lax.dot_general(qc, kc.astype(BF16), (((1,), (1,)), ((), ())),
                                     preferred_element_type=F32) * decay
            inner = _dot(scores.astype(BF16), vc)
            cross = _dot(qc, s.astype(BF16)) * cross_w
            kw = (kc * state_w).astype(BF16)
            s_ref[0, h] = s * chunk_decay + lax.dot_general(kw, vc, (((0,), (0,)), ((), ())),
                                                           preferred_element_type=F32)
            o_ref[rows, h * dv:(h + 1) * dv] = inner + cross


def _ret_seq(q, k, v, *, bsz, seq):
    m, qk = q.shape
    vd = v.shape[1]
    dk, dv = qk // N_HEADS, vd // N_HEADS
    tm = ROW_TILE
    assert seq % tm == 0 and tm % CHUNK == 0 and m == bsz * seq
    nt = seq // tm
    row = lambda n: pl.BlockSpec((tm, n), lambda b, t: (b * nt + t, 0))
    vmem = _vmem_limit(
        [((tm, qk), BF16, 2), ((tm, qk), F32, 2), ((tm, vd), BF16, 2), ((tm, vd), F32, 2),
         ((N_HEADS, dk, dv), F32, 2)],
        temporaries=16 * _nbytes((dk, dv), F32))
    return pl.pallas_call(
        _ret_seq_kernel,
        grid=(bsz, nt),
        in_specs=[row(qk), row(qk), row(vd)],
        out_specs=[row(vd), pl.BlockSpec((1, N_HEADS, dk, dv), lambda b, t: (b, 0, 0, 0))],
        out_shape=[jax.ShapeDtypeStruct((m, vd), F32), jax.ShapeDtypeStruct((bsz, N_HEADS, dk, dv), F32)],
        compiler_params=pltpu.CompilerParams(dimension_semantics=("arbitrary", "arbitrary"),
                                             vmem_limit_bytes=vmem),
        name="ret_seq",
    )(q, k, v)


STEP_ROWS = 16


def _ret_step_kernel(q_ref, k_ref, v_ref, s0_ref, o_ref, s_ref, *, seq):
    rows, qk = q_ref.shape
    n_seq = rows // seq
    dk = qk // N_HEADS
    dv = v_ref.shape[1] // N_HEADS
    r = lax.broadcasted_iota(jnp.int32, (rows, 1), 0)
    t = lax.rem(r, seq)
    tf = t.astype(F32)
    for h in range(N_HEADS):
        lg = _head_log_decay(h)
        q = q_ref[:, h * dk:(h + 1) * dk]
        k = k_ref[:, h * dk:(h + 1) * dk]
        v = v_ref[:, h * dv:(h + 1) * dv]
        acc = jnp.zeros((rows, dv), F32)
        for dlt in range(seq):
            kd = k if dlt == 0 else pltpu.roll(k, dlt, axis=0)
            vd = v if dlt == 0 else pltpu.roll(v, dlt, axis=0)
            sc = jnp.sum(q * kd, axis=-1, keepdims=True) * math.exp(dlt * lg)
            acc = acc + jnp.where(t >= dlt, sc, 0.0) * vd
        cross_w = jnp.exp((tf + 1.0) * lg)
        state_w = jnp.exp((seq - 1.0 - tf) * lg)
        qb = q.astype(BF16)
        kw = k * state_w
        vb = v.astype(BF16)
        cross = jnp.zeros((rows, dv), F32)
        for b in range(n_seq):
            own = (r >= b * seq) & (r < (b + 1) * seq)
            s0 = s0_ref[b, h]
            cross = cross + jnp.where(own, _dot(qb, s0.astype(BF16)), 0.0)
            kwb = jnp.where(own, kw, 0.0).astype(BF16)
            s_ref[b, h] = s0 * math.exp(seq * lg) + lax.dot_general(
                kwb, vb, (((0,), (0,)), ((), ())), preferred_element_type=F32)
        o_ref[:, h * dv:(h + 1) * dv] = acc + cross * cross_w


def _ret_step(q, k, v, s0, *, seq):
    m, qk = q.shape
    vd = v.shape[1]
    bsz, _, dk, dv = s0.shape
    assert seq <= CHUNK and STEP_ROWS % seq == 0 and m % STEP_ROWS == 0 and m == bsz * seq
    nb = STEP_ROWS // seq
    row = lambda n: pl.BlockSpec((STEP_ROWS, n), lambda i: (i, 0))
    st = pl.BlockSpec((nb, N_HEADS, dk, dv), lambda i: (i, 0, 0, 0))
    vmem = _vmem_limit([((nb, N_HEADS, dk, dv), F32, 4), ((STEP_ROWS, vd), F32, 8)],
                       temporaries=16 * _nbytes((dk, dv), F32))
    return pl.pallas_call(
        functools.partial(_ret_step_kernel, seq=seq),
        grid=(m // STEP_ROWS,),
        in_specs=[row(qk), row(qk), row(vd), st],
        out_specs=[row(vd), st],
        out_shape=[jax.ShapeDtypeStruct((m, vd), F32), jax.ShapeDtypeStruct(s0.shape, F32)],
        compiler_params=pltpu.CompilerParams(dimension_semantics=("arbitrary",), vmem_limit_bytes=vmem),
        name="ret_step",
    )(q, k, v, s0)


def _ret_out_kernel(x_ref, o_ref, sg_ref, gn_ref, wout_ref, y_ref, act_ref):
    dv = o_ref.shape[1] // N_HEADS
    for h in range(N_HEADS):
        cols = slice(h * dv, (h + 1) * dv)
        o = o_ref[:, cols]
        mu = jnp.mean(o, axis=-1, keepdims=True)
        var = jnp.mean(jnp.square(o - mu), axis=-1, keepdims=True)
        of = ((o - mu) * lax.rsqrt(var + GN_EPS)) * gn_ref[:, cols]
        act_ref[:, cols] = (sg_ref[:, cols] * of).astype(BF16)
    y_ref[...] = x_ref[...] + _dot(act_ref[...], wout_ref[...])


def _ret_out(x, o, sg, gn, wout):
    m, d = x.shape
    vd = o.shape[1]
    tm = ROW_TILE
    assert m % tm == 0
    row = lambda n: pl.BlockSpec((tm, n), lambda i: (i, 0))
    vmem = _vmem_limit(
        [((tm, d), F32, 4), ((tm, vd), F32, 4), ((vd, d), BF16, 1), ((tm, vd), BF16, 1)],
        temporaries=6 * _nbytes((tm, d), F32))
    return pl.pallas_call(
        _ret_out_kernel,
        grid=(m // tm,),
        in_specs=[row(d), row(vd), row(vd), _const_spec((1, vd)), _const_spec((vd, d))],
        out_specs=row(d),
        out_shape=jax.ShapeDtypeStruct((m, d), F32),
        scratch_shapes=[pltpu.VMEM((tm, vd), BF16)],
        compiler_params=pltpu.CompilerParams(dimension_semantics=("arbitrary",), vmem_limit_bytes=vmem),
        name="ret_out",
    )(x, o, sg, gn, wout)


def kernel(x_prompt, x_sample, state_conv, state_ret, norm_mix, norm_ffn, conv_w_in, conv_w, conv_w_out,
           ret_w_in, ret_gn, ret_w_out, ffn_w_gate, ffn_w_up, ffn_w_down, final_norm):
    bp, lp, d = x_prompt.shape
    bs, ls, _ = x_sample.shape
    depth = norm_mix.shape[0]
    dk = ret_w_in.shape[2] // 6 // N_HEADS
    inv = (ROPE_BASE ** (-jnp.arange(dk // 2, dtype=F32) / (dk // 2)))[None, :]
    fin = final_norm[None, :]

    xp = x_prompt
    xs = x_sample.reshape(bs * ls, d)
    conv_p, conv_s, ret_p, ret_s = [], [], [], []
    for i in range(depth):
        j = i // N_MIXERS
        g_mix = norm_mix[i][None, :]
        if i % N_MIXERS == 0:
            win, wout = conv_w_in[j].astype(BF16), conv_w_out[j].astype(BF16)
            xp, st = _conv_mixer_seq(xp, g_mix, win, conv_w[j], wout)
            conv_p.append(st)
            xs, st = _conv_mixer_step(xs, state_conv[j], g_mix, win, conv_w[j], wout, seq=ls)
            conv_s.append(st)
        else:
            win, wout = ret_w_in[j].astype(BF16), ret_w_out[j].astype(BF16)
            gn = ret_gn[j][None, :]
            xf = xp.reshape(bp * lp, d)
            q, k, v, sg = _ret_proj(xf, g_mix, win, inv, seq=lp, pos0=0, qv_dtype=BF16)
            o, st = _ret_seq(q, k, v, bsz=bp, seq=lp)
            ret_p.append(st)
            xp = _ret_out(xf, o, sg, gn, wout).reshape(bp, lp, d)
            q, k, v, sg = _ret_proj(xs, g_mix, win, inv, seq=ls, pos0=PAST_LEN, qv_dtype=F32)
            o, st = _ret_step(q, k, v, state_ret[j], seq=ls)
            ret_s.append(st)
            xs = _ret_out(xs, o, sg, gn, wout)
        last = i == depth - 1
        g_ffn = norm_ffn[i][None, :]
        wg, wu, wd = ffn_w_gate[i].astype(BF16), ffn_w_up[i].astype(BF16), ffn_w_down[i].astype(BF16)
        xp = _ffn(xp.reshape(bp * lp, d), g_ffn, wg, wu, wd, fin, final_norm=last).reshape(bp, lp, d)
        xs = _ffn(xs, g_ffn, wg, wu, wd, fin, final_norm=last)
    return (xp, xs.reshape(bs, ls, d), jnp.stack(conv_p), jnp.stack(conv_s), jnp.stack(ret_p),
            jnp.stack(ret_s))
```

```python
import functools
import math

import jax
import jax.numpy as jnp
from jax import lax
from jax.experimental import pallas as pl
from jax.experimental.pallas import tpu as pltpu

F32 = jnp.float32
BF16 = jnp.bfloat16

N_MIXERS = 2
CONV_WIDTH = 3
N_HEADS = 4
PAST_LEN = 16384
RMS_EPS = 1e-6
GN_EPS = 1e-6
ROPE_BASE = 10000.0

V7X_VMEM_LIMIT_CAP = 60000 * 1024
V7X_MXU_DIM = 256
SUBLANES = 8
ROW_TILE = 512
FF_CHUNK = V7X_MXU_DIM
SEQ_CHUNK = V7X_MXU_DIM
STEP_ROWS = 16


def _nbytes(shape, dtype):
    return math.prod(shape) * jnp.dtype(dtype).itemsize


def _vmem_limit(buffers, temporaries):
    need = sum(_nbytes(s, d) * n for s, d, n in buffers) + temporaries
    return min(int(need * 1.25) + (4 << 20), V7X_VMEM_LIMIT_CAP)


def _const_spec(shape):
    return pl.BlockSpec(shape, lambda *_: (0,) * len(shape), pipeline_mode=pl.Buffered(1))


def _layer_spec(w, layer):
    return pl.BlockSpec((None,) + w.shape[1:], lambda *_: (layer, 0, 0), pipeline_mode=pl.Buffered(1))


def _rmsnorm(x, g):
    ms = jnp.mean(x * x, axis=-1, keepdims=True)
    return (x * lax.rsqrt(ms + RMS_EPS)) * g


def _dot(a, b):
    return jnp.dot(a, b, preferred_element_type=F32)


def _dot_nt(a, b):
    return lax.dot_general(a, b, (((1,), (1,)), ((), ())), preferred_element_type=F32)


def _dot_tn(a, b):
    return lax.dot_general(a, b, (((0,), (0,)), ((), ())), preferred_element_type=F32)


def _head_log_decay(h):
    return math.log(1.0 - 2.0 ** (-5.0 - h))


def _ffn_kernel(x_ref, g_ref, wg_ref, wu_ref, wd_ref, fin_ref, o_ref, act_ref, *, final_norm):
    x = x_ref[...]
    n = _rmsnorm(x, g_ref[...]).astype(BF16)
    d_ff = wg_ref.shape[1]
    for j in range(d_ff // FF_CHUNK):
        cols = slice(j * FF_CHUNK, (j + 1) * FF_CHUNK)
        gate = _dot(n, wg_ref[:, cols])
        up = _dot(n, wu_ref[:, cols])
        act_ref[:, cols] = (jax.nn.silu(gate) * up).astype(BF16)
    y = x + _dot(act_ref[...], wd_ref[...])
    if final_norm:
        y = _rmsnorm(y, fin_ref[...])
    o_ref[...] = y


def _ffn(x, g, wg, wu, wd, fin, *, layer, final_norm):
    m, d = x.shape
    f = wg.shape[2]
    tm = ROW_TILE
    assert m % tm == 0 and f % FF_CHUNK == 0
    row = pl.BlockSpec((tm, d), lambda i: (i, 0))
    vmem = _vmem_limit(
        [((tm, d), F32, 4), ((d, f), BF16, 2), ((f, d), BF16, 1), ((tm, f), BF16, 1)],
        temporaries=6 * _nbytes((tm, d), F32))
    return pl.pallas_call(
        functools.partial(_ffn_kernel, final_norm=final_norm),
        grid=(m // tm,),
        in_specs=[row, _const_spec((1, d)), _layer_spec(wg, layer), _layer_spec(wu, layer),
                  _layer_spec(wd, layer), _const_spec((1, d))],
        out_specs=row,
        out_shape=jax.ShapeDtypeStruct((m, d), F32),
        scratch_shapes=[pltpu.VMEM((tm, f), BF16)],
        compiler_params=pltpu.CompilerParams(dimension_semantics=("arbitrary",), vmem_limit_bytes=vmem),
        name="ffn_final" if final_norm else "ffn",
    )(x, g, wg, wu, wd, fin)


def _conv_front(x, g_ref, win_ref):
    d = x.shape[1]
    hn = _rmsnorm(x, g_ref[...]).astype(BF16)
    b = _dot(hn, win_ref[:, 0:d])
    c = _dot(hn, win_ref[:, d:2 * d])
    h = _dot(hn, win_ref[:, 2 * d:3 * d])
    return b, c * h


def _conv_seq_kernel(x_ref, g_ref, win_ref, wc_ref, wout_ref, o_ref, st_ref, ubuf_ref):
    tm = x_ref.shape[1]
    x = x_ref[0]
    b, u = _conv_front(x, g_ref, win_ref)

    @pl.when(pl.program_id(1) == 0)
    def _():
        ubuf_ref[0:SUBLANES, :] = jnp.zeros((SUBLANES, x.shape[1]), F32)

    ubuf_ref[SUBLANES:SUBLANES + tm, :] = u
    y = (wc_ref[0:1, :] * ubuf_ref[SUBLANES - 2:SUBLANES - 2 + tm, :]
         + wc_ref[1:2, :] * ubuf_ref[SUBLANES - 1:SUBLANES - 1 + tm, :]
         + wc_ref[2:3, :] * u)
    ubuf_ref[0:SUBLANES, :] = ubuf_ref[tm:tm + SUBLANES, :]
    o_ref[0] = x + _dot((b * y).astype(BF16), wout_ref[...])
    st_ref[0] = u[tm - (CONV_WIDTH - 1):, :]


def _conv_mixer_seq(x, g, win, wc, wout, *, layer):
    bsz, seq, d = x.shape
    tm = ROW_TILE
    assert seq % tm == 0 and wc.shape[1] == CONV_WIDTH == 3
    row = pl.BlockSpec((1, tm, d), lambda b, t: (b, t, 0))
    vmem = _vmem_limit(
        [((tm, d), F32, 4), ((d, 3 * d), BF16, 1), ((d, d), BF16, 1), ((tm + SUBLANES, d), F32, 1)],
        temporaries=10 * _nbytes((tm, d), F32))
    return pl.pallas_call(
        _conv_seq_kernel,
        grid=(bsz, seq // tm),
        in_specs=[row, _const_spec((1, d)), _layer_spec(win, layer), _layer_spec(wc, layer),
                  _layer_spec(wout, layer)],
        out_specs=[row, pl.BlockSpec((1, CONV_WIDTH - 1, d), lambda b, t: (b, 0, 0))],
        out_shape=[jax.ShapeDtypeStruct((bsz, seq, d), F32),
                   jax.ShapeDtypeStruct((bsz, CONV_WIDTH - 1, d), F32)],
        scratch_shapes=[pltpu.VMEM((tm + SUBLANES, d), F32)],
        compiler_params=pltpu.CompilerParams(dimension_semantics=("arbitrary", "arbitrary"),
                                             vmem_limit_bytes=vmem),
        name="conv_mixer_seq",
    )(x, g, win, wc, wout)


def _conv_step_kernel(x_ref, pre_ref, g_ref, win_ref, wc_ref, wout_ref, o_ref, u_ref, *, seq):
    x = x_ref[...]
    m = x.shape[0]
    b, u = _conv_front(x, g_ref, win_ref)
    pre = pre_ref[...]
    t = lax.rem(lax.broadcasted_iota(jnp.int32, (m, 1), 0), seq)
    back1 = jnp.where(t >= 1, pltpu.roll(u, 1, axis=0), pltpu.roll(pre, m - 1, axis=0))
    back2 = jnp.where(t >= 2, pltpu.roll(u, 2, axis=0), pre)
    y = wc_ref[0:1, :] * back2 + wc_ref[1:2, :] * back1 + wc_ref[2:3, :] * u
    o_ref[...] = x + _dot((b * y).astype(BF16), wout_ref[...])
    u_ref[...] = u


def _conv_mixer_step(x, buf, g, win, wc, wout, *, layer, seq):
    m, d = x.shape
    taps = CONV_WIDTH - 1
    assert wc.shape[1] == CONV_WIDTH == 3 and seq >= taps and m % seq == 0
    pre = jnp.pad(buf, ((0, 0), (0, seq - taps), (0, 0))).reshape(m, d)
    full = pl.BlockSpec((m, d), lambda i: (0, 0))
    vmem = _vmem_limit(
        [((m, d), F32, 8), ((d, 3 * d), BF16, 1), ((d, d), BF16, 1)],
        temporaries=12 * _nbytes((m, d), F32))
    y, u = pl.pallas_call(
        functools.partial(_conv_step_kernel, seq=seq),
        grid=(1,),
        in_specs=[full, full, _const_spec((1, d)), _layer_spec(win, layer), _layer_spec(wc, layer),
                  _layer_spec(wout, layer)],
        out_specs=[full, full],
        out_shape=[jax.ShapeDtypeStruct((m, d), F32), jax.ShapeDtypeStruct((m, d), F32)],
        compiler_params=pltpu.CompilerParams(dimension_semantics=("arbitrary",), vmem_limit_bytes=vmem),
        name="conv_mixer_step",
    )(x, pre, g, win, wc, wout)
    return y, u.reshape(m // seq, seq, d)[:, seq - taps:, :]


def _rope_angles(inv_ref, pos):
    ang = pos.astype(F32) * inv_ref[...]
    return jnp.cos(ang), jnp.sin(ang)


def _rotate(z, cos, sin):
    half = z.shape[1] // 2
    x1, x2 = z[:, :half], z[:, half:]
    return jnp.concatenate([x1 * cos - x2 * sin, x1 * sin + x2 * cos], axis=1)


def _proj_head(hn, win_ref, h, cos, sin):
    ret_in = win_ref.shape[1]
    qk, vd = ret_in // 6, ret_in // 3
    dk, dv = qk // N_HEADS, vd // N_HEADS
    q = _rotate(_dot(hn, win_ref[:, h * dk:(h + 1) * dk]), cos, sin)
    k = _rotate(_dot(hn, win_ref[:, qk + h * dk:qk + (h + 1) * dk]), cos, sin) * dk ** -0.5
    v = _dot(hn, win_ref[:, 2 * qk + h * dv:2 * qk + (h + 1) * dv])
    sg = jax.nn.silu(_dot(hn, win_ref[:, 2 * qk + vd + h * dv:2 * qk + vd + (h + 1) * dv]))
    return q, k, v, sg


def _gn_gate(o, sg, gn):
    mu = jnp.mean(o, axis=-1, keepdims=True)
    var = jnp.mean(jnp.square(o - mu), axis=-1, keepdims=True)
    of = ((o - mu) * lax.rsqrt(var + GN_EPS)) * gn
    return (sg * of).astype(BF16)


def _rope_table_kernel(inv_ref, cos_ref, sin_ref):
    pos = lax.broadcasted_iota(jnp.int32, cos_ref.shape, 0)
    cos_ref[...], sin_ref[...] = _rope_angles(inv_ref, pos)


def _rope_tables(inv, seq):
    out = jax.ShapeDtypeStruct((seq, inv.shape[1]), F32)
    return pl.pallas_call(_rope_table_kernel, out_shape=[out, out], name="rope_tables")(inv)


def _ret_layer_seq_kernel(x_ref, g_ref, win_ref, cos_ref, sin_ref, gn_ref, wout_ref, y_ref, s_ref, act_ref):
    tm = x_ref.shape[1]
    dv = gn_ref.shape[1] // N_HEADS
    c_len = SEQ_CHUNK
    x = x_ref[0]
    hn = _rmsnorm(x, g_ref[...]).astype(BF16)
    cos, sin = cos_ref[...], sin_ref[...]

    @pl.when(pl.program_id(1) == 0)
    def _():
        s_ref[...] = jnp.zeros(s_ref.shape, F32)

    i = lax.broadcasted_iota(jnp.int32, (c_len, c_len), 0)
    j = lax.broadcasted_iota(jnp.int32, (c_len, c_len), 1)
    diff = (i - j).astype(F32)
    idx = lax.broadcasted_iota(jnp.int32, (c_len, 1), 0).astype(F32)
    for h in range(N_HEADS):
        lg = _head_log_decay(h)
        decay = jnp.where(diff >= 0, jnp.exp(jnp.maximum(diff, 0.0) * lg), 0.0)
        cross_w = jnp.exp((idx + 1.0) * lg)
        state_w = jnp.exp((c_len - 1.0 - idx) * lg)
        chunk_decay = math.exp(c_len * lg)
        q, k, v, sg = _proj_head(hn, win_ref, h, cos, sin)
        qb, vb = q.astype(BF16), v.astype(BF16)
        cols = slice(h * dv, (h + 1) * dv)
        for c in range(tm // c_len):
            rows = slice(c * c_len, (c + 1) * c_len)
            qc, kc, vc = qb[rows], k[rows], vb[rows]
            s = s_ref[0, h]
            scores = _dot_nt(qc, kc.astype(BF16)) * decay
            o = _dot(scores.astype(BF16), vc) + _dot(qc, s.astype(BF16)) * cross_w
            s_ref[0, h] = s * chunk_decay + _dot_tn((kc * state_w).astype(BF16), vc)
            act_ref[rows, cols] = _gn_gate(o, sg[rows], gn_ref[:, cols])
    y_ref[0] = x + _dot(act_ref[...], wout_ref[...])


def _ret_layer_seq(x, g, win, inv, gn, wout, *, layer):
    bsz, seq, d = x.shape
    ret_in, vd = win.shape[2], wout.shape[1]
    dk, dv = ret_in // 6 // N_HEADS, vd // N_HEADS
    tm = ROW_TILE
    assert seq % tm == 0 and tm % SEQ_CHUNK == 0
    cos, sin = _rope_tables(inv, seq)
    row = pl.BlockSpec((1, tm, d), lambda b, t: (b, t, 0))
    rope = pl.BlockSpec((tm, dk // 2), lambda b, t: (t, 0))
    vmem = _vmem_limit(
        [((tm, d), F32, 4), ((d, ret_in), BF16, 1), ((vd, d), BF16, 1), ((tm, dk), F32, 4),
         ((N_HEADS, dk, dv), F32, 2), ((tm, vd), BF16, 1)],
        temporaries=8 * _nbytes((tm, d), F32))
    return pl.pallas_call(
        _ret_layer_seq_kernel,
        grid=(bsz, seq // tm),
        in_specs=[row, _const_spec((1, d)), _layer_spec(win, layer), rope, rope, _const_spec((1, vd)),
                  _layer_spec(wout, layer)],
        out_specs=[row, pl.BlockSpec((1, N_HEADS, dk, dv), lambda b, t: (b, 0, 0, 0))],
        out_shape=[jax.ShapeDtypeStruct((bsz, seq, d), F32),
                   jax.ShapeDtypeStruct((bsz, N_HEADS, dk, dv), F32)],
        scratch_shapes=[pltpu.VMEM((tm, vd), BF16)],
        compiler_params=pltpu.CompilerParams(dimension_semantics=("arbitrary", "arbitrary"),
                                             vmem_limit_bytes=vmem),
        name="ret_layer_seq",
    )(x, g, win, cos, sin, gn, wout)


def _ret_proj_kernel(x_ref, g_ref, win_ref, inv_ref, q_ref, k_ref, v_ref, sg_ref, *, seq, pos0):
    tm = x_ref.shape[0]
    dk, dv = q_ref.shape[1] // N_HEADS, v_ref.shape[1] // N_HEADS
    hn = _rmsnorm(x_ref[...], g_ref[...]).astype(BF16)
    row = lax.broadcasted_iota(jnp.int32, (tm, dk // 2), 0)
    cos, sin = _rope_angles(inv_ref, pos0 + lax.rem(row, seq))
    for h in range(N_HEADS):
        q, k, v, sg = _proj_head(hn, win_ref, h, cos, sin)
        q_ref[:, h * dk:(h + 1) * dk] = q
        k_ref[:, h * dk:(h + 1) * dk] = k
        v_ref[:, h * dv:(h + 1) * dv] = v
        sg_ref[:, h * dv:(h + 1) * dv] = sg


def _ret_proj(x, g, win, inv, *, layer, seq, pos0):
    m, d = x.shape
    ret_in = win.shape[2]
    qk, vd = ret_in // 6, ret_in // 3
    tm = ROW_TILE
    assert m % tm == 0 and tm % seq == 0
    row = lambda n: pl.BlockSpec((tm, n), lambda i: (i, 0))
    vmem = _vmem_limit(
        [((tm, d), F32, 2), ((d, ret_in), BF16, 1), ((tm, qk), F32, 4), ((tm, vd), F32, 4)],
        temporaries=8 * _nbytes((tm, d), F32))
    return pl.pallas_call(
        functools.partial(_ret_proj_kernel, seq=seq, pos0=pos0),
        grid=(m // tm,),
        in_specs=[row(d), _const_spec((1, d)), _layer_spec(win, layer), _const_spec(inv.shape)],
        out_specs=[row(qk), row(qk), row(vd), row(vd)],
        out_shape=[jax.ShapeDtypeStruct((m, qk), F32), jax.ShapeDtypeStruct((m, qk), F32),
                   jax.ShapeDtypeStruct((m, vd), F32), jax.ShapeDtypeStruct((m, vd), F32)],
        compiler_params=pltpu.CompilerParams(dimension_semantics=("arbitrary",), vmem_limit_bytes=vmem),
        name="ret_proj",
    )(x, g, win, inv)


def _ret_step_kernel(q_ref, k_ref, v_ref, s0_ref, o_ref, s_ref, *, seq):
    rows, qk = q_ref.shape
    n_seq = rows // seq
    dk = qk // N_HEADS
    dv = v_ref.shape[1] // N_HEADS
    r = lax.broadcasted_iota(jnp.int32, (rows, 1), 0)
    t = lax.rem(r, seq)
    tf = t.astype(F32)
    for h in range(N_HEADS):
        lg = _head_log_decay(h)
        q = q_ref[:, h * dk:(h + 1) * dk]
        k = k_ref[:, h * dk:(h + 1) * dk]
        v = v_ref[:, h * dv:(h + 1) * dv]
        acc = jnp.zeros((rows, dv), F32)
        for dlt in range(seq):
            kd = k if dlt == 0 else pltpu.roll(k, dlt, axis=0)
            vd = v if dlt == 0 else pltpu.roll(v, dlt, axis=0)
            sc = jnp.sum(q * kd, axis=-1, keepdims=True) * math.exp(dlt * lg)
            acc = acc + jnp.where(t >= dlt, sc, 0.0) * vd
        cross_w = jnp.exp((tf + 1.0) * lg)
        state_w = jnp.exp((seq - 1.0 - tf) * lg)
        qb = q.astype(BF16)
        kw = k * state_w
        vb = v.astype(BF16)
        cross = jnp.zeros((rows, dv), F32)
        for b in range(n_seq):
            own = (r >= b * seq) & (r < (b + 1) * seq)
            s0 = s0_ref[b, h]
            cross = cross + jnp.where(own, _dot(qb, s0.astype(BF16)), 0.0)
            kwb = jnp.where(own, kw, 0.0).astype(BF16)
            s_ref[b, h] = s0 * math.exp(seq * lg) + _dot_tn(kwb, vb)
        o_ref[:, h * dv:(h + 1) * dv] = acc + cross * cross_w


def _ret_step(q, k, v, s0, *, layer, seq):
    m, qk = q.shape
    vd = v.shape[1]
    _, bsz, _, dk, dv = s0.shape
    assert seq <= SEQ_CHUNK and STEP_ROWS % seq == 0 and m % STEP_ROWS == 0 and m == bsz * seq
    nb = STEP_ROWS // seq
    row = lambda n: pl.BlockSpec((STEP_ROWS, n), lambda i: (i, 0))
    vmem = _vmem_limit([((nb, N_HEADS, dk, dv), F32, 4), ((STEP_ROWS, vd), F32, 8)],
                       temporaries=16 * _nbytes((dk, dv), F32))
    return pl.pallas_call(
        functools.partial(_ret_step_kernel, seq=seq),
        grid=(m // STEP_ROWS,),
        in_specs=[row(qk), row(qk), row(vd),
                  pl.BlockSpec((None, nb, N_HEADS, dk, dv), lambda i: (layer, i, 0, 0, 0))],
        out_specs=[row(vd), pl.BlockSpec((nb, N_HEADS, dk, dv), lambda i: (i, 0, 0, 0))],
        out_shape=[jax.ShapeDtypeStruct((m, vd), F32), jax.ShapeDtypeStruct(s0.shape[1:], F32)],
        compiler_params=pltpu.CompilerParams(dimension_semantics=("arbitrary",), vmem_limit_bytes=vmem),
        name="ret_step",
    )(q, k, v, s0)


def _ret_out_kernel(x_ref, o_ref, sg_ref, gn_ref, wout_ref, y_ref, act_ref):
    dv = o_ref.shape[1] // N_HEADS
    for h in range(N_HEADS):
        cols = slice(h * dv, (h + 1) * dv)
        act_ref[:, cols] = _gn_gate(o_ref[:, cols], sg_ref[:, cols], gn_ref[:, cols])
    y_ref[...] = x_ref[...] + _dot(act_ref[...], wout_ref[...])


def _ret_out(x, o, sg, gn, wout, *, layer):
    m, d = x.shape
    vd = o.shape[1]
    tm = ROW_TILE
    assert m % tm == 0
    row = lambda n: pl.BlockSpec((tm, n), lambda i: (i, 0))
    vmem = _vmem_limit(
        [((tm, d), F32, 4), ((tm, vd), F32, 4), ((vd, d), BF16, 1), ((tm, vd), BF16, 1)],
        temporaries=6 * _nbytes((tm, d), F32))
    return pl.pallas_call(
        _ret_out_kernel,
        grid=(m // tm,),
        in_specs=[row(d), row(vd), row(vd), _const_spec((1, vd)), _layer_spec(wout, layer)],
        out_specs=row(d),
        out_shape=jax.ShapeDtypeStruct((m, d), F32),
        scratch_shapes=[pltpu.VMEM((tm, vd), BF16)],
        compiler_params=pltpu.CompilerParams(dimension_semantics=("arbitrary",), vmem_limit_bytes=vmem),
        name="ret_out",
    )(x, o, sg, gn, wout)


def kernel(x_prompt, x_sample, state_conv, state_ret, norm_mix, norm_ffn, conv_w_in, conv_w, conv_w_out,
           ret_w_in, ret_gn, ret_w_out, ffn_w_gate, ffn_w_up, ffn_w_down, final_norm):
    bp, lp, d = x_prompt.shape
    bs, ls, _ = x_sample.shape
    depth = norm_mix.shape[0]
    dk = ret_w_in.shape[2] // 6 // N_HEADS
    inv = (ROPE_BASE ** (-jnp.arange(dk // 2, dtype=F32) / (dk // 2)))[None, :]
    fin = final_norm[None, :]
    conv_w_in, conv_w_out, ret_w_in, ret_w_out, ffn_w_gate, ffn_w_up, ffn_w_down = (
        w.astype(BF16) for w in (conv_w_in, conv_w_out, ret_w_in, ret_w_out, ffn_w_gate, ffn_w_up, ffn_w_down))

    xp = x_prompt
    xs = x_sample.reshape(bs * ls, d)
    conv_p, conv_s, ret_p, ret_s = [], [], [], []
    for i in range(depth):
        j = i // N_MIXERS
        g_mix = norm_mix[i][None, :]
        if i % N_MIXERS == 0:
            xp, st = _conv_mixer_seq(xp, g_mix, conv_w_in, conv_w, conv_w_out, layer=j)
            conv_p.append(st)
            xs, st = _conv_mixer_step(xs, state_conv[j], g_mix, conv_w_in, conv_w, conv_w_out, layer=j, seq=ls)
            conv_s.append(st)
        else:
            gn = ret_gn[j][None, :]
            xp, st = _ret_layer_seq(xp, g_mix, ret_w_in, inv, gn, ret_w_out, layer=j)
            ret_p.append(st)
            q, k, v, sg = _ret_proj(xs, g_mix, ret_w_in, inv, layer=j, seq=ls, pos0=PAST_LEN)
            o, st = _ret_step(q, k, v, state_ret, layer=j, seq=ls)
            ret_s.append(st)
            xs = _ret_out(xs, o, sg, gn, ret_w_out, layer=j)
        last = i == depth - 1
        g_ffn = norm_ffn[i][None, :]
        ffn = functools.partial(_ffn, g=g_ffn, wg=ffn_w_gate, wu=ffn_w_up, wd=ffn_w_down, fin=fin,
                                layer=i, final_norm=last)
        xp = ffn(xp.reshape(bp * lp, d)).reshape(bp, lp, d)
        xs = ffn(xs)
    return (xp, xs.reshape(bs, ls, d), jnp.stack(conv_p), jnp.stack(conv_s), jnp.stack(ret_p),
            jnp.stack(ret_s))
```

```python
import functools
import math

import jax
import jax.numpy as jnp
from jax import lax
from jax.experimental import pallas as pl
from jax.experimental.pallas import tpu as pltpu

F32 = jnp.float32
BF16 = jnp.bfloat16

N_MIXERS = 2
CONV_WIDTH = 3
N_HEADS = 4
PAST_LEN = 16384
RMS_EPS = 1e-6
GN_EPS = 1e-6
ROPE_BASE = 10000.0

V7X_VMEM_LIMIT_CAP = 60000 * 1024
V7X_MXU_DIM = 256
SUBLANES = 8
ROW_TILE = 512
FF_CHUNK = V7X_MXU_DIM
SEQ_CHUNK = V7X_MXU_DIM
STEP_ROWS = 16


def _nbytes(shape, dtype):
    return math.prod(shape) * jnp.dtype(dtype).itemsize


def _vmem_limit(buffers, temporaries):
    need = sum(_nbytes(s, d) * n for s, d, n in buffers) + temporaries
    return min(int(need * 1.25) + (4 << 20), V7X_VMEM_LIMIT_CAP)


def _const_spec(shape):
    return pl.BlockSpec(shape, lambda *_: (0,) * len(shape), pipeline_mode=pl.Buffered(1))


def _layer_spec(w, layer):
    return pl.BlockSpec((None,) + w.shape[1:], lambda *_: (layer, 0, 0), pipeline_mode=pl.Buffered(1))


def _rmsnorm(x, g):
    ms = jnp.mean(x * x, axis=-1, keepdims=True)
    return (x * lax.rsqrt(ms + RMS_EPS)) * g


def _dot(a, b):
    return jnp.dot(a, b, preferred_element_type=F32)


def _dot_nt(a, b):
    return lax.dot_general(a, b, (((1,), (1,)), ((), ())), preferred_element_type=F32)


def _dot_tn(a, b):
    return lax.dot_general(a, b, (((0,), (0,)), ((), ())), preferred_element_type=F32)


def _head_log_decay(h):
    return math.log(1.0 - 2.0 ** (-5.0 - h))


def _ffn_kernel(x_ref, g_ref, wg_ref, wu_ref, wd_ref, fin_ref, o_ref, act_ref, *, final_norm):
    d_ff = wg_ref.shape[1]
    for r0 in range(0, x_ref.shape[0], ROW_TILE):
        rows = slice(r0, r0 + ROW_TILE)
        x = x_ref[rows, :]
        n = _rmsnorm(x, g_ref[...]).astype(BF16)
        for j in range(d_ff // FF_CHUNK):
            cols = slice(j * FF_CHUNK, (j + 1) * FF_CHUNK)
            gate = _dot(n, wg_ref[:, cols])
            up = _dot(n, wu_ref[:, cols])
            act_ref[rows, cols] = (jax.nn.silu(gate) * up).astype(BF16)
        y = x + _dot(act_ref[rows, :], wd_ref[...])
        if final_norm:
            y = _rmsnorm(y, fin_ref[...])
        o_ref[rows, :] = y


def _ffn(x, g, wg, wu, wd, fin, *, layer, final_norm):
    m, d = x.shape
    f = wg.shape[2]
    tm = min(2 * ROW_TILE, m)
    assert m % tm == 0 and tm % ROW_TILE == 0 and f % FF_CHUNK == 0
    row = pl.BlockSpec((tm, d), lambda i: (i, 0))
    vmem = _vmem_limit(
        [((tm, d), F32, 4), ((d, f), BF16, 2), ((f, d), BF16, 1), ((tm, f), BF16, 1)],
        temporaries=6 * _nbytes((tm, d), F32))
    return pl.pallas_call(
        functools.partial(_ffn_kernel, final_norm=final_norm),
        grid=(m // tm,),
        in_specs=[row, _const_spec((1, d)), _layer_spec(wg, layer), _layer_spec(wu, layer),
                  _layer_spec(wd, layer), _const_spec((1, d))],
        out_specs=row,
        out_shape=jax.ShapeDtypeStruct((m, d), F32),
        scratch_shapes=[pltpu.VMEM((tm, f), BF16)],
        compiler_params=pltpu.CompilerParams(dimension_semantics=("arbitrary",), vmem_limit_bytes=vmem),
        name="ffn_final" if final_norm else "ffn",
    )(x, g, wg, wu, wd, fin)


def _conv_front(x, g_ref, win_ref):
    d = x.shape[1]
    hn = _rmsnorm(x, g_ref[...]).astype(BF16)
    b = _dot(hn, win_ref[:, 0:d])
    c = _dot(hn, win_ref[:, d:2 * d])
    h = _dot(hn, win_ref[:, 2 * d:3 * d])
    return b, c * h


def _conv_seq_kernel(x_ref, g_ref, win_ref, wc_ref, wout_ref, o_ref, st_ref, ubuf_ref):
    tm, d = x_ref.shape[1:]

    @pl.when(pl.program_id(1) == 0)
    def _():
        ubuf_ref[0:SUBLANES, :] = jnp.zeros((SUBLANES, d), F32)

    for r0 in range(0, tm, ROW_TILE):
        x = x_ref[0, r0:r0 + ROW_TILE, :]
        b, u = _conv_front(x, g_ref, win_ref)
        ubuf_ref[SUBLANES + r0:SUBLANES + r0 + ROW_TILE, :] = u
        y = (wc_ref[0:1, :] * ubuf_ref[SUBLANES - 2 + r0:SUBLANES - 2 + r0 + ROW_TILE, :]
             + wc_ref[1:2, :] * ubuf_ref[SUBLANES - 1 + r0:SUBLANES - 1 + r0 + ROW_TILE, :]
             + wc_ref[2:3, :] * u)
        o_ref[0, r0:r0 + ROW_TILE, :] = x + _dot((b * y).astype(BF16), wout_ref[...])
    st_ref[0] = ubuf_ref[SUBLANES + tm - (CONV_WIDTH - 1):SUBLANES + tm, :]
    ubuf_ref[0:SUBLANES, :] = ubuf_ref[tm:tm + SUBLANES, :]


def _conv_mixer_seq(x, g, win, wc, wout, *, layer):
    bsz, seq, d = x.shape
    tm = 2 * ROW_TILE
    assert seq % tm == 0 and wc.shape[1] == CONV_WIDTH == 3
    row = pl.BlockSpec((1, tm, d), lambda b, t: (b, t, 0))
    vmem = _vmem_limit(
        [((tm, d), F32, 4), ((d, 3 * d), BF16, 1), ((d, d), BF16, 1), ((tm + SUBLANES, d), F32, 1)],
        temporaries=10 * _nbytes((tm, d), F32))
    return pl.pallas_call(
        _conv_seq_kernel,
        grid=(bsz, seq // tm),
        in_specs=[row, _const_spec((1, d)), _layer_spec(win, layer), _layer_spec(wc, layer),
                  _layer_spec(wout, layer)],
        out_specs=[row, pl.BlockSpec((1, CONV_WIDTH - 1, d), lambda b, t: (b, 0, 0))],
        out_shape=[jax.ShapeDtypeStruct((bsz, seq, d), F32),
                   jax.ShapeDtypeStruct((bsz, CONV_WIDTH - 1, d), F32)],
        scratch_shapes=[pltpu.VMEM((tm + SUBLANES, d), F32)],
        compiler_params=pltpu.CompilerParams(dimension_semantics=("arbitrary", "arbitrary"),
                                             vmem_limit_bytes=vmem),
        name="conv_mixer_seq",
    )(x, g, win, wc, wout)


def _conv_step_kernel(x_ref, pre_ref, g_ref, win_ref, wc_ref, wout_ref, o_ref, u_ref, *, seq):
    x = x_ref[...]
    m = x.shape[0]
    b, u = _conv_front(x, g_ref, win_ref)
    pre = pre_ref[...]
    t = lax.rem(lax.broadcasted_iota(jnp.int32, (m, 1), 0), seq)
    back1 = jnp.where(t >= 1, pltpu.roll(u, 1, axis=0), pltpu.roll(pre, m - 1, axis=0))
    back2 = jnp.where(t >= 2, pltpu.roll(u, 2, axis=0), pre)
    y = wc_ref[0:1, :] * back2 + wc_ref[1:2, :] * back1 + wc_ref[2:3, :] * u
    o_ref[...] = x + _dot((b * y).astype(BF16), wout_ref[...])
    u_ref[...] = u


def _conv_mixer_step(x, buf, g, win, wc, wout, *, layer, seq):
    m, d = x.shape
    taps = CONV_WIDTH - 1
    assert wc.shape[1] == CONV_WIDTH == 3 and seq >= taps and m % seq == 0
    pre = jnp.pad(buf, ((0, 0), (0, seq - taps), (0, 0))).reshape(m, d)
    full = pl.BlockSpec((m, d), lambda i: (0, 0))
    vmem = _vmem_limit(
        [((m, d), F32, 8), ((d, 3 * d), BF16, 1), ((d, d), BF16, 1)],
        temporaries=12 * _nbytes((m, d), F32))
    y, u = pl.pallas_call(
        functools.partial(_conv_step_kernel, seq=seq),
        grid=(1,),
        in_specs=[full, full, _const_spec((1, d)), _layer_spec(win, layer), _layer_spec(wc, layer),
                  _layer_spec(wout, layer)],
        out_specs=[full, full],
        out_shape=[jax.ShapeDtypeStruct((m, d), F32), jax.ShapeDtypeStruct((m, d), F32)],
        compiler_params=pltpu.CompilerParams(dimension_semantics=("arbitrary",), vmem_limit_bytes=vmem),
        name="conv_mixer_step",
    )(x, pre, g, win, wc, wout)
    return y, u.reshape(m // seq, seq, d)[:, seq - taps:, :]


def _rope_angles(inv_ref, pos):
    ang = pos.astype(F32) * inv_ref[...]
    return jnp.cos(ang), jnp.sin(ang)


def _rotate(z, cos, sin):
    half = z.shape[1] // 2
    x1, x2 = z[:, :half], z[:, half:]
    return jnp.concatenate([x1 * cos - x2 * sin, x1 * sin + x2 * cos], axis=1)


def _proj_head(hn, win_ref, h, cos, sin):
    ret_in = win_ref.shape[1]
    qk, vd = ret_in // 6, ret_in // 3
    dk, dv = qk // N_HEADS, vd // N_HEADS
    q = _rotate(_dot(hn, win_ref[:, h * dk:(h + 1) * dk]), cos, sin)
    k = _rotate(_dot(hn, win_ref[:, qk + h * dk:qk + (h + 1) * dk]), cos, sin) * dk ** -0.5
    v = _dot(hn, win_ref[:, 2 * qk + h * dv:2 * qk + (h + 1) * dv])
    sg = jax.nn.silu(_dot(hn, win_ref[:, 2 * qk + vd + h * dv:2 * qk + vd + (h + 1) * dv]))
    return q, k, v, sg


def _gn_gate(o, sg, gn):
    mu = jnp.mean(o, axis=-1, keepdims=True)
    var = jnp.mean(jnp.square(o - mu), axis=-1, keepdims=True)
    of = ((o - mu) * lax.rsqrt(var + GN_EPS)) * gn
    return (sg * of).astype(BF16)


def _rope_table_kernel(inv_ref, cos_ref, sin_ref):
    pos = lax.broadcasted_iota(jnp.int32, cos_ref.shape, 0)
    cos_ref[...], sin_ref[...] = _rope_angles(inv_ref, pos)


def _rope_tables(inv, seq):
    out = jax.ShapeDtypeStruct((seq, inv.shape[1]), F32)
    return pl.pallas_call(_rope_table_kernel, out_shape=[out, out], name="rope_tables")(inv)


def _ret_layer_seq_kernel(x_ref, g_ref, win_ref, cos_ref, sin_ref, gn_ref, wout_ref, y_ref, s_ref, act_ref):
    tm = x_ref.shape[1]
    dv = gn_ref.shape[1] // N_HEADS
    c_len = SEQ_CHUNK

    @pl.when(pl.program_id(1) == 0)
    def _():
        s_ref[...] = jnp.zeros(s_ref.shape, F32)

    i = lax.broadcasted_iota(jnp.int32, (c_len, c_len), 0)
    j = lax.broadcasted_iota(jnp.int32, (c_len, c_len), 1)
    diff = (i - j).astype(F32)
    idx = lax.broadcasted_iota(jnp.int32, (c_len, 1), 0).astype(F32)
    for c in range(tm // c_len):
        rows = slice(c * c_len, (c + 1) * c_len)
        x = x_ref[0, rows, :]
        hn = _rmsnorm(x, g_ref[...]).astype(BF16)
        cos, sin = cos_ref[rows, :], sin_ref[rows, :]
        for h in range(N_HEADS):
            lg = _head_log_decay(h)
            decay = jnp.where(diff >= 0, jnp.exp(jnp.maximum(diff, 0.0) * lg), 0.0)
            cross_w = jnp.exp((idx + 1.0) * lg)
            state_w = jnp.exp((c_len - 1.0 - idx) * lg)
            chunk_decay = math.exp(c_len * lg)
            q, k, v, sg = _proj_head(hn, win_ref, h, cos, sin)
            qc, vc = q.astype(BF16), v.astype(BF16)
            cols = slice(h * dv, (h + 1) * dv)
            s = s_ref[0, h]
            scores = _dot_nt(qc, k.astype(BF16)) * decay
            o = _dot(scores.astype(BF16), vc) + _dot(qc, s.astype(BF16)) * cross_w
            s_ref[0, h] = s * chunk_decay + _dot_tn((k * state_w).astype(BF16), vc)
            act_ref[rows, cols] = _gn_gate(o, sg, gn_ref[:, cols])
        y_ref[0, rows, :] = x + _dot(act_ref[rows, :], wout_ref[...])


def _ret_layer_seq(x, g, win, inv, gn, wout, *, layer):
    bsz, seq, d = x.shape
    ret_in, vd = win.shape[2], wout.shape[1]
    dk, dv = ret_in // 6 // N_HEADS, vd // N_HEADS
    tm = ROW_TILE
    assert seq % tm == 0 and tm % SEQ_CHUNK == 0
    cos, sin = _rope_tables(inv, seq)
    row = pl.BlockSpec((1, tm, d), lambda b, t: (b, t, 0))
    rope = pl.BlockSpec((tm, dk // 2), lambda b, t: (t, 0))
    vmem = _vmem_limit(
        [((tm, d), F32, 4), ((d, ret_in), BF16, 1), ((vd, d), BF16, 1), ((tm, dk), F32, 4),
         ((N_HEADS, dk, dv), F32, 2), ((tm, vd), BF16, 1)],
        temporaries=8 * _nbytes((tm, d), F32))
    return pl.pallas_call(
        _ret_layer_seq_kernel,
        grid=(bsz, seq // tm),
        in_specs=[row, _const_spec((1, d)), _layer_spec(win, layer), rope, rope, _const_spec((1, vd)),
                  _layer_spec(wout, layer)],
        out_specs=[row, pl.BlockSpec((1, N_HEADS, dk, dv), lambda b, t: (b, 0, 0, 0))],
        out_shape=[jax.ShapeDtypeStruct((bsz, seq, d), F32),
                   jax.ShapeDtypeStruct((bsz, N_HEADS, dk, dv), F32)],
        scratch_shapes=[pltpu.VMEM((tm, vd), BF16)],
        compiler_params=pltpu.CompilerParams(dimension_semantics=("arbitrary", "arbitrary"),
                                             vmem_limit_bytes=vmem),
        name="ret_layer_seq",
    )(x, g, win, cos, sin, gn, wout)


def _ret_proj_kernel(x_ref, g_ref, win_ref, inv_ref, q_ref, k_ref, v_ref, sg_ref, *, seq, pos0):
    tm = x_ref.shape[0]
    dk, dv = q_ref.shape[1] // N_HEADS, v_ref.shape[1] // N_HEADS
    hn = _rmsnorm(x_ref[...], g_ref[...]).astype(BF16)
    row = lax.broadcasted_iota(jnp.int32, (tm, dk // 2), 0)
    cos, sin = _rope_angles(inv_ref, pos0 + lax.rem(row, seq))
    for h in range(N_HEADS):
        q, k, v, sg = _proj_head(hn, win_ref, h, cos, sin)
        q_ref[:, h * dk:(h + 1) * dk] = q
        k_ref[:, h * dk:(h + 1) * dk] = k
        v_ref[:, h * dv:(h + 1) * dv] = v
        sg_ref[:, h * dv:(h + 1) * dv] = sg


def _ret_proj(x, g, win, inv, *, layer, seq, pos0):
    m, d = x.shape
    ret_in = win.shape[2]
    qk, vd = ret_in // 6, ret_in // 3
    tm = ROW_TILE
    assert m % tm == 0 and tm % seq == 0
    row = lambda n: pl.BlockSpec((tm, n), lambda i: (i, 0))
    vmem = _vmem_limit(
        [((tm, d), F32, 2), ((d, ret_in), BF16, 1), ((tm, qk), F32, 4), ((tm, vd), F32, 4)],
        temporaries=8 * _nbytes((tm, d), F32))
    return pl.pallas_call(
        functools.partial(_ret_proj_kernel, seq=seq, pos0=pos0),
        grid=(m // tm,),
        in_specs=[row(d), _const_spec((1, d)), _layer_spec(win, layer), _const_spec(inv.shape)],
        out_specs=[row(qk), row(qk), row(vd), row(vd)],
        out_shape=[jax.ShapeDtypeStruct((m, qk), F32), jax.ShapeDtypeStruct((m, qk), F32),
                   jax.ShapeDtypeStruct((m, vd), F32), jax.ShapeDtypeStruct((m, vd), F32)],
        compiler_params=pltpu.CompilerParams(dimension_semantics=("arbitrary",), vmem_limit_bytes=vmem),
        name="ret_proj",
    )(x, g, win, inv)


def _ret_step_kernel(q_ref, k_ref, v_ref, s0_ref, o_ref, s_ref, *, seq):
    rows, qk = q_ref.shape
    n_seq = rows // seq
    dk = qk // N_HEADS
    dv = v_ref.shape[1] // N_HEADS
    r = lax.broadcasted_iota(jnp.int32, (rows, 1), 0)
    t = lax.rem(r, seq)
    tf = t.astype(F32)
    for h in range(N_HEADS):
        lg = _head_log_decay(h)
        q = q_ref[:, h * dk:(h + 1) * dk]
        k = k_ref[:, h * dk:(h + 1) * dk]
        v = v_ref[:, h * dv:(h + 1) * dv]
        acc = jnp.zeros((rows, dv), F32)
        for dlt in range(seq):
            kd = k if dlt == 0 else pltpu.roll(k, dlt, axis=0)
            vd = v if dlt == 0 else pltpu.roll(v, dlt, axis=0)
            sc = jnp.sum(q * kd, axis=-1, keepdims=True) * math.exp(dlt * lg)
            acc = acc + jnp.where(t >= dlt, sc, 0.0) * vd
        cross_w = jnp.exp((tf + 1.0) * lg)
        state_w = jnp.exp((seq - 1.0 - tf) * lg)
        qb = q.astype(BF16)
        kw = k * state_w
        vb = v.astype(BF16)
        cross = jnp.zeros((rows, dv), F32)
        for b in range(n_seq):
            own = (r >= b * seq) & (r < (b + 1) * seq)
            s0 = s0_ref[b, h]
            cross = cross + jnp.where(own, _dot(qb, s0.astype(BF16)), 0.0)
            kwb = jnp.where(own, kw, 0.0).astype(BF16)
            s_ref[b, h] = s0 * math.exp(seq * lg) + _dot_tn(kwb, vb)
        o_ref[:, h * dv:(h + 1) * dv] = acc + cross * cross_w


def _ret_step(q, k, v, s0, *, layer, seq):
    m, qk = q.shape
    vd = v.shape[1]
    _, bsz, _, dk, dv = s0.shape
    assert seq <= SEQ_CHUNK and STEP_ROWS % seq == 0 and m % STEP_ROWS == 0 and m == bsz * seq
    nb = STEP_ROWS // seq
    row = lambda n: pl.BlockSpec((STEP_ROWS, n), lambda i: (i, 0))
    vmem = _vmem_limit([((nb, N_HEADS, dk, dv), F32, 4), ((STEP_ROWS, vd), F32, 8)],
                       temporaries=16 * _nbytes((dk, dv), F32))
    return pl.pallas_call(
        functools.partial(_ret_step_kernel, seq=seq),
        grid=(m // STEP_ROWS,),
        in_specs=[row(qk), row(qk), row(vd),
                  pl.BlockSpec((None, nb, N_HEADS, dk, dv), lambda i: (layer, i, 0, 0, 0))],
        out_specs=[row(vd), pl.BlockSpec((nb, N_HEADS, dk, dv), lambda i: (i, 0, 0, 0))],
        out_shape=[jax.ShapeDtypeStruct((m, vd), F32), jax.ShapeDtypeStruct(s0.shape[1:], F32)],
        compiler_params=pltpu.CompilerParams(dimension_semantics=("arbitrary",), vmem_limit_bytes=vmem),
        name="ret_step",
    )(q, k, v, s0)


def _ret_out_kernel(x_ref, o_ref, sg_ref, gn_ref, wout_ref, y_ref, act_ref):
    dv = o_ref.shape[1] // N_HEADS
    for h in range(N_HEADS):
        cols = slice(h * dv, (h + 1) * dv)
        act_ref[:, cols] = _gn_gate(o_ref[:, cols], sg_ref[:, cols], gn_ref[:, cols])
    y_ref[...] = x_ref[...] + _dot(act_ref[...], wout_ref[...])


def _ret_out(x, o, sg, gn, wout, *, layer):
    m, d = x.shape
    vd = o.shape[1]
    tm = ROW_TILE
    assert m % tm == 0
    row = lambda n: pl.BlockSpec((tm, n), lambda i: (i, 0))
    vmem = _vmem_limit(
        [((tm, d), F32, 4), ((tm, vd), F32, 4), ((vd, d), BF16, 1), ((tm, vd), BF16, 1)],
        temporaries=6 * _nbytes((tm, d), F32))
    return pl.pallas_call(
        _ret_out_kernel,
        grid=(m // tm,),
        in_specs=[row(d), row(vd), row(vd), _const_spec((1, vd)), _layer_spec(wout, layer)],
        out_specs=row(d),
        out_shape=jax.ShapeDtypeStruct((m, d), F32),
        scratch_shapes=[pltpu.VMEM((tm, vd), BF16)],
        compiler_params=pltpu.CompilerParams(dimension_semantics=("arbitrary",), vmem_limit_bytes=vmem),
        name="ret_out",
    )(x, o, sg, gn, wout)


def kernel(x_prompt, x_sample, state_conv, state_ret, norm_mix, norm_ffn, conv_w_in, conv_w, conv_w_out,
           ret_w_in, ret_gn, ret_w_out, ffn_w_gate, ffn_w_up, ffn_w_down, final_norm):
    bp, lp, d = x_prompt.shape
    bs, ls, _ = x_sample.shape
    depth = norm_mix.shape[0]
    dk = ret_w_in.shape[2] // 6 // N_HEADS
    inv = (ROPE_BASE ** (-jnp.arange(dk // 2, dtype=F32) / (dk // 2)))[None, :]
    fin = final_norm[None, :]
    conv_w_in, conv_w_out, ret_w_in, ret_w_out, ffn_w_gate, ffn_w_up, ffn_w_down = (
        w.astype(BF16) for w in (conv_w_in, conv_w_out, ret_w_in, ret_w_out, ffn_w_gate, ffn_w_up, ffn_w_down))

    xp = x_prompt
    xs = x_sample.reshape(bs * ls, d)
    conv_p, conv_s, ret_p, ret_s = [], [], [], []
    for i in range(depth):
        j = i // N_MIXERS
        g_mix = norm_mix[i][None, :]
        if i % N_MIXERS == 0:
            xp, st = _conv_mixer_seq(xp, g_mix, conv_w_in, conv_w, conv_w_out, layer=j)
            conv_p.append(st)
            xs, st = _conv_mixer_step(xs, state_conv[j], g_mix, conv_w_in, conv_w, conv_w_out, layer=j, seq=ls)
            conv_s.append(st)
        else:
            gn = ret_gn[j][None, :]
            xp, st = _ret_layer_seq(xp, g_mix, ret_w_in, inv, gn, ret_w_out, layer=j)
            ret_p.append(st)
            q, k, v, sg = _ret_proj(xs, g_mix, ret_w_in, inv, layer=j, seq=ls, pos0=PAST_LEN)
            o, st = _ret_step(q, k, v, state_ret, layer=j, seq=ls)
            ret_s.append(st)
            xs = _ret_out(xs, o, sg, gn, ret_w_out, layer=j)
        last = i == depth - 1
        g_ffn = norm_ffn[i][None, :]
        ffn = functools.partial(_ffn, g=g_ffn, wg=ffn_w_gate, wu=ffn_w_up, wd=ffn_w_down, fin=fin,
                                layer=i, final_norm=last)
        xp = ffn(xp.reshape(bp * lp, d)).reshape(bp, lp, d)
        xs = ffn(xs)
    return (xp, xs.reshape(bs, ls, d), jnp.stack(conv_p), jnp.stack(conv_s), jnp.stack(ret_p),
            jnp.stack(ret_s))
```

```python
import functools
import math

import jax
import jax.numpy as jnp
from jax import lax
from jax.experimental import pallas as pl
from jax.experimental.pallas import tpu as pltpu

F32 = jnp.float32
BF16 = jnp.bfloat16

N_MIXERS = 2
CONV_WIDTH = 3
N_HEADS = 4
PAST_LEN = 16384
RMS_EPS = 1e-6
GN_EPS = 1e-6
ROPE_BASE = 10000.0

V7X_VMEM_LIMIT_CAP = 60000 * 1024
V7X_MXU_DIM = 256
SUBLANES = 8
ROW_TILE = 512
FF_CHUNK = V7X_MXU_DIM
SEQ_CHUNK = V7X_MXU_DIM
STEP_ROWS = 16


def _nbytes(shape, dtype):
    return math.prod(shape) * jnp.dtype(dtype).itemsize


def _vmem_limit(buffers, temporaries):
    need = sum(_nbytes(s, d) * n for s, d, n in buffers) + temporaries
    return min(int(need * 1.25) + (4 << 20), V7X_VMEM_LIMIT_CAP)


def _const_spec(shape):
    return pl.BlockSpec(shape, lambda *_: (0,) * len(shape), pipeline_mode=pl.Buffered(1))


def _layer_spec(w, layer):
    return pl.BlockSpec((None,) + w.shape[1:], lambda *_: (layer, 0, 0), pipeline_mode=pl.Buffered(1))


def _rmsnorm(x, g):
    ms = jnp.mean(x * x, axis=-1, keepdims=True)
    return (x * lax.rsqrt(ms + RMS_EPS)) * g


def _dot(a, b):
    return jnp.dot(a, b, preferred_element_type=F32)


def _dot_nt(a, b):
    return lax.dot_general(a, b, (((1,), (1,)), ((), ())), preferred_element_type=F32)


def _dot_tn(a, b):
    return lax.dot_general(a, b, (((0,), (0,)), ((), ())), preferred_element_type=F32)


def _head_log_decay(h):
    return math.log(1.0 - 2.0 ** (-5.0 - h))


def _ffn_rows(x_ref, g_ref, wg_ref, wu_ref, wd_ref, fin_ref, o_ref, act_ref, rows, *, final_norm,
              after_chunk=None):
    x = x_ref[rows, :]
    n = _rmsnorm(x, g_ref[...]).astype(BF16)
    for j in range(wg_ref.shape[1] // FF_CHUNK):
        cols = slice(j * FF_CHUNK, (j + 1) * FF_CHUNK)
        gate = _dot(n, wg_ref[:, cols])
        up = _dot(n, wu_ref[:, cols])
        act_ref[rows, cols] = (jax.nn.silu(gate) * up).astype(BF16)
        if after_chunk is not None:
            after_chunk(j)
    y = x + _dot(act_ref[rows, :], wd_ref[...])
    if final_norm:
        y = _rmsnorm(y, fin_ref[...])
    o_ref[rows, :] = y


def _ffn_kernel(x_ref, g_ref, wg_ref, wu_ref, wd_ref, fin_ref, o_ref, act_ref, *, final_norm):
    for r0 in range(0, x_ref.shape[0], ROW_TILE):
        _ffn_rows(x_ref, g_ref, wg_ref, wu_ref, wd_ref, fin_ref, o_ref, act_ref, slice(r0, r0 + ROW_TILE),
                  final_norm=final_norm)


def _ffn(x, g, wg, wu, wd, fin, *, layer, final_norm):
    m, d = x.shape
    f = wg.shape[2]
    tm = min(2 * ROW_TILE, m)
    assert m % tm == 0 and tm % ROW_TILE == 0 and f % FF_CHUNK == 0
    row = pl.BlockSpec((tm, d), lambda i: (i, 0))
    vmem = _vmem_limit(
        [((tm, d), F32, 4), ((d, f), BF16, 2), ((f, d), BF16, 1), ((tm, f), BF16, 1)],
        temporaries=6 * _nbytes((tm, d), F32))
    return pl.pallas_call(
        functools.partial(_ffn_kernel, final_norm=final_norm),
        grid=(m // tm,),
        in_specs=[row, _const_spec((1, d)), _layer_spec(wg, layer), _layer_spec(wu, layer),
                  _layer_spec(wd, layer), _const_spec((1, d))],
        out_specs=row,
        out_shape=jax.ShapeDtypeStruct((m, d), F32),
        scratch_shapes=[pltpu.VMEM((tm, f), BF16)],
        compiler_params=pltpu.CompilerParams(dimension_semantics=("arbitrary",), vmem_limit_bytes=vmem),
        name="ffn_final" if final_norm else "ffn",
    )(x, g, wg, wu, wd, fin)


def _conv_front(x, g_ref, win_ref):
    d = x.shape[1]
    hn = _rmsnorm(x, g_ref[...]).astype(BF16)
    b = _dot(hn, win_ref[:, 0:d])
    c = _dot(hn, win_ref[:, d:2 * d])
    h = _dot(hn, win_ref[:, 2 * d:3 * d])
    return b, c * h


def _conv_seq_kernel(x_ref, g_ref, win_ref, wc_ref, wout_ref, o_ref, st_ref, ubuf_ref):
    tm, d = x_ref.shape[1:]

    @pl.when(pl.program_id(1) == 0)
    def _():
        ubuf_ref[0:SUBLANES, :] = jnp.zeros((SUBLANES, d), F32)

    for r0 in range(0, tm, ROW_TILE):
        x = x_ref[0, r0:r0 + ROW_TILE, :]
        b, u = _conv_front(x, g_ref, win_ref)
        ubuf_ref[SUBLANES + r0:SUBLANES + r0 + ROW_TILE, :] = u
        y = (wc_ref[0:1, :] * ubuf_ref[SUBLANES - 2 + r0:SUBLANES - 2 + r0 + ROW_TILE, :]
             + wc_ref[1:2, :] * ubuf_ref[SUBLANES - 1 + r0:SUBLANES - 1 + r0 + ROW_TILE, :]
             + wc_ref[2:3, :] * u)
        o_ref[0, r0:r0 + ROW_TILE, :] = x + _dot((b * y).astype(BF16), wout_ref[...])
    st_ref[0] = ubuf_ref[SUBLANES + tm - (CONV_WIDTH - 1):SUBLANES + tm, :]
    ubuf_ref[0:SUBLANES, :] = ubuf_ref[tm:tm + SUBLANES, :]


def _conv_mixer_seq(x, g, win, wc, wout, *, layer):
    bsz, seq, d = x.shape
    tm = 2 * ROW_TILE
    assert seq % tm == 0 and wc.shape[1] == CONV_WIDTH == 3
    row = pl.BlockSpec((1, tm, d), lambda b, t: (b, t, 0))
    vmem = _vmem_limit(
        [((tm, d), F32, 4), ((d, 3 * d), BF16, 1), ((d, d), BF16, 1), ((tm + SUBLANES, d), F32, 1)],
        temporaries=10 * _nbytes((tm, d), F32))
    return pl.pallas_call(
        _conv_seq_kernel,
        grid=(bsz, seq // tm),
        in_specs=[row, _const_spec((1, d)), _layer_spec(win, layer), _layer_spec(wc, layer),
                  _layer_spec(wout, layer)],
        out_specs=[row, pl.BlockSpec((1, CONV_WIDTH - 1, d), lambda b, t: (b, 0, 0))],
        out_shape=[jax.ShapeDtypeStruct((bsz, seq, d), F32),
                   jax.ShapeDtypeStruct((bsz, CONV_WIDTH - 1, d), F32)],
        scratch_shapes=[pltpu.VMEM((tm + SUBLANES, d), F32)],
        compiler_params=pltpu.CompilerParams(dimension_semantics=("arbitrary", "arbitrary"),
                                             vmem_limit_bytes=vmem),
        name="conv_mixer_seq",
    )(x, g, win, wc, wout)


def _conv_step_kernel(x_ref, pre_ref, g_ref, win_ref, wc_ref, wout_ref, o_ref, u_ref, *, seq):
    x = x_ref[...]
    m = x.shape[0]
    b, u = _conv_front(x, g_ref, win_ref)
    pre = pre_ref[...]
    t = lax.rem(lax.broadcasted_iota(jnp.int32, (m, 1), 0), seq)
    back1 = jnp.where(t >= 1, pltpu.roll(u, 1, axis=0), pltpu.roll(pre, m - 1, axis=0))
    back2 = jnp.where(t >= 2, pltpu.roll(u, 2, axis=0), pre)
    y = wc_ref[0:1, :] * back2 + wc_ref[1:2, :] * back1 + wc_ref[2:3, :] * u
    o_ref[...] = x + _dot((b * y).astype(BF16), wout_ref[...])
    u_ref[...] = u


def _conv_mixer_step(x, buf, g, win, wc, wout, *, layer, seq):
    m, d = x.shape
    taps = CONV_WIDTH - 1
    assert wc.shape[1] == CONV_WIDTH == 3 and seq >= taps and m % seq == 0
    pre = jnp.pad(buf, ((0, 0), (0, seq - taps), (0, 0))).reshape(m, d)
    full = pl.BlockSpec((m, d), lambda i: (0, 0))
    vmem = _vmem_limit(
        [((m, d), F32, 8), ((d, 3 * d), BF16, 1), ((d, d), BF16, 1)],
        temporaries=12 * _nbytes((m, d), F32))
    y, u = pl.pallas_call(
        functools.partial(_conv_step_kernel, seq=seq),
        grid=(1,),
        in_specs=[full, full, _const_spec((1, d)), _layer_spec(win, layer), _layer_spec(wc, layer),
                  _layer_spec(wout, layer)],
        out_specs=[full, full],
        out_shape=[jax.ShapeDtypeStruct((m, d), F32), jax.ShapeDtypeStruct((m, d), F32)],
        compiler_params=pltpu.CompilerParams(dimension_semantics=("arbitrary",), vmem_limit_bytes=vmem),
        name="conv_mixer_step",
    )(x, pre, g, win, wc, wout)
    return y, u.reshape(m // seq, seq, d)[:, seq - taps:, :]


def _rope_angles(inv_ref, pos):
    ang = pos.astype(F32) * inv_ref[...]
    return jnp.cos(ang), jnp.sin(ang)


def _rotate(z, cos, sin):
    half = z.shape[1] // 2
    x1, x2 = z[:, :half], z[:, half:]
    return jnp.concatenate([x1 * cos - x2 * sin, x1 * sin + x2 * cos], axis=1)


def _proj_head(hn, win_ref, h, cos, sin):
    ret_in = win_ref.shape[1]
    qk, vd = ret_in // 6, ret_in // 3
    dk, dv = qk // N_HEADS, vd // N_HEADS
    q = _rotate(_dot(hn, win_ref[:, h * dk:(h + 1) * dk]), cos, sin)
    k = _rotate(_dot(hn, win_ref[:, qk + h * dk:qk + (h + 1) * dk]), cos, sin) * dk ** -0.5
    v = _dot(hn, win_ref[:, 2 * qk + h * dv:2 * qk + (h + 1) * dv])
    sg = jax.nn.silu(_dot(hn, win_ref[:, 2 * qk + vd + h * dv:2 * qk + vd + (h + 1) * dv]))
    return q, k, v, sg


def _gn_gate(o, sg, gn):
    mu = jnp.mean(o, axis=-1, keepdims=True)
    var = jnp.mean(jnp.square(o - mu), axis=-1, keepdims=True)
    of = ((o - mu) * lax.rsqrt(var + GN_EPS)) * gn
    return (sg * of).astype(BF16)


def _rope_table_kernel(inv_ref, cos_ref, sin_ref):
    pos = lax.broadcasted_iota(jnp.int32, cos_ref.shape, 0)
    cos_ref[...], sin_ref[...] = _rope_angles(inv_ref, pos)


def _rope_tables(inv, seq):
    out = jax.ShapeDtypeStruct((seq, inv.shape[1]), F32)
    return pl.pallas_call(_rope_table_kernel, out_shape=[out, out], name="rope_tables")(inv)


def _ret_layer_seq_kernel(x_ref, g_ref, win_ref, cos_ref, sin_ref, gn_ref, wout_ref, y_ref, s_ref, act_ref):
    tm = x_ref.shape[1]
    dv = gn_ref.shape[1] // N_HEADS
    c_len = SEQ_CHUNK

    @pl.when(pl.program_id(1) == 0)
    def _():
        s_ref[...] = jnp.zeros(s_ref.shape, F32)

    i = lax.broadcasted_iota(jnp.int32, (c_len, c_len), 0)
    j = lax.broadcasted_iota(jnp.int32, (c_len, c_len), 1)
    diff = (i - j).astype(F32)
    idx = lax.broadcasted_iota(jnp.int32, (c_len, 1), 0).astype(F32)
    for c in range(tm // c_len):
        rows = slice(c * c_len, (c + 1) * c_len)
        x = x_ref[0, rows, :]
        hn = _rmsnorm(x, g_ref[...]).astype(BF16)
        cos, sin = cos_ref[rows, :], sin_ref[rows, :]
        for h in range(N_HEADS):
            lg = _head_log_decay(h)
            decay = jnp.where(diff >= 0, jnp.exp(jnp.maximum(diff, 0.0) * lg), 0.0)
            cross_w = jnp.exp((idx + 1.0) * lg)
            state_w = jnp.exp((c_len - 1.0 - idx) * lg)
            chunk_decay = math.exp(c_len * lg)
            q, k, v, sg = _proj_head(hn, win_ref, h, cos, sin)
            qc, vc = q.astype(BF16), v.astype(BF16)
            cols = slice(h * dv, (h + 1) * dv)
            s = s_ref[0, h]
            scores = _dot_nt(qc, k.astype(BF16)) * decay
            o = _dot(scores.astype(BF16), vc) + _dot(qc, s.astype(BF16)) * cross_w
            s_ref[0, h] = s * chunk_decay + _dot_tn((k * state_w).astype(BF16), vc)
            act_ref[rows, cols] = _gn_gate(o, sg, gn_ref[:, cols])
        y_ref[0, rows, :] = x + _dot(act_ref[rows, :], wout_ref[...])


def _ret_layer_seq(x, g, win, inv, gn, wout, *, layer):
    bsz, seq, d = x.shape
    ret_in, vd = win.shape[2], wout.shape[1]
    dk, dv = ret_in // 6 // N_HEADS, vd // N_HEADS
    tm = ROW_TILE
    assert seq % tm == 0 and tm % SEQ_CHUNK == 0
    cos, sin = _rope_tables(inv, seq)
    row = pl.BlockSpec((1, tm, d), lambda b, t: (b, t, 0))
    rope = pl.BlockSpec((tm, dk // 2), lambda b, t: (t, 0))
    vmem = _vmem_limit(
        [((tm, d), F32, 4), ((d, ret_in), BF16, 1), ((vd, d), BF16, 1), ((tm, dk), F32, 4),
         ((N_HEADS, dk, dv), F32, 2), ((tm, vd), BF16, 1)],
        temporaries=8 * _nbytes((tm, d), F32))
    return pl.pallas_call(
        _ret_layer_seq_kernel,
        grid=(bsz, seq // tm),
        in_specs=[row, _const_spec((1, d)), _layer_spec(win, layer), rope, rope, _const_spec((1, vd)),
                  _layer_spec(wout, layer)],
        out_specs=[row, pl.BlockSpec((1, N_HEADS, dk, dv), lambda b, t: (b, 0, 0, 0))],
        out_shape=[jax.ShapeDtypeStruct((bsz, seq, d), F32),
                   jax.ShapeDtypeStruct((bsz, N_HEADS, dk, dv), F32)],
        scratch_shapes=[pltpu.VMEM((tm, vd), BF16)],
        compiler_params=pltpu.CompilerParams(dimension_semantics=("arbitrary", "arbitrary"),
                                             vmem_limit_bytes=vmem),
        name="ret_layer_seq",
    )(x, g, win, cos, sin, gn, wout)


def _ret_proj_kernel(x_ref, g_ref, win_ref, inv_ref, q_ref, k_ref, v_ref, sg_ref, *, seq, pos0):
    tm = x_ref.shape[0]
    dk, dv = q_ref.shape[1] // N_HEADS, v_ref.shape[1] // N_HEADS
    hn = _rmsnorm(x_ref[...], g_ref[...]).astype(BF16)
    row = lax.broadcasted_iota(jnp.int32, (tm, dk // 2), 0)
    cos, sin = _rope_angles(inv_ref, pos0 + lax.rem(row, seq))
    for h in range(N_HEADS):
        q, k, v, sg = _proj_head(hn, win_ref, h, cos, sin)
        q_ref[:, h * dk:(h + 1) * dk] = q
        k_ref[:, h * dk:(h + 1) * dk] = k
        v_ref[:, h * dv:(h + 1) * dv] = v
        sg_ref[:, h * dv:(h + 1) * dv] = sg


def _ret_proj(x, g, win, inv, *, layer, seq, pos0):
    m, d = x.shape
    ret_in = win.shape[2]
    qk, vd = ret_in // 6, ret_in // 3
    tm = ROW_TILE
    assert m % tm == 0 and tm % seq == 0
    row = lambda n: pl.BlockSpec((tm, n), lambda i: (i, 0))
    vmem = _vmem_limit(
        [((tm, d), F32, 2), ((d, ret_in), BF16, 1), ((tm, qk), F32, 4), ((tm, vd), F32, 4)],
        temporaries=8 * _nbytes((tm, d), F32))
    return pl.pallas_call(
        functools.partial(_ret_proj_kernel, seq=seq, pos0=pos0),
        grid=(m // tm,),
        in_specs=[row(d), _const_spec((1, d)), _layer_spec(win, layer), _const_spec(inv.shape)],
        out_specs=[row(qk), row(qk), row(vd), row(vd)],
        out_shape=[jax.ShapeDtypeStruct((m, qk), F32), jax.ShapeDtypeStruct((m, qk), F32),
                   jax.ShapeDtypeStruct((m, vd), F32), jax.ShapeDtypeStruct((m, vd), F32)],
        compiler_params=pltpu.CompilerParams(dimension_semantics=("arbitrary",), vmem_limit_bytes=vmem),
        name="ret_proj",
    )(x, g, win, inv)


class _DecodeGroup:
    def __init__(self, q_ref, k_ref, v_ref, rows, seq):
        n = rows.stop - rows.start
        dk, dv = q_ref.shape[1] // N_HEADS, v_ref.shape[1] // N_HEADS
        self.seq = seq
        self.r = lax.broadcasted_iota(jnp.int32, (n, 1), 0)
        t = lax.rem(self.r, seq)
        tf = t.astype(F32)
        self.qb, self.kw, self.vb, self.inner, self.cross, self.cross_w = [], [], [], [], [], []
        for h in range(N_HEADS):
            lg = _head_log_decay(h)
            q = q_ref[rows, h * dk:(h + 1) * dk]
            k = k_ref[rows, h * dk:(h + 1) * dk]
            v = v_ref[rows, h * dv:(h + 1) * dv]
            acc = jnp.zeros((n, dv), F32)
            for dlt in range(seq):
                kd = k if dlt == 0 else pltpu.roll(k, dlt, axis=0)
                vd = v if dlt == 0 else pltpu.roll(v, dlt, axis=0)
                sc = jnp.sum(q * kd, axis=-1, keepdims=True) * math.exp(dlt * lg)
                acc = acc + jnp.where(t >= dlt, sc, 0.0) * vd
            self.inner.append(acc)
            self.cross.append(jnp.zeros((n, dv), F32))
            self.cross_w.append(jnp.exp((tf + 1.0) * lg))
            self.qb.append(q.astype(BF16))
            self.kw.append(k * jnp.exp((seq - 1.0 - tf) * lg))
            self.vb.append(v.astype(BF16))

    def advance(self, b, s0_ref, s_ref):
        own = (self.r >= b * self.seq) & (self.r < (b + 1) * self.seq)
        for h in range(N_HEADS):
            s0 = s0_ref[h]
            self.cross[h] = self.cross[h] + jnp.where(own, _dot(self.qb[h], s0.astype(BF16)), 0.0)
            kwb = jnp.where(own, self.kw[h], 0.0).astype(BF16)
            s_ref[h] = s0 * math.exp(self.seq * _head_log_decay(h)) + _dot_tn(kwb, self.vb[h])

    def output(self, h):
        return self.inner[h] + self.cross[h] * self.cross_w[h]


def _ffn_decode_kernel(x_ref, g_ref, wg_ref, wu_ref, wd_ref, fin_ref, q_ref, k_ref, v_ref, s0_hbm,
                       o_ref, ro_ref, s_hbm, act_ref, sin_ref, sout_ref, sem_in, sem_out,
                       *, final_norm, seq, layer):
    i = pl.program_id(0)
    last = pl.num_programs(0) - 1
    n_sub = x_ref.shape[0] // ROW_TILE
    per_group = STEP_ROWS // seq
    per_step = n_sub * per_group
    n_chunks = wg_ref.shape[1] // FF_CHUNK
    dv = v_ref.shape[1] // N_HEADS
    run_after = {(u + 1) * n_chunks // per_group - 1: u for u in range(per_group)}

    def in_copy(n, slot):
        return pltpu.make_async_copy(s0_hbm.at[layer, n], sin_ref.at[slot], sem_in.at[slot])

    def out_copy(n, slot):
        return pltpu.make_async_copy(sout_ref.at[slot], s_hbm.at[n], sem_out.at[slot])

    @pl.when(i == 0)
    def _():
        in_copy(0, 0).start()

    def advance(group, sub, b):
        l = sub * per_group + b
        n = i * per_step + l
        slot, nxt = l % 2, (l + 1) % 2
        in_copy(n, slot).wait()
        if l + 1 < per_step:
            in_copy(n + 1, nxt).start()
        else:
            pl.when(i < last)(lambda: in_copy(n + 1, nxt).start())
        if l >= 2:
            out_copy(n - 2, slot).wait()
        else:
            pl.when(i > 0)(lambda: out_copy(n - 2, slot).wait())
        group.advance(b, sin_ref.at[slot], sout_ref.at[slot])
        out_copy(n, slot).start()

    for sub in range(n_sub):
        rows16 = slice(sub * STEP_ROWS, (sub + 1) * STEP_ROWS)
        group = _DecodeGroup(q_ref, k_ref, v_ref, rows16, seq)

        def after_chunk(j, group=group, sub=sub):
            if j in run_after:
                advance(group, sub, run_after[j])

        _ffn_rows(x_ref, g_ref, wg_ref, wu_ref, wd_ref, fin_ref, o_ref, act_ref,
                  slice(sub * ROW_TILE, (sub + 1) * ROW_TILE), final_norm=final_norm, after_chunk=after_chunk)
        for h in range(N_HEADS):
            ro_ref[rows16, h * dv:(h + 1) * dv] = group.output(h)

    @pl.when(i == last)
    def _():
        for l in range(per_step - 2, per_step):
            out_copy(i * per_step + l, l % 2).wait()


def _ffn_decode(x, g, wg, wu, wd, fin, q, k, v, s0, *, layer, final_norm, state_layer, seq):
    m, d = x.shape
    f = wg.shape[2]
    ms, qk = q.shape
    vd = v.shape[1]
    _, bsz, _, dk, dv = s0.shape
    tm = 2 * ROW_TILE
    steps = m // tm
    rows_s = (tm // ROW_TILE) * STEP_ROWS
    assert m % tm == 0 and f % FF_CHUNK == 0 and f // FF_CHUNK >= STEP_ROWS // seq
    assert seq <= SEQ_CHUNK and STEP_ROWS % seq == 0 and ms == bsz * seq and ms == steps * rows_s
    assert (rows_s // seq) % 2 == 0
    row = pl.BlockSpec((tm, d), lambda i: (i, 0))
    srow = lambda n: pl.BlockSpec((rows_s, n), lambda i: (i, 0))
    hbm = pl.BlockSpec(memory_space=pl.ANY)
    state = (2, N_HEADS, dk, dv)
    vmem = _vmem_limit(
        [((tm, d), F32, 4), ((d, f), BF16, 2), ((f, d), BF16, 1), ((tm, f), BF16, 1), (state, F32, 2),
         ((rows_s, vd), F32, 8)],
        temporaries=8 * _nbytes((tm, d), F32))
    return pl.pallas_call(
        functools.partial(_ffn_decode_kernel, final_norm=final_norm, seq=seq, layer=state_layer),
        grid=(steps,),
        in_specs=[row, _const_spec((1, d)), _layer_spec(wg, layer), _layer_spec(wu, layer),
                  _layer_spec(wd, layer), _const_spec((1, d)), srow(qk), srow(qk), srow(vd), hbm],
        out_specs=[row, srow(vd), hbm],
        out_shape=[jax.ShapeDtypeStruct((m, d), F32), jax.ShapeDtypeStruct((ms, vd), F32),
                   jax.ShapeDtypeStruct(s0.shape[1:], F32)],
        scratch_shapes=[pltpu.VMEM((tm, f), BF16), pltpu.VMEM(state, F32), pltpu.VMEM(state, F32),
                        pltpu.SemaphoreType.DMA((2,)), pltpu.SemaphoreType.DMA((2,))],
        compiler_params=pltpu.CompilerParams(dimension_semantics=("arbitrary",), vmem_limit_bytes=vmem),
        name="ffn_decode_ret",
    )(x, g, wg, wu, wd, fin, q, k, v, s0)


def _ret_out_kernel(x_ref, o_ref, sg_ref, gn_ref, wout_ref, y_ref, act_ref):
    dv = o_ref.shape[1] // N_HEADS
    for h in range(N_HEADS):
        cols = slice(h * dv, (h + 1) * dv)
        act_ref[:, cols] = _gn_gate(o_ref[:, cols], sg_ref[:, cols], gn_ref[:, cols])
    y_ref[...] = x_ref[...] + _dot(act_ref[...], wout_ref[...])


def _ret_out(x, o, sg, gn, wout, *, layer):
    m, d = x.shape
    vd = o.shape[1]
    tm = ROW_TILE
    assert m % tm == 0
    row = lambda n: pl.BlockSpec((tm, n), lambda i: (i, 0))
    vmem = _vmem_limit(
        [((tm, d), F32, 4), ((tm, vd), F32, 4), ((vd, d), BF16, 1), ((tm, vd), BF16, 1)],
        temporaries=6 * _nbytes((tm, d), F32))
    return pl.pallas_call(
        _ret_out_kernel,
        grid=(m // tm,),
        in_specs=[row(d), row(vd), row(vd), _const_spec((1, vd)), _layer_spec(wout, layer)],
        out_specs=row(d),
        out_shape=jax.ShapeDtypeStruct((m, d), F32),
        scratch_shapes=[pltpu.VMEM((tm, vd), BF16)],
        compiler_params=pltpu.CompilerParams(dimension_semantics=("arbitrary",), vmem_limit_bytes=vmem),
        name="ret_out",
    )(x, o, sg, gn, wout)


def kernel(x_prompt, x_sample, state_conv, state_ret, norm_mix, norm_ffn, conv_w_in, conv_w, conv_w_out,
           ret_w_in, ret_gn, ret_w_out, ffn_w_gate, ffn_w_up, ffn_w_down, final_norm):
    bp, lp, d = x_prompt.shape
    bs, ls, _ = x_sample.shape
    depth = norm_mix.shape[0]
    dk = ret_w_in.shape[2] // 6 // N_HEADS
    inv = (ROPE_BASE ** (-jnp.arange(dk // 2, dtype=F32) / (dk // 2)))[None, :]
    fin = final_norm[None, :]
    conv_w_in, conv_w_out, ret_w_in, ret_w_out, ffn_w_gate, ffn_w_up, ffn_w_down = (
        w.astype(BF16) for w in (conv_w_in, conv_w_out, ret_w_in, ret_w_out, ffn_w_gate, ffn_w_up, ffn_w_down))

    xp = x_prompt
    xs = x_sample.reshape(bs * ls, d)
    conv_p, conv_s, ret_p, ret_s = [], [], [], []
    for i in range(depth):
        j = i // N_MIXERS
        g_mix = norm_mix[i][None, :]
        if i % N_MIXERS == 0:
            xp, st = _conv_mixer_seq(xp, g_mix, conv_w_in, conv_w, conv_w_out, layer=j)
            conv_p.append(st)
            xs, st = _conv_mixer_step(xs, state_conv[j], g_mix, conv_w_in, conv_w, conv_w_out, layer=j, seq=ls)
            conv_s.append(st)
        else:
            gn = ret_gn[j][None, :]
            xp, st = _ret_layer_seq(xp, g_mix, ret_w_in, inv, gn, ret_w_out, layer=j)
            ret_p.append(st)
            xs = _ret_out(xs, ret_o, sg, gn, ret_w_out, layer=j)
        last = i == depth - 1
        ffn = dict(g=norm_ffn[i][None, :], wg=ffn_w_gate, wu=ffn_w_up, wd=ffn_w_down, fin=fin, layer=i,
                   final_norm=last)
        xs = _ffn(xs, **ffn)
        if not last and (i + 1) % N_MIXERS == 1:
            jn = (i + 1) // N_MIXERS
            q, k, v, sg = _ret_proj(xs, norm_mix[i + 1][None, :], ret_w_in, inv, layer=jn, seq=ls, pos0=PAST_LEN)
            xp, ret_o, st = _ffn_decode(xp.reshape(bp * lp, d), q=q, k=k, v=v, s0=state_ret, state_layer=jn,
                                        seq=ls, **ffn)
            ret_s.append(st)
        else:
            xp = _ffn(xp.reshape(bp * lp, d), **ffn)
        xp = xp.reshape(bp, lp, d)
    return (xp, xs.reshape(bs, ls, d), jnp.stack(conv_p), jnp.stack(conv_s), jnp.stack(ret_p),
            jnp.stack(ret_s))
```

```python
import functools
import math

import jax
import jax.numpy as jnp
from jax import lax
from jax.experimental import pallas as pl
from jax.experimental.pallas import tpu as pltpu

F32 = jnp.float32
BF16 = jnp.bfloat16

N_MIXERS = 2
CONV_WIDTH = 3
N_HEADS = 4
PAST_LEN = 16384
RMS_EPS = 1e-6
GN_EPS = 1e-6
ROPE_BASE = 10000.0

V7X_VMEM_LIMIT_CAP = 60000 * 1024
V7X_MXU_DIM = 256
SUBLANES = 8
ROW_TILE = 512
FF_CHUNK = V7X_MXU_DIM
SEQ_CHUNK = V7X_MXU_DIM
BF16_ROWS = 16
STEP_ROWS = BF16_ROWS
CAST_CHUNKS = 16


def _nbytes(shape, dtype):
    return math.prod(shape) * jnp.dtype(dtype).itemsize


def _vmem_limit(buffers, temporaries):
    need = sum(_nbytes(s, d) * n for s, d, n in buffers) + temporaries
    return min(int(need * 1.25) + (4 << 20), V7X_VMEM_LIMIT_CAP)


def _const_spec(shape):
    return pl.BlockSpec(shape, lambda *_: (0,) * len(shape), pipeline_mode=pl.Buffered(1))


def _layer_spec(w, layer):
    return pl.BlockSpec((None,) + w.shape[1:], lambda *_: (layer, 0, 0), pipeline_mode=pl.Buffered(1))


class _CastJobs:
    def __init__(self, weights):
        self.weights = list(weights)
        self.n = len(self.weights)

    @staticmethod
    def chunk(grid_idx, grid):
        step = grid_idx[0]
        for extent, idx in zip(grid[1:], grid_idx[1:]):
            step = step * extent + idx
        return jnp.minimum(step, CAST_CHUNKS - 1)

    def specs(self, grid):
        ins, outs, shapes = [], [], []
        for w, layer in self.weights:
            rows, cols = w.shape[1:]
            assert rows % (CAST_CHUNKS * BF16_ROWS) == 0 and math.prod(grid) >= CAST_CHUNKS
            blk = (None, rows // CAST_CHUNKS, cols)
            ins.append(pl.BlockSpec(blk, lambda *g, layer=layer: (layer, self.chunk(g, grid), 0)))
            outs.append(pl.BlockSpec(blk, lambda *g: (0, self.chunk(g, grid), 0)))
            shapes.append(jax.ShapeDtypeStruct((1, rows, cols), BF16))
        return ins, outs, shapes

    def vmem(self):
        return [((w.shape[1] // CAST_CHUNKS, w.shape[2]), F32, 3) for w, _ in self.weights]

    def run(self, src_refs, dst_refs, step):
        @pl.when(step < CAST_CHUNKS)
        def _():
            for src, dst in zip(src_refs, dst_refs):
                dst[...] = src[...].astype(BF16)


def _rmsnorm(x, g):
    ms = jnp.mean(x * x, axis=-1, keepdims=True)
    return (x * lax.rsqrt(ms + RMS_EPS)) * g


def _dot(a, b):
    return jnp.dot(a, b, preferred_element_type=F32)


def _dot_nt(a, b):
    return lax.dot_general(a, b, (((1,), (1,)), ((), ())), preferred_element_type=F32)


def _dot_tn(a, b):
    return lax.dot_general(a, b, (((0,), (0,)), ((), ())), preferred_element_type=F32)


def _head_log_decay(h):
    return math.log(1.0 - 2.0 ** (-5.0 - h))


def _ffn_rows(x_ref, g_ref, wg_ref, wu_ref, wd_ref, fin_ref, o_ref, act_ref, rows, *, final_norm,
              after_chunk=None):
    x = x_ref[rows, :]
    n = _rmsnorm(x, g_ref[...]).astype(BF16)
    for j in range(wg_ref.shape[1] // FF_CHUNK):
        cols = slice(j * FF_CHUNK, (j + 1) * FF_CHUNK)
        gate = _dot(n, wg_ref[:, cols])
        up = _dot(n, wu_ref[:, cols])
        act_ref[rows, cols] = (jax.nn.silu(gate) * up).astype(BF16)
        if after_chunk is not None:
            after_chunk(j)
    y = x + _dot(act_ref[rows, :], wd_ref[...])
    if final_norm:
        y = _rmsnorm(y, fin_ref[...])
    o_ref[rows, :] = y


def _ffn_kernel(x_ref, g_ref, wg_ref, wu_ref, wd_ref, fin_ref, o_ref, act_ref, *, final_norm):
    for r0 in range(0, x_ref.shape[0], ROW_TILE):
        _ffn_rows(x_ref, g_ref, wg_ref, wu_ref, wd_ref, fin_ref, o_ref, act_ref, slice(r0, r0 + ROW_TILE),
                  final_norm=final_norm)


def _ffn(x, g, wg, wu, wd, fin, *, layer, final_norm):
    m, d = x.shape
    f = wg.shape[2]
    tm = min(2 * ROW_TILE, m)
    assert m % tm == 0 and tm % ROW_TILE == 0 and f % FF_CHUNK == 0
    row = pl.BlockSpec((tm, d), lambda i: (i, 0))
    vmem = _vmem_limit(
        [((tm, d), F32, 4), ((d, f), BF16, 2), ((f, d), BF16, 1), ((tm, f), BF16, 1)],
        temporaries=6 * _nbytes((tm, d), F32))
    return pl.pallas_call(
        functools.partial(_ffn_kernel, final_norm=final_norm),
        grid=(m // tm,),
        in_specs=[row, _const_spec((1, d)), _layer_spec(wg, layer), _layer_spec(wu, layer),
                  _layer_spec(wd, layer), _const_spec((1, d))],
        out_specs=row,
        out_shape=jax.ShapeDtypeStruct((m, d), F32),
        scratch_shapes=[pltpu.VMEM((tm, f), BF16)],
        compiler_params=pltpu.CompilerParams(dimension_semantics=("arbitrary",), vmem_limit_bytes=vmem),
        name="ffn_final" if final_norm else "ffn",
    )(x, g, wg, wu, wd, fin)


def _conv_front(x, g_ref, win_ref):
    d = x.shape[1]
    hn = _rmsnorm(x, g_ref[...]).astype(BF16)
    b = _dot(hn, win_ref[:, 0:d])
    c = _dot(hn, win_ref[:, d:2 * d])
    h = _dot(hn, win_ref[:, 2 * d:3 * d])
    return b, c * h


def _conv_seq_kernel(x_ref, g_ref, win_ref, wc_ref, wout_ref, *refs, casts):
    cast_src, (o_ref, st_ref), cast_dst, ubuf_ref = (
        refs[:casts.n], refs[casts.n:casts.n + 2], refs[casts.n + 2:-1], refs[-1])
    tm, d = x_ref.shape[1:]

    @pl.when(pl.program_id(1) == 0)
    def _():
        ubuf_ref[0:SUBLANES, :] = jnp.zeros((SUBLANES, d), F32)

    casts.run(cast_src, cast_dst, pl.program_id(0) * pl.num_programs(1) + pl.program_id(1))
    for r0 in range(0, tm, ROW_TILE):
        x = x_ref[0, r0:r0 + ROW_TILE, :]
        b, u = _conv_front(x, g_ref, win_ref)
        ubuf_ref[SUBLANES + r0:SUBLANES + r0 + ROW_TILE, :] = u
        y = (wc_ref[0:1, :] * ubuf_ref[SUBLANES - 2 + r0:SUBLANES - 2 + r0 + ROW_TILE, :]
             + wc_ref[1:2, :] * ubuf_ref[SUBLANES - 1 + r0:SUBLANES - 1 + r0 + ROW_TILE, :]
             + wc_ref[2:3, :] * u)
        o_ref[0, r0:r0 + ROW_TILE, :] = x + _dot((b * y).astype(BF16), wout_ref[...])
    st_ref[0] = ubuf_ref[SUBLANES + tm - (CONV_WIDTH - 1):SUBLANES + tm, :]
    ubuf_ref[0:SUBLANES, :] = ubuf_ref[tm:tm + SUBLANES, :]


def _conv_mixer_seq(x, g, win, wc, wout, *, layer, cast=()):
    bsz, seq, d = x.shape
    tm = 2 * ROW_TILE
    assert seq % tm == 0 and wc.shape[1] == CONV_WIDTH == 3
    grid = (bsz, seq // tm)
    casts = _CastJobs(cast)
    cast_in, cast_out, cast_shapes = casts.specs(grid)
    row = pl.BlockSpec((1, tm, d), lambda b, t: (b, t, 0))
    vmem = _vmem_limit(
        [((tm, d), F32, 4), ((d, 3 * d), BF16, 1), ((d, d), BF16, 1), ((tm + SUBLANES, d), F32, 1)]
        + casts.vmem(), temporaries=5 * _nbytes((tm, d), F32))
    y, st, *casted = pl.pallas_call(
        functools.partial(_conv_seq_kernel, casts=casts),
        grid=grid,
        in_specs=[row, _const_spec((1, d)), _layer_spec(win, layer), _layer_spec(wc, layer),
                  _layer_spec(wout, layer)] + cast_in,
        out_specs=[row, pl.BlockSpec((1, CONV_WIDTH - 1, d), lambda b, t: (b, 0, 0))] + cast_out,
        out_shape=[jax.ShapeDtypeStruct((bsz, seq, d), F32),
                   jax.ShapeDtypeStruct((bsz, CONV_WIDTH - 1, d), F32)] + cast_shapes,
        scratch_shapes=[pltpu.VMEM((tm + SUBLANES, d), F32)],
        compiler_params=pltpu.CompilerParams(dimension_semantics=("arbitrary", "arbitrary"),
                                             vmem_limit_bytes=vmem),
        name="conv_mixer_seq",
    )(x, g, win, wc, wout, *(w for w, _ in cast))
    return y, st, casted


def _conv_step_kernel(x_ref, pre_ref, g_ref, win_ref, wc_ref, wout_ref, o_ref, u_ref, *, seq):
    x = x_ref[...]
    m = x.shape[0]
    b, u = _conv_front(x, g_ref, win_ref)
    pre = pre_ref[...]
    t = lax.rem(lax.broadcasted_iota(jnp.int32, (m, 1), 0), seq)
    back1 = jnp.where(t >= 1, pltpu.roll(u, 1, axis=0), pltpu.roll(pre, m - 1, axis=0))
    back2 = jnp.where(t >= 2, pltpu.roll(u, 2, axis=0), pre)
    y = wc_ref[0:1, :] * back2 + wc_ref[1:2, :] * back1 + wc_ref[2:3, :] * u
    o_ref[...] = x + _dot((b * y).astype(BF16), wout_ref[...])
    u_ref[...] = u


def _conv_mixer_step(x, buf, g, win, wc, wout, *, layer, seq):
    m, d = x.shape
    taps = CONV_WIDTH - 1
    assert wc.shape[1] == CONV_WIDTH == 3 and seq >= taps and m % seq == 0
    pre = jnp.pad(buf, ((0, 0), (0, seq - taps), (0, 0))).reshape(m, d)
    full = pl.BlockSpec((m, d), lambda i: (0, 0))
    vmem = _vmem_limit(
        [((m, d), F32, 8), ((d, 3 * d), BF16, 1), ((d, d), BF16, 1)],
        temporaries=12 * _nbytes((m, d), F32))
    y, u = pl.pallas_call(
        functools.partial(_conv_step_kernel, seq=seq),
        grid=(1,),
        in_specs=[full, full, _const_spec((1, d)), _layer_spec(win, layer), _layer_spec(wc, layer),
                  _layer_spec(wout, layer)],
        out_specs=[full, full],
        out_shape=[jax.ShapeDtypeStruct((m, d), F32), jax.ShapeDtypeStruct((m, d), F32)],
        compiler_params=pltpu.CompilerParams(dimension_semantics=("arbitrary",), vmem_limit_bytes=vmem),
        name="conv_mixer_step",
    )(x, pre, g, win, wc, wout)
    return y, u.reshape(m // seq, seq, d)[:, seq - taps:, :]


def _rope_angles(inv_ref, pos):
    ang = pos.astype(F32) * inv_ref[...]
    return jnp.cos(ang), jnp.sin(ang)


def _rotate(z, cos, sin):
    half = z.shape[1] // 2
    x1, x2 = z[:, :half], z[:, half:]
    return jnp.concatenate([x1 * cos - x2 * sin, x1 * sin + x2 * cos], axis=1)


def _proj_head(hn, win_ref, h, cos, sin):
    ret_in = win_ref.shape[1]
    qk, vd = ret_in // 6, ret_in // 3
    dk, dv = qk // N_HEADS, vd // N_HEADS
    q = _rotate(_dot(hn, win_ref[:, h * dk:(h + 1) * dk]), cos, sin)
    k = _rotate(_dot(hn, win_ref[:, qk + h * dk:qk + (h + 1) * dk]), cos, sin) * dk ** -0.5
    v = _dot(hn, win_ref[:, 2 * qk + h * dv:2 * qk + (h + 1) * dv])
    sg = jax.nn.silu(_dot(hn, win_ref[:, 2 * qk + vd + h * dv:2 * qk + vd + (h + 1) * dv]))
    return q, k, v, sg


def _gn_gate(o, sg, gn):
    mu = jnp.mean(o, axis=-1, keepdims=True)
    var = jnp.mean(jnp.square(o - mu), axis=-1, keepdims=True)
    of = ((o - mu) * lax.rsqrt(var + GN_EPS)) * gn
    return (sg * of).astype(BF16)


def _rope_table_kernel(inv_ref, cos_ref, sin_ref):
    pos = lax.broadcasted_iota(jnp.int32, cos_ref.shape, 0)
    cos_ref[...], sin_ref[...] = _rope_angles(inv_ref, pos)


def _rope_tables(inv, seq):
    out = jax.ShapeDtypeStruct((seq, inv.shape[1]), F32)
    return pl.pallas_call(_rope_table_kernel, out_shape=[out, out], name="rope_tables")(inv)


def _ret_layer_seq_kernel(x_ref, g_ref, win_ref, cos_ref, sin_ref, gn_ref, wout_ref, *refs, casts):
    cast_src, (y_ref, s_ref), cast_dst, act_ref = (
        refs[:casts.n], refs[casts.n:casts.n + 2], refs[casts.n + 2:-1], refs[-1])
    tm = x_ref.shape[1]
    dv = gn_ref.shape[1] // N_HEADS
    c_len = SEQ_CHUNK

    @pl.when(pl.program_id(1) == 0)
    def _():
        s_ref[...] = jnp.zeros(s_ref.shape, F32)

    casts.run(cast_src, cast_dst, pl.program_id(0) * pl.num_programs(1) + pl.program_id(1))
    i = lax.broadcasted_iota(jnp.int32, (c_len, c_len), 0)
    j = lax.broadcasted_iota(jnp.int32, (c_len, c_len), 1)
    diff = (i - j).astype(F32)
    idx = lax.broadcasted_iota(jnp.int32, (c_len, 1), 0).astype(F32)
    for c in range(tm // c_len):
        rows = slice(c * c_len, (c + 1) * c_len)
        x = x_ref[0, rows, :]
        hn = _rmsnorm(x, g_ref[...]).astype(BF16)
        cos, sin = cos_ref[rows, :], sin_ref[rows, :]
        for h in range(N_HEADS):
            lg = _head_log_decay(h)
            decay = jnp.where(diff >= 0, jnp.exp(jnp.maximum(diff, 0.0) * lg), 0.0)
            cross_w = jnp.exp((idx + 1.0) * lg)
            state_w = jnp.exp((c_len - 1.0 - idx) * lg)
            chunk_decay = math.exp(c_len * lg)
            q, k, v, sg = _proj_head(hn, win_ref, h, cos, sin)
            qc, vc = q.astype(BF16), v.astype(BF16)
            cols = slice(h * dv, (h + 1) * dv)
            s = s_ref[0, h]
            scores = _dot_nt(qc, k.astype(BF16)) * decay
            o = _dot(scores.astype(BF16), vc) + _dot(qc, s.astype(BF16)) * cross_w
            s_ref[0, h] = s * chunk_decay + _dot_tn((k * state_w).astype(BF16), vc)
            act_ref[rows, cols] = _gn_gate(o, sg, gn_ref[:, cols])
        y_ref[0, rows, :] = x + _dot(act_ref[rows, :], wout_ref[...])


def _ret_layer_seq(x, g, win, inv, gn, wout, *, layer, cast=()):
    bsz, seq, d = x.shape
    ret_in, vd = win.shape[2], wout.shape[1]
    dk, dv = ret_in // 6 // N_HEADS, vd // N_HEADS
    tm = ROW_TILE
    assert seq % tm == 0 and tm % SEQ_CHUNK == 0
    grid = (bsz, seq // tm)
    casts = _CastJobs(cast)
    cast_in, cast_out, cast_shapes = casts.specs(grid)
    cos, sin = _rope_tables(inv, seq)
    row = pl.BlockSpec((1, tm, d), lambda b, t: (b, t, 0))
    rope = pl.BlockSpec((tm, dk // 2), lambda b, t: (t, 0))
    vmem = _vmem_limit(
        [((tm, d), F32, 4), ((d, ret_in), BF16, 1), ((vd, d), BF16, 1), ((tm, dk), F32, 4),
         ((N_HEADS, dk, dv), F32, 2), ((tm, vd), BF16, 1)] + casts.vmem(),
        temporaries=8 * _nbytes((tm, d), F32))
    y, st, *casted = pl.pallas_call(
        functools.partial(_ret_layer_seq_kernel, casts=casts),
        grid=grid,
        in_specs=[row, _const_spec((1, d)), _layer_spec(win, layer), rope, rope, _const_spec((1, vd)),
                  _layer_spec(wout, layer)] + cast_in,
        out_specs=[row, pl.BlockSpec((1, N_HEADS, dk, dv), lambda b, t: (b, 0, 0, 0))] + cast_out,
        out_shape=[jax.ShapeDtypeStruct((bsz, seq, d), F32),
                   jax.ShapeDtypeStruct((bsz, N_HEADS, dk, dv), F32)] + cast_shapes,
        scratch_shapes=[pltpu.VMEM((tm, vd), BF16)],
        compiler_params=pltpu.CompilerParams(dimension_semantics=("arbitrary", "arbitrary"),
                                             vmem_limit_bytes=vmem),
        name="ret_layer_seq",
    )(x, g, win, cos, sin, gn, wout, *(w for w, _ in cast))
    return y, st, casted


def _ret_proj_kernel(x_ref, g_ref, win_ref, inv_ref, q_ref, k_ref, v_ref, sg_ref, *, seq, pos0):
    tm = x_ref.shape[0]
    dk, dv = q_ref.shape[1] // N_HEADS, v_ref.shape[1] // N_HEADS
    hn = _rmsnorm(x_ref[...], g_ref[...]).astype(BF16)
    row = lax.broadcasted_iota(jnp.int32, (tm, dk // 2), 0)
    cos, sin = _rope_angles(inv_ref, pos0 + lax.rem(row, seq))
    for h in range(N_HEADS):
        q, k, v, sg = _proj_head(hn, win_ref, h, cos, sin)
        q_ref[:, h * dk:(h + 1) * dk] = q
        k_ref[:, h * dk:(h + 1) * dk] = k
        v_ref[:, h * dv:(h + 1) * dv] = v
        sg_ref[:, h * dv:(h + 1) * dv] = sg


def _ret_proj(x, g, win, inv, *, layer, seq, pos0):
    m, d = x.shape
    ret_in = win.shape[2]
    qk, vd = ret_in // 6, ret_in // 3
    tm = ROW_TILE
    assert m % tm == 0 and tm % seq == 0
    row = lambda n: pl.BlockSpec((tm, n), lambda i: (i, 0))
    vmem = _vmem_limit(
        [((tm, d), F32, 2), ((d, ret_in), BF16, 1), ((tm, qk), F32, 4), ((tm, vd), F32, 4)],
        temporaries=8 * _nbytes((tm, d), F32))
    return pl.pallas_call(
        functools.partial(_ret_proj_kernel, seq=seq, pos0=pos0),
        grid=(m // tm,),
        in_specs=[row(d), _const_spec((1, d)), _layer_spec(win, layer), _const_spec(inv.shape)],
        out_specs=[row(qk), row(qk), row(vd), row(vd)],
        out_shape=[jax.ShapeDtypeStruct((m, qk), F32), jax.ShapeDtypeStruct((m, qk), F32),
                   jax.ShapeDtypeStruct((m, vd), F32), jax.ShapeDtypeStruct((m, vd), F32)],
        compiler_params=pltpu.CompilerParams(dimension_semantics=("arbitrary",), vmem_limit_bytes=vmem),
        name="ret_proj",
    )(x, g, win, inv)


class _DecodeGroup:
    def __init__(self, q_ref, k_ref, v_ref, rows, seq):
        n = rows.stop - rows.start
        dk, dv = q_ref.shape[1] // N_HEADS, v_ref.shape[1] // N_HEADS
        self.seq = seq
        self.r = lax.broadcasted_iota(jnp.int32, (n, 1), 0)
        t = lax.rem(self.r, seq)
        tf = t.astype(F32)
        self.qb, self.kw, self.vb, self.inner, self.cross, self.cross_w = [], [], [], [], [], []
        for h in range(N_HEADS):
            lg = _head_log_decay(h)
            q = q_ref[rows, h * dk:(h + 1) * dk]
            k = k_ref[rows, h * dk:(h + 1) * dk]
            v = v_ref[rows, h * dv:(h + 1) * dv]
            acc = jnp.zeros((n, dv), F32)
            for dlt in range(seq):
                kd = k if dlt == 0 else pltpu.roll(k, dlt, axis=0)
                vd = v if dlt == 0 else pltpu.roll(v, dlt, axis=0)
                sc = jnp.sum(q * kd, axis=-1, keepdims=True) * math.exp(dlt * lg)
                acc = acc + jnp.where(t >= dlt, sc, 0.0) * vd
            self.inner.append(acc)
            self.cross.append(jnp.zeros((n, dv), F32))
            self.cross_w.append(jnp.exp((tf + 1.0) * lg))
            self.qb.append(q.astype(BF16))
            self.kw.append(k * jnp.exp((seq - 1.0 - tf) * lg))
            self.vb.append(v.astype(BF16))

    def advance(self, b, s0_ref, s_ref):
        own = (self.r >= b * self.seq) & (self.r < (b + 1) * self.seq)
        for h in range(N_HEADS):
            s0 = s0_ref[h]
            self.cross[h] = self.cross[h] + jnp.where(own, _dot(self.qb[h], s0.astype(BF16)), 0.0)
            kwb = jnp.where(own, self.kw[h], 0.0).astype(BF16)
            s_ref[h] = s0 * math.exp(self.seq * _head_log_decay(h)) + _dot_tn(kwb, self.vb[h])

    def output(self, h):
        return self.inner[h] + self.cross[h] * self.cross_w[h]


def _ffn_decode_kernel(x_ref, g_ref, wg_ref, wu_ref, wd_ref, fin_ref, q_ref, k_ref, v_ref, s0_hbm,
                       o_ref, ro_ref, s_hbm, act_ref, sin_ref, sout_ref, sem_in, sem_out,
                       *, final_norm, seq, layer):
    i = pl.program_id(0)
    last = pl.num_programs(0) - 1
    n_sub = x_ref.shape[0] // ROW_TILE
    per_group = STEP_ROWS // seq
    per_step = n_sub * per_group
    n_chunks = wg_ref.shape[1] // FF_CHUNK
    dv = v_ref.shape[1] // N_HEADS
    run_after = {(u + 1) * n_chunks // per_group - 1: u for u in range(per_group)}

    def in_copy(n, slot):
        return pltpu.make_async_copy(s0_hbm.at[layer, n], sin_ref.at[slot], sem_in.at[slot])

    def out_copy(n, slot):
        return pltpu.make_async_copy(sout_ref.at[slot], s_hbm.at[n], sem_out.at[slot])

    @pl.when(i == 0)
    def _():
        in_copy(0, 0).start()

    def advance(group, sub, b):
        l = sub * per_group + b
        n = i * per_step + l
        slot, nxt = l % 2, (l + 1) % 2
        if l >= 1:
            out_copy(n - 1, nxt).start()
        in_copy(n, slot).wait()
        if l + 1 < per_step:
            in_copy(n + 1, nxt).start()
        else:
            pl.when(i < last)(lambda: in_copy(n + 1, nxt).start())
        if l >= 2:
            out_copy(n - 2, slot).wait()
        else:
            pl.when(i > 0)(lambda: out_copy(n - 2, slot).wait())
        group.advance(b, sin_ref.at[slot], sout_ref.at[slot])

    for sub in range(n_sub):
        rows16 = slice(sub * STEP_ROWS, (sub + 1) * STEP_ROWS)
        group = _DecodeGroup(q_ref, k_ref, v_ref, rows16, seq)

        def after_chunk(j, group=group, sub=sub):
            if j in run_after:
                advance(group, sub, run_after[j])

        _ffn_rows(x_ref, g_ref, wg_ref, wu_ref, wd_ref, fin_ref, o_ref, act_ref,
                  slice(sub * ROW_TILE, (sub + 1) * ROW_TILE), final_norm=final_norm, after_chunk=after_chunk)
        for h in range(N_HEADS):
            ro_ref[rows16, h * dv:(h + 1) * dv] = group.output(h)

    out_copy(i * per_step + per_step - 1, (per_step - 1) % 2).start()

    @pl.when(i == last)
    def _():
        for l in range(per_step - 2, per_step):
            out_copy(i * per_step + l, l % 2).wait()


def _ffn_decode(x, g, wg, wu, wd, fin, q, k, v, s0, *, layer, final_norm, state_layer, seq):
    m, d = x.shape
    f = wg.shape[2]
    ms, qk = q.shape
    vd = v.shape[1]
    _, bsz, _, dk, dv = s0.shape
    tm = 2 * ROW_TILE
    steps = m // tm
    rows_s = (tm // ROW_TILE) * STEP_ROWS
    assert m % tm == 0 and f % FF_CHUNK == 0 and f // FF_CHUNK >= STEP_ROWS // seq
    assert seq <= SEQ_CHUNK and STEP_ROWS % seq == 0 and ms == bsz * seq and ms == steps * rows_s
    assert (rows_s // seq) % 2 == 0
    row = pl.BlockSpec((tm, d), lambda i: (i, 0))
    srow = lambda n: pl.BlockSpec((rows_s, n), lambda i: (i, 0))
    hbm = pl.BlockSpec(memory_space=pl.ANY)
    state = (2, N_HEADS, dk, dv)
    vmem = _vmem_limit(
        [((tm, d), F32, 4), ((d, f), BF16, 2), ((f, d), BF16, 1), ((tm, f), BF16, 1), (state, F32, 2),
         ((rows_s, vd), F32, 8)],
        temporaries=8 * _nbytes((tm, d), F32))
    return pl.pallas_call(
        functools.partial(_ffn_decode_kernel, final_norm=final_norm, seq=seq, layer=state_layer),
        grid=(steps,),
        in_specs=[row, _const_spec((1, d)), _layer_spec(wg, layer), _layer_spec(wu, layer),
                  _layer_spec(wd, layer), _const_spec((1, d)), srow(qk), srow(qk), srow(vd), hbm],
        out_specs=[row, srow(vd), hbm],
        out_shape=[jax.ShapeDtypeStruct((m, d), F32), jax.ShapeDtypeStruct((ms, vd), F32),
                   jax.ShapeDtypeStruct(s0.shape[1:], F32)],
        scratch_shapes=[pltpu.VMEM((tm, f), BF16), pltpu.VMEM(state, F32), pltpu.VMEM(state, F32),
                        pltpu.SemaphoreType.DMA((2,)), pltpu.SemaphoreType.DMA((2,))],
        compiler_params=pltpu.CompilerParams(dimension_semantics=("arbitrary",), vmem_limit_bytes=vmem),
        name="ffn_decode_ret",
    )(x, g, wg, wu, wd, fin, q, k, v, s0)


def _ret_out_kernel(x_ref, o_ref, sg_ref, gn_ref, wout_ref, y_ref, act_ref):
    dv = o_ref.shape[1] // N_HEADS
    for h in range(N_HEADS):
        cols = slice(h * dv, (h + 1) * dv)
        act_ref[:, cols] = _gn_gate(o_ref[:, cols], sg_ref[:, cols], gn_ref[:, cols])
    y_ref[...] = x_ref[...] + _dot(act_ref[...], wout_ref[...])


def _ret_out(x, o, sg, gn, wout, *, layer):
    m, d = x.shape
    vd = o.shape[1]
    tm = ROW_TILE
    assert m % tm == 0
    row = lambda n: pl.BlockSpec((tm, n), lambda i: (i, 0))
    vmem = _vmem_limit(
        [((tm, d), F32, 4), ((tm, vd), F32, 4), ((vd, d), BF16, 1), ((tm, vd), BF16, 1)],
        temporaries=6 * _nbytes((tm, d), F32))
    return pl.pallas_call(
        _ret_out_kernel,
        grid=(m // tm,),
        in_specs=[row(d), row(vd), row(vd), _const_spec((1, vd)), _layer_spec(wout, layer)],
        out_specs=row(d),
        out_shape=jax.ShapeDtypeStruct((m, d), F32),
        scratch_shapes=[pltpu.VMEM((tm, vd), BF16)],
        compiler_params=pltpu.CompilerParams(dimension_semantics=("arbitrary",), vmem_limit_bytes=vmem),
        name="ret_out",
    )(x, o, sg, gn, wout)


def kernel(x_prompt, x_sample, state_conv, state_ret, norm_mix, norm_ffn, conv_w_in, conv_w, conv_w_out,
           ret_w_in, ret_gn, ret_w_out, ffn_w_gate, ffn_w_up, ffn_w_down, final_norm):
    bp, lp, d = x_prompt.shape
    bs, ls, _ = x_sample.shape
    depth = norm_mix.shape[0]
    dk = ret_w_in.shape[2] // 6 // N_HEADS
    inv = (ROPE_BASE ** (-jnp.arange(dk // 2, dtype=F32) / (dk // 2)))[None, :]
    fin = final_norm[None, :]
    conv_w_in, conv_w_out = conv_w_in.astype(BF16), conv_w_out.astype(BF16)

    xp = x_prompt
    xs = x_sample.reshape(bs * ls, d)
    conv_p, conv_s, ret_p, ret_s = [], [], [], []
    for i in range(depth):
        j = i // N_MIXERS
        g_mix = norm_mix[i][None, :]
        cast = [(ffn_w_gate, i), (ffn_w_up, i), (ffn_w_down, i)]
        if i % N_MIXERS == 0:
            ret_next = i + 1 < depth
            if ret_next:
                cast += [(ret_w_in, (i + 1) // N_MIXERS), (ret_w_out, (i + 1) // N_MIXERS)]
            xp, st, casted = _conv_mixer_seq(xp, g_mix, conv_w_in, conv_w, conv_w_out, layer=j, cast=cast)
            conv_p.append(st)
            if ret_next:
                ret_win, ret_wout = casted[3:]
            xs, st = _conv_mixer_step(xs, state_conv[j], g_mix, conv_w_in, conv_w, conv_w_out, layer=j, seq=ls)
            conv_s.append(st)
        else:
            gn = ret_gn[j][None, :]
            xp, st, casted = _ret_layer_seq(xp, g_mix, ret_win, inv, gn, ret_wout, layer=0, cast=cast)
            ret_p.append(st)
            xs = _ret_out(xs, ret_o, sg, gn, ret_wout, layer=0)
        last = i == depth - 1
        ffn = dict(g=norm_ffn[i][None, :], wg=casted[0], wu=casted[1], wd=casted[2], fin=fin, layer=0,
                   final_norm=last)
        xs = _ffn(xs, **ffn)
        if not last and (i + 1) % N_MIXERS == 1:
            q, k, v, sg = _ret_proj(xs, norm_mix[i + 1][None, :], ret_win, inv, layer=0, seq=ls, pos0=PAST_LEN)
            xp, ret_o, st = _ffn_decode(xp.reshape(bp * lp, d), q=q, k=k, v=v, s0=state_ret,
                                        state_layer=(i + 1) // N_MIXERS, seq=ls, **ffn)
            ret_s.append(st)
        else:
            xp = _ffn(xp.reshape(bp * lp, d), **ffn)
        xp = xp.reshape(bp, lp, d)
    return (xp, xs.reshape(bs, ls, d), jnp.stack(conv_p), jnp.stack(conv_s), jnp.stack(ret_p),
            jnp.stack(ret_s))
```

```python
import functools
import math

import jax
import jax.numpy as jnp
from jax import lax
from jax.experimental import pallas as pl
from jax.experimental.pallas import tpu as pltpu

F32 = jnp.float32
BF16 = jnp.bfloat16

N_MIXERS = 2
CONV_WIDTH = 3
N_HEADS = 4
PAST_LEN = 16384
RMS_EPS = 1e-6
GN_EPS = 1e-6
ROPE_BASE = 10000.0

V7X_VMEM_LIMIT_CAP = 60000 * 1024
V7X_MXU_DIM = 256
SUBLANES = 8
ROW_TILE = 512
FF_CHUNK = V7X_MXU_DIM
SEQ_CHUNK = V7X_MXU_DIM
BF16_ROWS = 16
STEP_ROWS = BF16_ROWS
CAST_CHUNKS = 16
IN_SLOTS, OUT_SLOTS = 4, 2


def _nbytes(shape, dtype):
    return math.prod(shape) * jnp.dtype(dtype).itemsize


def _vmem_limit(buffers, temporaries):
    need = sum(_nbytes(s, d) * n for s, d, n in buffers) + temporaries
    return min(int(need * 1.25) + (4 << 20), V7X_VMEM_LIMIT_CAP)


def _const_spec(shape):
    return pl.BlockSpec(shape, lambda *_: (0,) * len(shape), pipeline_mode=pl.Buffered(1))


def _layer_spec(w, layer):
    return pl.BlockSpec((None,) + w.shape[1:], lambda *_: (layer, 0, 0), pipeline_mode=pl.Buffered(1))


class _CastJobs:
    def __init__(self, weights):
        self.weights = list(weights)
        self.n = len(self.weights)

    @staticmethod
    def chunk(grid_idx, grid):
        step = grid_idx[0]
        for extent, idx in zip(grid[1:], grid_idx[1:]):
            step = step * extent + idx
        return jnp.minimum(step, CAST_CHUNKS - 1)

    def specs(self, grid):
        ins, outs, shapes = [], [], []
        for w, layer in self.weights:
            rows, cols = w.shape[1:]
            assert rows % (CAST_CHUNKS * BF16_ROWS) == 0 and math.prod(grid) >= CAST_CHUNKS
            blk = (None, rows // CAST_CHUNKS, cols)
            ins.append(pl.BlockSpec(blk, lambda *g, layer=layer: (layer, self.chunk(g, grid), 0)))
            outs.append(pl.BlockSpec(blk, lambda *g: (0, self.chunk(g, grid), 0)))
            shapes.append(jax.ShapeDtypeStruct((1, rows, cols), BF16))
        return ins, outs, shapes

    def vmem(self):
        return [((w.shape[1] // CAST_CHUNKS, w.shape[2]), F32, 3) for w, _ in self.weights]

    def run(self, src_refs, dst_refs, step):
        @pl.when(step < CAST_CHUNKS)
        def _():
            for src, dst in zip(src_refs, dst_refs):
                dst[...] = src[...].astype(BF16)


def _rmsnorm(x, g):
    ms = jnp.mean(x * x, axis=-1, keepdims=True)
    return (x * lax.rsqrt(ms + RMS_EPS)) * g


def _dot(a, b):
    return jnp.dot(a, b, preferred_element_type=F32)


def _dot_nt(a, b):
    return lax.dot_general(a, b, (((1,), (1,)), ((), ())), preferred_element_type=F32)


def _dot_tn(a, b):
    return lax.dot_general(a, b, (((0,), (0,)), ((), ())), preferred_element_type=F32)


def _head_log_decay(h):
    return math.log(1.0 - 2.0 ** (-5.0 - h))


def _ffn_rows(x_ref, g_ref, wg_ref, wu_ref, wd_ref, fin_ref, o_ref, act_ref, rows, *, final_norm,
              after_chunk=None):
    x = x_ref[rows, :]
    n = _rmsnorm(x, g_ref[...]).astype(BF16)
    for j in range(wg_ref.shape[1] // FF_CHUNK):
        cols = slice(j * FF_CHUNK, (j + 1) * FF_CHUNK)
        gate = _dot(n, wg_ref[:, cols])
        up = _dot(n, wu_ref[:, cols])
        act_ref[rows, cols] = (jax.nn.silu(gate) * up).astype(BF16)
        if after_chunk is not None:
            after_chunk(j)
    y = x + _dot(act_ref[rows, :], wd_ref[...])
    if final_norm:
        y = _rmsnorm(y, fin_ref[...])
    o_ref[rows, :] = y


def _ffn_kernel(x_ref, g_ref, wg_ref, wu_ref, wd_ref, fin_ref, o_ref, act_ref, *, final_norm):
    for r0 in range(0, x_ref.shape[0], ROW_TILE):
        _ffn_rows(x_ref, g_ref, wg_ref, wu_ref, wd_ref, fin_ref, o_ref, act_ref, slice(r0, r0 + ROW_TILE),
                  final_norm=final_norm)


def _ffn(x, g, wg, wu, wd, fin, *, layer, final_norm):
    m, d = x.shape
    f = wg.shape[2]
    tm = min(2 * ROW_TILE, m)
    assert m % tm == 0 and tm % ROW_TILE == 0 and f % FF_CHUNK == 0
    row = pl.BlockSpec((tm, d), lambda i: (i, 0))
    vmem = _vmem_limit(
        [((tm, d), F32, 4), ((d, f), BF16, 2), ((f, d), BF16, 1), ((tm, f), BF16, 1)],
        temporaries=6 * _nbytes((tm, d), F32))
    return pl.pallas_call(
        functools.partial(_ffn_kernel, final_norm=final_norm),
        grid=(m // tm,),
        in_specs=[row, _const_spec((1, d)), _layer_spec(wg, layer), _layer_spec(wu, layer),
                  _layer_spec(wd, layer), _const_spec((1, d))],
        out_specs=row,
        out_shape=jax.ShapeDtypeStruct((m, d), F32),
        scratch_shapes=[pltpu.VMEM((tm, f), BF16)],
        compiler_params=pltpu.CompilerParams(dimension_semantics=("arbitrary",), vmem_limit_bytes=vmem),
        name="ffn_final" if final_norm else "ffn",
    )(x, g, wg, wu, wd, fin)


def _conv_front(x, g_ref, win_ref):
    d = x.shape[1]
    hn = _rmsnorm(x, g_ref[...]).astype(BF16)
    b = _dot(hn, win_ref[:, 0:d])
    c = _dot(hn, win_ref[:, d:2 * d])
    h = _dot(hn, win_ref[:, 2 * d:3 * d])
    return b, c * h


def _conv_seq_kernel(x_ref, g_ref, win_ref, wc_ref, wout_ref, *refs, casts):
    cast_src, (o_ref, st_ref), cast_dst, ubuf_ref = (
        refs[:casts.n], refs[casts.n:casts.n + 2], refs[casts.n + 2:-1], refs[-1])
    tm, d = x_ref.shape[1:]

    @pl.when(pl.program_id(1) == 0)
    def _():
        ubuf_ref[0:SUBLANES, :] = jnp.zeros((SUBLANES, d), F32)

    casts.run(cast_src, cast_dst, pl.program_id(0) * pl.num_programs(1) + pl.program_id(1))
    for r0 in range(0, tm, ROW_TILE):
        x = x_ref[0, r0:r0 + ROW_TILE, :]
        b, u = _conv_front(x, g_ref, win_ref)
        ubuf_ref[SUBLANES + r0:SUBLANES + r0 + ROW_TILE, :] = u
        y = (wc_ref[0:1, :] * ubuf_ref[SUBLANES - 2 + r0:SUBLANES - 2 + r0 + ROW_TILE, :]
             + wc_ref[1:2, :] * ubuf_ref[SUBLANES - 1 + r0:SUBLANES - 1 + r0 + ROW_TILE, :]
             + wc_ref[2:3, :] * u)
        o_ref[0, r0:r0 + ROW_TILE, :] = x + _dot((b * y).astype(BF16), wout_ref[...])
    st_ref[0] = ubuf_ref[SUBLANES + tm - (CONV_WIDTH - 1):SUBLANES + tm, :]
    ubuf_ref[0:SUBLANES, :] = ubuf_ref[tm:tm + SUBLANES, :]


def _conv_mixer_seq(x, g, win, wc, wout, *, layer, cast=()):
    bsz, seq, d = x.shape
    tm = 2 * ROW_TILE
    assert seq % tm == 0 and wc.shape[1] == CONV_WIDTH == 3
    grid = (bsz, seq // tm)
    casts = _CastJobs(cast)
    cast_in, cast_out, cast_shapes = casts.specs(grid)
    row = pl.BlockSpec((1, tm, d), lambda b, t: (b, t, 0))
    vmem = _vmem_limit(
        [((tm, d), F32, 4), ((d, 3 * d), BF16, 1), ((d, d), BF16, 1), ((tm + SUBLANES, d), F32, 1)]
        + casts.vmem(), temporaries=5 * _nbytes((tm, d), F32))
    y, st, *casted = pl.pallas_call(
        functools.partial(_conv_seq_kernel, casts=casts),
        grid=grid,
        in_specs=[row, _const_spec((1, d)), _layer_spec(win, layer), _layer_spec(wc, layer),
                  _layer_spec(wout, layer)] + cast_in,
        out_specs=[row, pl.BlockSpec((1, CONV_WIDTH - 1, d), lambda b, t: (b, 0, 0))] + cast_out,
        out_shape=[jax.ShapeDtypeStruct((bsz, seq, d), F32),
                   jax.ShapeDtypeStruct((bsz, CONV_WIDTH - 1, d), F32)] + cast_shapes,
        scratch_shapes=[pltpu.VMEM((tm + SUBLANES, d), F32)],
        compiler_params=pltpu.CompilerParams(dimension_semantics=("arbitrary", "arbitrary"),
                                             vmem_limit_bytes=vmem),
        name="conv_mixer_seq",
    )(x, g, win, wc, wout, *(w for w, _ in cast))
    return y, st, casted


def _conv_step_kernel(x_ref, pre_ref, g_ref, win_ref, wc_ref, wout_ref, o_ref, u_ref, *, seq):
    x = x_ref[...]
    m = x.shape[0]
    b, u = _conv_front(x, g_ref, win_ref)
    pre = pre_ref[...]
    t = lax.rem(lax.broadcasted_iota(jnp.int32, (m, 1), 0), seq)
    back1 = jnp.where(t >= 1, pltpu.roll(u, 1, axis=0), pltpu.roll(pre, m - 1, axis=0))
    back2 = jnp.where(t >= 2, pltpu.roll(u, 2, axis=0), pre)
    y = wc_ref[0:1, :] * back2 + wc_ref[1:2, :] * back1 + wc_ref[2:3, :] * u
    o_ref[...] = x + _dot((b * y).astype(BF16), wout_ref[...])
    u_ref[...] = u


def _conv_mixer_step(x, buf, g, win, wc, wout, *, layer, seq):
    m, d = x.shape
    taps = CONV_WIDTH - 1
    assert wc.shape[1] == CONV_WIDTH == 3 and seq >= taps and m % seq == 0
    pre = jnp.pad(buf, ((0, 0), (0, seq - taps), (0, 0))).reshape(m, d)
    full = pl.BlockSpec((m, d), lambda i: (0, 0))
    vmem = _vmem_limit(
        [((m, d), F32, 8), ((d, 3 * d), BF16, 1), ((d, d), BF16, 1)],
        temporaries=12 * _nbytes((m, d), F32))
    y, u = pl.pallas_call(
        functools.partial(_conv_step_kernel, seq=seq),
        grid=(1,),
        in_specs=[full, full, _const_spec((1, d)), _layer_spec(win, layer), _layer_spec(wc, layer),
                  _layer_spec(wout, layer)],
        out_specs=[full, full],
        out_shape=[jax.ShapeDtypeStruct((m, d), F32), jax.ShapeDtypeStruct((m, d), F32)],
        compiler_params=pltpu.CompilerParams(dimension_semantics=("arbitrary",), vmem_limit_bytes=vmem),
        name="conv_mixer_step",
    )(x, pre, g, win, wc, wout)
    return y, u.reshape(m // seq, seq, d)[:, seq - taps:, :]


def _rope_angles(inv_ref, pos):
    ang = pos.astype(F32) * inv_ref[...]
    return jnp.cos(ang), jnp.sin(ang)


def _rotate(z, cos, sin):
    half = z.shape[1] // 2
    x1, x2 = z[:, :half], z[:, half:]
    return jnp.concatenate([x1 * cos - x2 * sin, x1 * sin + x2 * cos], axis=1)


def _proj_head(hn, win_ref, h, cos, sin):
    ret_in = win_ref.shape[1]
    qk, vd = ret_in // 6, ret_in // 3
    dk, dv = qk // N_HEADS, vd // N_HEADS
    q = _rotate(_dot(hn, win_ref[:, h * dk:(h + 1) * dk]), cos, sin)
    k = _rotate(_dot(hn, win_ref[:, qk + h * dk:qk + (h + 1) * dk]), cos, sin) * dk ** -0.5
    v = _dot(hn, win_ref[:, 2 * qk + h * dv:2 * qk + (h + 1) * dv])
    sg = jax.nn.silu(_dot(hn, win_ref[:, 2 * qk + vd + h * dv:2 * qk + vd + (h + 1) * dv]))
    return q, k, v, sg


def _gn_gate(o, sg, gn):
    mu = jnp.mean(o, axis=-1, keepdims=True)
    var = jnp.mean(jnp.square(o - mu), axis=-1, keepdims=True)
    of = ((o - mu) * lax.rsqrt(var + GN_EPS)) * gn
    return (sg * of).astype(BF16)


def _rope_table_kernel(inv_ref, cos_ref, sin_ref):
    pos = lax.broadcasted_iota(jnp.int32, cos_ref.shape, 0)
    cos_ref[...], sin_ref[...] = _rope_angles(inv_ref, pos)


def _rope_tables(inv, seq):
    out = jax.ShapeDtypeStruct((seq, inv.shape[1]), F32)
    return pl.pallas_call(_rope_table_kernel, out_shape=[out, out], name="rope_tables")(inv)


def _ret_layer_seq_kernel(x_ref, g_ref, win_ref, cos_ref, sin_ref, gn_ref, wout_ref, *refs, casts):
    cast_src, (y_ref, s_ref), cast_dst, act_ref = (
        refs[:casts.n], refs[casts.n:casts.n + 2], refs[casts.n + 2:-1], refs[-1])
    tm = x_ref.shape[1]
    dv = gn_ref.shape[1] // N_HEADS
    c_len = SEQ_CHUNK

    @pl.when(pl.program_id(1) == 0)
    def _():
        s_ref[...] = jnp.zeros(s_ref.shape, F32)

    casts.run(cast_src, cast_dst, pl.program_id(0) * pl.num_programs(1) + pl.program_id(1))
    i = lax.broadcasted_iota(jnp.int32, (c_len, c_len), 0)
    j = lax.broadcasted_iota(jnp.int32, (c_len, c_len), 1)
    diff = (i - j).astype(F32)
    idx = lax.broadcasted_iota(jnp.int32, (c_len, 1), 0).astype(F32)
    for c in range(tm // c_len):
        rows = slice(c * c_len, (c + 1) * c_len)
        x = x_ref[0, rows, :]
        hn = _rmsnorm(x, g_ref[...]).astype(BF16)
        cos, sin = cos_ref[rows, :], sin_ref[rows, :]
        for h in range(N_HEADS):
            lg = _head_log_decay(h)
            decay = jnp.where(diff >= 0, jnp.exp(jnp.maximum(diff, 0.0) * lg), 0.0)
            cross_w = jnp.exp((idx + 1.0) * lg)
            state_w = jnp.exp((c_len - 1.0 - idx) * lg)
            chunk_decay = math.exp(c_len * lg)
            q, k, v, sg = _proj_head(hn, win_ref, h, cos, sin)
            qc, vc = q.astype(BF16), v.astype(BF16)
            cols = slice(h * dv, (h + 1) * dv)
            s = s_ref[0, h]
            scores = _dot_nt(qc, k.astype(BF16)) * decay
            o = _dot(scores.astype(BF16), vc) + _dot(qc, s.astype(BF16)) * cross_w
            s_ref[0, h] = s * chunk_decay + _dot_tn((k * state_w).astype(BF16), vc)
            act_ref[rows, cols] = _gn_gate(o, sg, gn_ref[:, cols])
        y_ref[0, rows, :] = x + _dot(act_ref[rows, :], wout_ref[...])


def _ret_layer_seq(x, g, win, inv, gn, wout, *, layer, cast=()):
    bsz, seq, d = x.shape
    ret_in, vd = win.shape[2], wout.shape[1]
    dk, dv = ret_in // 6 // N_HEADS, vd // N_HEADS
    tm = ROW_TILE
    assert seq % tm == 0 and tm % SEQ_CHUNK == 0
    grid = (bsz, seq // tm)
    casts = _CastJobs(cast)
    cast_in, cast_out, cast_shapes = casts.specs(grid)
    cos, sin = _rope_tables(inv, seq)
    row = pl.BlockSpec((1, tm, d), lambda b, t: (b, t, 0))
    rope = pl.BlockSpec((tm, dk // 2), lambda b, t: (t, 0))
    vmem = _vmem_limit(
        [((tm, d), F32, 4), ((d, ret_in), BF16, 1), ((vd, d), BF16, 1), ((tm, dk), F32, 4),
         ((N_HEADS, dk, dv), F32, 2), ((tm, vd), BF16, 1)] + casts.vmem(),
        temporaries=8 * _nbytes((tm, d), F32))
    y, st, *casted = pl.pallas_call(
        functools.partial(_ret_layer_seq_kernel, casts=casts),
        grid=grid,
        in_specs=[row, _const_spec((1, d)), _layer_spec(win, layer), rope, rope, _const_spec((1, vd)),
                  _layer_spec(wout, layer)] + cast_in,
        out_specs=[row, pl.BlockSpec((1, N_HEADS, dk, dv), lambda b, t: (b, 0, 0, 0))] + cast_out,
        out_shape=[jax.ShapeDtypeStruct((bsz, seq, d), F32),
                   jax.ShapeDtypeStruct((bsz, N_HEADS, dk, dv), F32)] + cast_shapes,
        scratch_shapes=[pltpu.VMEM((tm, vd), BF16)],
        compiler_params=pltpu.CompilerParams(dimension_semantics=("arbitrary", "arbitrary"),
                                             vmem_limit_bytes=vmem),
        name="ret_layer_seq",
    )(x, g, win, cos, sin, gn, wout, *(w for w, _ in cast))
    return y, st, casted


def _ret_proj_kernel(x_ref, g_ref, win_ref, inv_ref, q_ref, k_ref, v_ref, sg_ref, *, seq, pos0):
    tm = x_ref.shape[0]
    dk, dv = q_ref.shape[1] // N_HEADS, v_ref.shape[1] // N_HEADS
    hn = _rmsnorm(x_ref[...], g_ref[...]).astype(BF16)
    row = lax.broadcasted_iota(jnp.int32, (tm, dk // 2), 0)
    cos, sin = _rope_angles(inv_ref, pos0 + lax.rem(row, seq))
    for h in range(N_HEADS):
        q, k, v, sg = _proj_head(hn, win_ref, h, cos, sin)
        q_ref[:, h * dk:(h + 1) * dk] = q
        k_ref[:, h * dk:(h + 1) * dk] = k
        v_ref[:, h * dv:(h + 1) * dv] = v
        sg_ref[:, h * dv:(h + 1) * dv] = sg


def _ret_proj(x, g, win, inv, *, layer, seq, pos0):
    m, d = x.shape
    ret_in = win.shape[2]
    qk, vd = ret_in // 6, ret_in // 3
    tm = ROW_TILE
    assert m % tm == 0 and tm % seq == 0
    row = lambda n: pl.BlockSpec((tm, n), lambda i: (i, 0))
    vmem = _vmem_limit(
        [((tm, d), F32, 2), ((d, ret_in), BF16, 1), ((tm, qk), F32, 4), ((tm, vd), F32, 4)],
        temporaries=8 * _nbytes((tm, d), F32))
    return pl.pallas_call(
        functools.partial(_ret_proj_kernel, seq=seq, pos0=pos0),
        grid=(m // tm,),
        in_specs=[row(d), _const_spec((1, d)), _layer_spec(win, layer), _const_spec(inv.shape)],
        out_specs=[row(qk), row(qk), row(vd), row(vd)],
        out_shape=[jax.ShapeDtypeStruct((m, qk), F32), jax.ShapeDtypeStruct((m, qk), F32),
                   jax.ShapeDtypeStruct((m, vd), F32), jax.ShapeDtypeStruct((m, vd), F32)],
        compiler_params=pltpu.CompilerParams(dimension_semantics=("arbitrary",), vmem_limit_bytes=vmem),
        name="ret_proj",
    )(x, g, win, inv)


class _DecodeGroup:
    def __init__(self, q_ref, k_ref, v_ref, rows, seq):
        n = rows.stop - rows.start
        dk, dv = q_ref.shape[1] // N_HEADS, v_ref.shape[1] // N_HEADS
        self.seq = seq
        self.r = lax.broadcasted_iota(jnp.int32, (n, 1), 0)
        t = lax.rem(self.r, seq)
        tf = t.astype(F32)
        self.qb, self.kw, self.vb, self.inner, self.cross, self.cross_w = [], [], [], [], [], []
        for h in range(N_HEADS):
            lg = _head_log_decay(h)
            q = q_ref[rows, h * dk:(h + 1) * dk]
            k = k_ref[rows, h * dk:(h + 1) * dk]
            v = v_ref[rows, h * dv:(h + 1) * dv]
            acc = jnp.zeros((n, dv), F32)
            for dlt in range(seq):
                kd = k if dlt == 0 else pltpu.roll(k, dlt, axis=0)
                vd = v if dlt == 0 else pltpu.roll(v, dlt, axis=0)
                sc = jnp.sum(q * kd, axis=-1, keepdims=True) * math.exp(dlt * lg)
                acc = acc + jnp.where(t >= dlt, sc, 0.0) * vd
            self.inner.append(acc)
            self.cross.append(jnp.zeros((n, dv), F32))
            self.cross_w.append(jnp.exp((tf + 1.0) * lg))
            self.qb.append(q.astype(BF16))
            self.kw.append(k * jnp.exp((seq - 1.0 - tf) * lg))
            self.vb.append(v.astype(BF16))

    def advance(self, b, s0_ref, s_ref):
        own = (self.r >= b * self.seq) & (self.r < (b + 1) * self.seq)
        for h in range(N_HEADS):
            s0 = s0_ref[h]
            self.cross[h] = self.cross[h] + jnp.where(own, _dot(self.qb[h], s0.astype(BF16)), 0.0)
            kwb = jnp.where(own, self.kw[h], 0.0).astype(BF16)
            s_ref[h] = s0 * math.exp(self.seq * _head_log_decay(h)) + _dot_tn(kwb, self.vb[h])

    def output(self, h):
        return self.inner[h] + self.cross[h] * self.cross_w[h]


def _ffn_decode_kernel(x_ref, g_ref, wg_ref, wu_ref, wd_ref, fin_ref, q_ref, k_ref, v_ref, s0_hbm,
                       o_ref, ro_ref, s_hbm, act_ref, sin_ref, sout_ref, sem_in, sem_out,
                       *, final_norm, seq, layer):
    i = pl.program_id(0)
    last = pl.num_programs(0) - 1
    n_sub = x_ref.shape[0] // ROW_TILE
    per_group = STEP_ROWS // seq
    per_step = n_sub * per_group
    n_chunks = wg_ref.shape[1] // FF_CHUNK
    dv = v_ref.shape[1] // N_HEADS
    run_after = {(u + 1) * n_chunks // per_group - 1: u for u in range(per_group)}

    def in_copy(n, slot):
        return pltpu.make_async_copy(s0_hbm.at[layer, n], sin_ref.at[slot], sem_in.at[slot])

    def out_copy(n, slot):
        return pltpu.make_async_copy(sout_ref.at[slot], s_hbm.at[n], sem_out.at[slot])

    lead = IN_SLOTS - 1

    @pl.when(i == 0)
    def _():
        for n in range(lead):
            in_copy(n, n % IN_SLOTS).start()

    def advance(group, sub, b):
        l = sub * per_group + b
        n = i * per_step + l
        slot, oslot = l % IN_SLOTS, l % OUT_SLOTS
        in_copy(n, slot).wait()
        if l + lead < per_step:
            in_copy(n + lead, (l + lead) % IN_SLOTS).start()
        else:
            pl.when(i < last)(lambda: in_copy(n + lead, (l + lead) % IN_SLOTS).start())
        if l >= OUT_SLOTS:
            out_copy(n - OUT_SLOTS, oslot).wait()
        else:
            pl.when(i > 0)(lambda: out_copy(n - OUT_SLOTS, oslot).wait())
        group.advance(b, sin_ref.at[slot], sout_ref.at[oslot])
        out_copy(n, oslot).start()

    for sub in range(n_sub):
        rows16 = slice(sub * STEP_ROWS, (sub + 1) * STEP_ROWS)
        group = _DecodeGroup(q_ref, k_ref, v_ref, rows16, seq)

        def after_chunk(j, group=group, sub=sub):
            if j in run_after:
                advance(group, sub, run_after[j])

        _ffn_rows(x_ref, g_ref, wg_ref, wu_ref, wd_ref, fin_ref, o_ref, act_ref,
                  slice(sub * ROW_TILE, (sub + 1) * ROW_TILE), final_norm=final_norm, after_chunk=after_chunk)
        for h in range(N_HEADS):
            ro_ref[rows16, h * dv:(h + 1) * dv] = group.output(h)

    @pl.when(i == last)
    def _():
        for l in range(per_step - OUT_SLOTS, per_step):
            out_copy(i * per_step + l, l % OUT_SLOTS).wait()


def _ffn_decode(x, g, wg, wu, wd, fin, q, k, v, s0, *, layer, final_norm, state_layer, seq):
    m, d = x.shape
    f = wg.shape[2]
    ms, qk = q.shape
    vd = v.shape[1]
    _, bsz, _, dk, dv = s0.shape
    tm = 2 * ROW_TILE
    steps = m // tm
    rows_s = (tm // ROW_TILE) * STEP_ROWS
    assert m % tm == 0 and f % FF_CHUNK == 0 and f // FF_CHUNK >= STEP_ROWS // seq
    assert seq <= SEQ_CHUNK and STEP_ROWS % seq == 0 and ms == bsz * seq and ms == steps * rows_s
    assert (rows_s // seq) % IN_SLOTS == 0 and (rows_s // seq) % OUT_SLOTS == 0
    row = pl.BlockSpec((tm, d), lambda i: (i, 0))
    srow = lambda n: pl.BlockSpec((rows_s, n), lambda i: (i, 0))
    hbm = pl.BlockSpec(memory_space=pl.ANY)
    state = (N_HEADS, dk, dv)
    vmem = _vmem_limit(
        [((tm, d), F32, 4), ((d, f), BF16, 2), ((f, d), BF16, 1), ((tm, f), BF16, 1),
         (state, F32, IN_SLOTS + OUT_SLOTS),
         ((rows_s, vd), F32, 8)],
        temporaries=8 * _nbytes((tm, d), F32))
    return pl.pallas_call(
        functools.partial(_ffn_decode_kernel, final_norm=final_norm, seq=seq, layer=state_layer),
        grid=(steps,),
        in_specs=[row, _const_spec((1, d)), _layer_spec(wg, layer), _layer_spec(wu, layer),
                  _layer_spec(wd, layer), _const_spec((1, d)), srow(qk), srow(qk), srow(vd), hbm],
        out_specs=[row, srow(vd), hbm],
        out_shape=[jax.ShapeDtypeStruct((m, d), F32), jax.ShapeDtypeStruct((ms, vd), F32),
                   jax.ShapeDtypeStruct(s0.shape[1:], F32)],
        scratch_shapes=[pltpu.VMEM((tm, f), BF16), pltpu.VMEM((IN_SLOTS,) + state, F32),
                        pltpu.VMEM((OUT_SLOTS,) + state, F32), pltpu.SemaphoreType.DMA((IN_SLOTS,)),
                        pltpu.SemaphoreType.DMA((OUT_SLOTS,))],
        compiler_params=pltpu.CompilerParams(dimension_semantics=("arbitrary",), vmem_limit_bytes=vmem),
        name="ffn_decode_ret",
    )(x, g, wg, wu, wd, fin, q, k, v, s0)


def _ret_out_kernel(x_ref, o_ref, sg_ref, gn_ref, wout_ref, y_ref, act_ref):
    dv = o_ref.shape[1] // N_HEADS
    for h in range(N_HEADS):
        cols = slice(h * dv, (h + 1) * dv)
        act_ref[:, cols] = _gn_gate(o_ref[:, cols], sg_ref[:, cols], gn_ref[:, cols])
    y_ref[...] = x_ref[...] + _dot(act_ref[...], wout_ref[...])


def _ret_out(x, o, sg, gn, wout, *, layer):
    m, d = x.shape
    vd = o.shape[1]
    tm = ROW_TILE
    assert m % tm == 0
    row = lambda n: pl.BlockSpec((tm, n), lambda i: (i, 0))
    vmem = _vmem_limit(
        [((tm, d), F32, 4), ((tm, vd), F32, 4), ((vd, d), BF16, 1), ((tm, vd), BF16, 1)],
        temporaries=6 * _nbytes((tm, d), F32))
    return pl.pallas_call(
        _ret_out_kernel,
        grid=(m // tm,),
        in_specs=[row(d), row(vd), row(vd), _const_spec((1, vd)), _layer_spec(wout, layer)],
        out_specs=row(d),
        out_shape=jax.ShapeDtypeStruct((m, d), F32),
        scratch_shapes=[pltpu.VMEM((tm, vd), BF16)],
        compiler_params=pltpu.CompilerParams(dimension_semantics=("arbitrary",), vmem_limit_bytes=vmem),
        name="ret_out",
    )(x, o, sg, gn, wout)


def kernel(x_prompt, x_sample, state_conv, state_ret, norm_mix, norm_ffn, conv_w_in, conv_w, conv_w_out,
           ret_w_in, ret_gn, ret_w_out, ffn_w_gate, ffn_w_up, ffn_w_down, final_norm):
    bp, lp, d = x_prompt.shape
    bs, ls, _ = x_sample.shape
    depth = norm_mix.shape[0]
    dk = ret_w_in.shape[2] // 6 // N_HEADS
    inv = (ROPE_BASE ** (-jnp.arange(dk // 2, dtype=F32) / (dk // 2)))[None, :]
    fin = final_norm[None, :]
    conv_w_in, conv_w_out = conv_w_in.astype(BF16), conv_w_out.astype(BF16)

    xp = x_prompt
    xs = x_sample.reshape(bs * ls, d)
    conv_p, conv_s, ret_p, ret_s = [], [], [], []
    for i in range(depth):
        j = i // N_MIXERS
        g_mix = norm_mix[i][None, :]
        cast = [(ffn_w_gate, i), (ffn_w_up, i), (ffn_w_down, i)]
        if i % N_MIXERS == 0:
            ret_next = i + 1 < depth
            if ret_next:
                cast += [(ret_w_in, (i + 1) // N_MIXERS), (ret_w_out, (i + 1) // N_MIXERS)]
            xp, st, casted = _conv_mixer_seq(xp, g_mix, conv_w_in, conv_w, conv_w_out, layer=j, cast=cast)
            conv_p.append(st)
            if ret_next:
                ret_win, ret_wout = casted[3:]
            xs, st = _conv_mixer_step(xs, state_conv[j], g_mix, conv_w_in, conv_w, conv_w_out, layer=j, seq=ls)
            conv_s.append(st)
        else:
            gn = ret_gn[j][None, :]
            xp, st, casted = _ret_layer_seq(xp, g_mix, ret_win, inv, gn, ret_wout, layer=0, cast=cast)
            ret_p.append(st)
            xs = _ret_out(xs, ret_o, sg, gn, ret_wout, layer=0)
        last = i == depth - 1
        ffn = dict(g=norm_ffn[i][None, :], wg=casted[0], wu=casted[1], wd=casted[2], fin=fin, layer=0,
                   final_norm=last)
        xs = _ffn(xs, **ffn)
        if not last and (i + 1) % N_MIXERS == 1:
            q, k, v, sg = _ret_proj(xs, norm_mix[i + 1][None, :], ret_win, inv, layer=0, seq=ls, pos0=PAST_LEN)
            xp, ret_o, st = _ffn_decode(xp.reshape(bp * lp, d), q=q, k=k, v=v, s0=state_ret,
                                        state_layer=(i + 1) // N_MIXERS, seq=ls, **ffn)
            ret_s.append(st)
        else:
            xp = _ffn(xp.reshape(bp * lp, d), **ffn)
        xp = xp.reshape(bp, lp, d)
    return (xp, xs.reshape(bs, ls, d), jnp.stack(conv_p), jnp.stack(conv_s), jnp.stack(ret_p),
            jnp.stack(ret_s))
```

```python
import functools
import math

import jax
import jax.numpy as jnp
from jax import lax
from jax.experimental import pallas as pl
from jax.experimental.pallas import tpu as pltpu

F32 = jnp.float32
BF16 = jnp.bfloat16

N_MIXERS = 2
CONV_WIDTH = 3
N_HEADS = 4
PAST_LEN = 16384
RMS_EPS = 1e-6
GN_EPS = 1e-6
ROPE_BASE = 10000.0

V7X_VMEM_LIMIT_CAP = 60000 * 1024
V7X_MXU_DIM = 256
SUBLANES = 8
ROW_TILE = 512
FF_CHUNK = V7X_MXU_DIM
SEQ_CHUNK = V7X_MXU_DIM
BF16_ROWS = 16
STEP_ROWS = BF16_ROWS
CAST_CHUNKS = 16
IN_SLOTS, OUT_SLOTS = 4, 2


def _nbytes(shape, dtype):
    return math.prod(shape) * jnp.dtype(dtype).itemsize


def _vmem_limit(buffers, temporaries):
    need = sum(_nbytes(s, d) * n for s, d, n in buffers) + temporaries
    return min(int(need * 1.25) + (4 << 20), V7X_VMEM_LIMIT_CAP)


def _const_spec(shape):
    return pl.BlockSpec(shape, lambda *_: (0,) * len(shape), pipeline_mode=pl.Buffered(1))


def _layer_spec(w, layer):
    return pl.BlockSpec((None,) + w.shape[1:], lambda *_: (layer, 0, 0), pipeline_mode=pl.Buffered(1))


class _SideJobs:
    def __init__(self, weights=(), rope=None):
        self.weights = list(weights)
        self.rope = rope
        self.n_in = len(self.weights) + (rope is not None)
        self.n_out = len(self.weights) + 2 * (rope is not None)

    @staticmethod
    def chunk(grid_idx, grid):
        step = grid_idx[0]
        for extent, idx in zip(grid[1:], grid_idx[1:]):
            step = step * extent + idx
        return jnp.minimum(step, CAST_CHUNKS - 1)

    def specs(self, grid):
        assert math.prod(grid) >= CAST_CHUNKS
        ins, outs, shapes = [], [], []
        for w, layer in self.weights:
            rows, cols = w.shape[1:]
            assert rows % (CAST_CHUNKS * BF16_ROWS) == 0
            blk = (None, rows // CAST_CHUNKS, cols)
            ins.append(pl.BlockSpec(blk, lambda *g, layer=layer: (layer, self.chunk(g, grid), 0)))
            outs.append(pl.BlockSpec(blk, lambda *g: (0, self.chunk(g, grid), 0)))
            shapes.append(jax.ShapeDtypeStruct((1, rows, cols), BF16))
        if self.rope is not None:
            inv, seq = self.rope
            assert seq % (CAST_CHUNKS * SUBLANES) == 0
            ins.append(_const_spec(inv.shape))
            outs += [pl.BlockSpec((seq // CAST_CHUNKS, inv.shape[1]), lambda *g: (self.chunk(g, grid), 0))] * 2
            shapes += [jax.ShapeDtypeStruct((seq, inv.shape[1]), F32)] * 2
        return ins, outs, shapes

    def operands(self):
        return [w for w, _ in self.weights] + ([self.rope[0]] if self.rope is not None else [])

    def vmem(self):
        return [((w.shape[1] // CAST_CHUNKS, w.shape[2]), F32, 3) for w, _ in self.weights]

    def run(self, src_refs, dst_refs, step):
        n_w = len(self.weights)

        @pl.when(step < CAST_CHUNKS)
        def _():
            for src, dst in zip(src_refs[:n_w], dst_refs[:n_w]):
                dst[...] = src[...].astype(BF16)
            if self.rope is not None:
                cos_ref, sin_ref = dst_refs[n_w:]
                pos = step * cos_ref.shape[0] + lax.broadcasted_iota(jnp.int32, cos_ref.shape, 0)
                cos_ref[...], sin_ref[...] = _rope_angles(src_refs[n_w], pos)


def _rmsnorm(x, g):
    ms = jnp.mean(x * x, axis=-1, keepdims=True)
    return (x * lax.rsqrt(ms + RMS_EPS)) * g


def _dot(a, b):
    return jnp.dot(a, b, preferred_element_type=F32)


def _dot_nt(a, b):
    return lax.dot_general(a, b, (((1,), (1,)), ((), ())), preferred_element_type=F32)


def _dot_tn(a, b):
    return lax.dot_general(a, b, (((0,), (0,)), ((), ())), preferred_element_type=F32)


def _head_log_decay(h):
    return math.log(1.0 - 2.0 ** (-5.0 - h))


def _ffn_rows(x_ref, g_ref, wg_ref, wu_ref, wd_ref, fin_ref, o_ref, act_ref, rows, *, final_norm,
              after_chunk=None):
    x = x_ref[rows, :]
    n = _rmsnorm(x, g_ref[...]).astype(BF16)
    for j in range(wg_ref.shape[1] // FF_CHUNK):
        cols = slice(j * FF_CHUNK, (j + 1) * FF_CHUNK)
        gate = _dot(n, wg_ref[:, cols])
        up = _dot(n, wu_ref[:, cols])
        act_ref[rows, cols] = (jax.nn.silu(gate) * up).astype(BF16)
        if after_chunk is not None:
            after_chunk(j)
    y = x + _dot(act_ref[rows, :], wd_ref[...])
    if final_norm:
        y = _rmsnorm(y, fin_ref[...])
    o_ref[rows, :] = y


def _ffn_kernel(x_ref, g_ref, wg_ref, wu_ref, wd_ref, fin_ref, o_ref, act_ref, *, final_norm):
    for r0 in range(0, x_ref.shape[0], ROW_TILE):
        _ffn_rows(x_ref, g_ref, wg_ref, wu_ref, wd_ref, fin_ref, o_ref, act_ref, slice(r0, r0 + ROW_TILE),
                  final_norm=final_norm)


def _ffn(x, g, wg, wu, wd, fin, *, layer, final_norm):
    m, d = x.shape
    f = wg.shape[2]
    tm = min(2 * ROW_TILE, m)
    assert m % tm == 0 and tm % ROW_TILE == 0 and f % FF_CHUNK == 0
    row = pl.BlockSpec((tm, d), lambda i: (i, 0))
    vmem = _vmem_limit(
        [((tm, d), F32, 4), ((d, f), BF16, 2), ((f, d), BF16, 1), ((tm, f), BF16, 1)],
        temporaries=6 * _nbytes((tm, d), F32))
    return pl.pallas_call(
        functools.partial(_ffn_kernel, final_norm=final_norm),
        grid=(m // tm,),
        in_specs=[row, _const_spec((1, d)), _layer_spec(wg, layer), _layer_spec(wu, layer),
                  _layer_spec(wd, layer), _const_spec((1, d))],
        out_specs=row,
        out_shape=jax.ShapeDtypeStruct((m, d), F32),
        scratch_shapes=[pltpu.VMEM((tm, f), BF16)],
        compiler_params=pltpu.CompilerParams(dimension_semantics=("arbitrary",), vmem_limit_bytes=vmem),
        name="ffn_final" if final_norm else "ffn",
    )(x, g, wg, wu, wd, fin)


def _conv_front(x, g_ref, win_ref):
    d = x.shape[1]
    hn = _rmsnorm(x, g_ref[...]).astype(BF16)
    b = _dot(hn, win_ref[:, 0:d])
    c = _dot(hn, win_ref[:, d:2 * d])
    h = _dot(hn, win_ref[:, 2 * d:3 * d])
    return b, c * h


def _conv_seq_kernel(x_ref, g_ref, win_ref, wc_ref, wout_ref, *refs, jobs):
    job_src, (o_ref, st_ref), job_dst, ubuf_ref = (
        refs[:jobs.n_in], refs[jobs.n_in:jobs.n_in + 2], refs[jobs.n_in + 2:-1], refs[-1])
    tm, d = x_ref.shape[1:]

    @pl.when(pl.program_id(1) == 0)
    def _():
        ubuf_ref[0:SUBLANES, :] = jnp.zeros((SUBLANES, d), F32)

    jobs.run(job_src, job_dst, pl.program_id(0) * pl.num_programs(1) + pl.program_id(1))
    for r0 in range(0, tm, ROW_TILE):
        x = x_ref[0, r0:r0 + ROW_TILE, :]
        b, u = _conv_front(x, g_ref, win_ref)
        ubuf_ref[SUBLANES + r0:SUBLANES + r0 + ROW_TILE, :] = u
        y = (wc_ref[0:1, :] * ubuf_ref[SUBLANES - 2 + r0:SUBLANES - 2 + r0 + ROW_TILE, :]
             + wc_ref[1:2, :] * ubuf_ref[SUBLANES - 1 + r0:SUBLANES - 1 + r0 + ROW_TILE, :]
             + wc_ref[2:3, :] * u)
        o_ref[0, r0:r0 + ROW_TILE, :] = x + _dot((b * y).astype(BF16), wout_ref[...])
    st_ref[0] = ubuf_ref[SUBLANES + tm - (CONV_WIDTH - 1):SUBLANES + tm, :]
    ubuf_ref[0:SUBLANES, :] = ubuf_ref[tm:tm + SUBLANES, :]


def _conv_mixer_seq(x, g, win, wc, wout, *, layer, jobs):
    bsz, seq, d = x.shape
    tm = 2 * ROW_TILE
    assert seq % tm == 0 and wc.shape[1] == CONV_WIDTH == 3
    grid = (bsz, seq // tm)
    job_in, job_out, job_shapes = jobs.specs(grid)
    row = pl.BlockSpec((1, tm, d), lambda b, t: (b, t, 0))
    vmem = _vmem_limit(
        [((tm, d), F32, 4), ((d, 3 * d), BF16, 1), ((d, d), BF16, 1), ((tm + SUBLANES, d), F32, 1)]
        + jobs.vmem(), temporaries=5 * _nbytes((tm, d), F32))
    y, st, *job_res = pl.pallas_call(
        functools.partial(_conv_seq_kernel, jobs=jobs),
        grid=grid,
        in_specs=[row, _const_spec((1, d)), _layer_spec(win, layer), _layer_spec(wc, layer),
                  _layer_spec(wout, layer)] + job_in,
        out_specs=[row, pl.BlockSpec((1, CONV_WIDTH - 1, d), lambda b, t: (b, 0, 0))] + job_out,
        out_shape=[jax.ShapeDtypeStruct((bsz, seq, d), F32),
                   jax.ShapeDtypeStruct((bsz, CONV_WIDTH - 1, d), F32)] + job_shapes,
        scratch_shapes=[pltpu.VMEM((tm + SUBLANES, d), F32)],
        compiler_params=pltpu.CompilerParams(dimension_semantics=("arbitrary", "arbitrary"),
                                             vmem_limit_bytes=vmem),
        name="conv_mixer_seq",
    )(x, g, win, wc, wout, *jobs.operands())
    return y, st, job_res


def _conv_step_kernel(x_ref, pre_ref, g_ref, win_ref, wc_ref, wout_ref, o_ref, u_ref, *, seq):
    x = x_ref[...]
    m = x.shape[0]
    b, u = _conv_front(x, g_ref, win_ref)
    pre = pre_ref[...]
    t = lax.rem(lax.broadcasted_iota(jnp.int32, (m, 1), 0), seq)
    back1 = jnp.where(t >= 1, pltpu.roll(u, 1, axis=0), pltpu.roll(pre, m - 1, axis=0))
    back2 = jnp.where(t >= 2, pltpu.roll(u, 2, axis=0), pre)
    y = wc_ref[0:1, :] * back2 + wc_ref[1:2, :] * back1 + wc_ref[2:3, :] * u
    o_ref[...] = x + _dot((b * y).astype(BF16), wout_ref[...])
    u_ref[...] = u


def _start_copies(pairs, sem):
    copies = [pltpu.make_async_copy(src, dst, sem.at[n]) for n, (src, dst) in enumerate(pairs)]
    for c in copies:
        c.start()
    return copies


def _decode_conv_ffn_kernel(x_ref, pre_ref, gmix_ref, wc_ref, gffn_ref, fin_ref,
                            win_hbm, wout_hbm, wg_hbm, wu_hbm, wd_hbm, u_ref, y_ref,
                            win_ref, wout_ref, wg_ref, wu_ref, wd_ref, x1_ref, act_ref, sem,
                            *, seq, conv_layer, ffn_layer, final_norm):
    copies = _start_copies(
        [(win_hbm.at[conv_layer], win_ref), (wout_hbm.at[conv_layer], wout_ref), (wg_hbm.at[ffn_layer], wg_ref),
         (wu_hbm.at[ffn_layer], wu_ref), (wd_hbm.at[ffn_layer], wd_ref)], sem)
    for c in copies[:2]:
        c.wait()
    _conv_step_kernel(x_ref, pre_ref, gmix_ref, win_ref, wc_ref, wout_ref, x1_ref, u_ref, seq=seq)
    for c in copies[2:]:
        c.wait()
    _ffn_rows(x1_ref, gffn_ref, wg_ref, wu_ref, wd_ref, fin_ref, y_ref, act_ref, slice(0, x_ref.shape[0]),
              final_norm=final_norm)


def _decode_conv_ffn(x, buf, g_mix, win, wc, wout, g_ffn, wg, wu, wd, fin, *, conv_layer, ffn_layer, seq,
                     final_norm):
    m, d = x.shape
    f = wg.shape[2]
    taps = CONV_WIDTH - 1
    assert wc.shape[1] == CONV_WIDTH == 3 and seq >= taps and m % seq == 0 and f % FF_CHUNK == 0
    pre = jnp.pad(buf, ((0, 0), (0, seq - taps), (0, 0))).reshape(m, d)
    full = pl.BlockSpec((m, d), lambda i: (0, 0))
    hbm = pl.BlockSpec(memory_space=pl.ANY)
    weights = [(d, 3 * d), (d, d), (d, f), (d, f), (f, d)]
    vmem = _vmem_limit([((m, d), F32, 5), ((m, f), BF16, 1)] + [(w, BF16, 1) for w in weights],
                       temporaries=12 * _nbytes((m, d), F32))
    u, y = pl.pallas_call(
        functools.partial(_decode_conv_ffn_kernel, seq=seq, conv_layer=conv_layer, ffn_layer=ffn_layer,
                          final_norm=final_norm),
        grid=(1,),
        in_specs=[full, full, _const_spec((1, d)), _layer_spec(wc, conv_layer), _const_spec((1, d)),
                  _const_spec((1, d))] + [hbm] * len(weights),
        out_specs=[full, full],
        out_shape=[jax.ShapeDtypeStruct((m, d), F32)] * 2,
        scratch_shapes=[pltpu.VMEM(w, BF16) for w in weights]
        + [pltpu.VMEM((m, d), F32), pltpu.VMEM((m, f), BF16), pltpu.SemaphoreType.DMA((len(weights),))],
        compiler_params=pltpu.CompilerParams(dimension_semantics=("arbitrary",), vmem_limit_bytes=vmem),
        name="decode_conv_ffn",
    )(x, pre, g_mix, wc, g_ffn, fin, win, wout, wg, wu, wd)
    return y, u.reshape(m // seq, seq, d)[:, seq - taps:, :]


def _rope_angles(inv_ref, pos):
    ang = pos.astype(F32) * inv_ref[...]
    return jnp.cos(ang), jnp.sin(ang)


def _rotate(z, cos, sin):
    half = z.shape[1] // 2
    x1, x2 = z[:, :half], z[:, half:]
    return jnp.concatenate([x1 * cos - x2 * sin, x1 * sin + x2 * cos], axis=1)


def _proj_head(hn, win_ref, h, cos, sin):
    ret_in = win_ref.shape[1]
    qk, vd = ret_in // 6, ret_in // 3
    dk, dv = qk // N_HEADS, vd // N_HEADS
    q = _rotate(_dot(hn, win_ref[:, h * dk:(h + 1) * dk]), cos, sin)
    k = _rotate(_dot(hn, win_ref[:, qk + h * dk:qk + (h + 1) * dk]), cos, sin) * dk ** -0.5
    v = _dot(hn, win_ref[:, 2 * qk + h * dv:2 * qk + (h + 1) * dv])
    sg = jax.nn.silu(_dot(hn, win_ref[:, 2 * qk + vd + h * dv:2 * qk + vd + (h + 1) * dv]))
    return q, k, v, sg


def _gn_gate(o, sg, gn):
    mu = jnp.mean(o, axis=-1, keepdims=True)
    var = jnp.mean(jnp.square(o - mu), axis=-1, keepdims=True)
    of = ((o - mu) * lax.rsqrt(var + GN_EPS)) * gn
    return (sg * of).astype(BF16)


def _ret_layer_seq_kernel(x_ref, g_ref, win_ref, cos_ref, sin_ref, gn_ref, wout_ref, *refs, jobs):
    job_src, (y_ref, s_ref), job_dst, act_ref = (
        refs[:jobs.n_in], refs[jobs.n_in:jobs.n_in + 2], refs[jobs.n_in + 2:-1], refs[-1])
    tm = x_ref.shape[1]
    dv = gn_ref.shape[1] // N_HEADS
    c_len = SEQ_CHUNK

    @pl.when(pl.program_id(1) == 0)
    def _():
        s_ref[...] = jnp.zeros(s_ref.shape, F32)

    jobs.run(job_src, job_dst, pl.program_id(0) * pl.num_programs(1) + pl.program_id(1))
    i = lax.broadcasted_iota(jnp.int32, (c_len, c_len), 0)
    j = lax.broadcasted_iota(jnp.int32, (c_len, c_len), 1)
    diff = (i - j).astype(F32)
    idx = lax.broadcasted_iota(jnp.int32, (c_len, 1), 0).astype(F32)
    for c in range(tm // c_len):
        rows = slice(c * c_len, (c + 1) * c_len)
        x = x_ref[0, rows, :]
        hn = _rmsnorm(x, g_ref[...]).astype(BF16)
        cos, sin = cos_ref[rows, :], sin_ref[rows, :]
        for h in range(N_HEADS):
            lg = _head_log_decay(h)
            decay = jnp.where(diff >= 0, jnp.exp(jnp.maximum(diff, 0.0) * lg), 0.0)
            cross_w = jnp.exp((idx + 1.0) * lg)
            state_w = jnp.exp((c_len - 1.0 - idx) * lg)
            chunk_decay = math.exp(c_len * lg)
            q, k, v, sg = _proj_head(hn, win_ref, h, cos, sin)
            qc, vc = q.astype(BF16), v.astype(BF16)
            cols = slice(h * dv, (h + 1) * dv)
            s = s_ref[0, h]
            scores = _dot_nt(qc, k.astype(BF16)) * decay
            o = _dot(scores.astype(BF16), vc) + _dot(qc, s.astype(BF16)) * cross_w
            s_ref[0, h] = s * chunk_decay + _dot_tn((k * state_w).astype(BF16), vc)
            act_ref[rows, cols] = _gn_gate(o, sg, gn_ref[:, cols])
        y_ref[0, rows, :] = x + _dot(act_ref[rows, :], wout_ref[...])


def _ret_layer_seq(x, g, win, cos, sin, gn, wout, *, layer, jobs):
    bsz, seq, d = x.shape
    ret_in, vd = win.shape[2], wout.shape[1]
    dk, dv = ret_in // 6 // N_HEADS, vd // N_HEADS
    tm = ROW_TILE
    assert seq % tm == 0 and tm % SEQ_CHUNK == 0 and cos.shape == (seq, dk // 2)
    grid = (bsz, seq // tm)
    job_in, job_out, job_shapes = jobs.specs(grid)
    row = pl.BlockSpec((1, tm, d), lambda b, t: (b, t, 0))
    rope = pl.BlockSpec((tm, dk // 2), lambda b, t: (t, 0))
    vmem = _vmem_limit(
        [((tm, d), F32, 4), ((d, ret_in), BF16, 1), ((vd, d), BF16, 1), ((tm, dk), F32, 4),
         ((N_HEADS, dk, dv), F32, 2), ((tm, vd), BF16, 1)] + jobs.vmem(),
        temporaries=8 * _nbytes((tm, d), F32))
    y, st, *job_res = pl.pallas_call(
        functools.partial(_ret_layer_seq_kernel, jobs=jobs),
        grid=grid,
        in_specs=[row, _const_spec((1, d)), _layer_spec(win, layer), rope, rope, _const_spec((1, vd)),
                  _layer_spec(wout, layer)] + job_in,
        out_specs=[row, pl.BlockSpec((1, N_HEADS, dk, dv), lambda b, t: (b, 0, 0, 0))] + job_out,
        out_shape=[jax.ShapeDtypeStruct((bsz, seq, d), F32),
                   jax.ShapeDtypeStruct((bsz, N_HEADS, dk, dv), F32)] + job_shapes,
        scratch_shapes=[pltpu.VMEM((tm, vd), BF16)],
        compiler_params=pltpu.CompilerParams(dimension_semantics=("arbitrary", "arbitrary"),
                                             vmem_limit_bytes=vmem),
        name="ret_layer_seq",
    )(x, g, win, cos, sin, gn, wout, *jobs.operands())
    return y, st, job_res


def _ret_proj_kernel(x_ref, g_ref, win_ref, inv_ref, q_ref, k_ref, v_ref, sg_ref, *, seq, pos0):
    tm = x_ref.shape[0]
    dk, dv = q_ref.shape[1] // N_HEADS, v_ref.shape[1] // N_HEADS
    hn = _rmsnorm(x_ref[...], g_ref[...]).astype(BF16)
    row = lax.broadcasted_iota(jnp.int32, (tm, dk // 2), 0)
    cos, sin = _rope_angles(inv_ref, pos0 + lax.rem(row, seq))
    for h in range(N_HEADS):
        q, k, v, sg = _proj_head(hn, win_ref, h, cos, sin)
        q_ref[:, h * dk:(h + 1) * dk] = q
        k_ref[:, h * dk:(h + 1) * dk] = k
        v_ref[:, h * dv:(h + 1) * dv] = v
        sg_ref[:, h * dv:(h + 1) * dv] = sg


def _ret_proj(x, g, win, inv, *, layer, seq, pos0):
    m, d = x.shape
    ret_in = win.shape[2]
    qk, vd = ret_in // 6, ret_in // 3
    tm = ROW_TILE
    assert m % tm == 0 and tm % seq == 0
    row = lambda n: pl.BlockSpec((tm, n), lambda i: (i, 0))
    vmem = _vmem_limit(
        [((tm, d), F32, 2), ((d, ret_in), BF16, 1), ((tm, qk), F32, 4), ((tm, vd), F32, 4)],
        temporaries=8 * _nbytes((tm, d), F32))
    return pl.pallas_call(
        functools.partial(_ret_proj_kernel, seq=seq, pos0=pos0),
        grid=(m // tm,),
        in_specs=[row(d), _const_spec((1, d)), _layer_spec(win, layer), _const_spec(inv.shape)],
        out_specs=[row(qk), row(qk), row(vd), row(vd)],
        out_shape=[jax.ShapeDtypeStruct((m, qk), F32), jax.ShapeDtypeStruct((m, qk), F32),
                   jax.ShapeDtypeStruct((m, vd), F32), jax.ShapeDtypeStruct((m, vd), F32)],
        compiler_params=pltpu.CompilerParams(dimension_semantics=("arbitrary",), vmem_limit_bytes=vmem),
        name="ret_proj",
    )(x, g, win, inv)


class _DecodeGroup:
    def __init__(self, q_ref, k_ref, v_ref, rows, seq):
        n = rows.stop - rows.start
        dk, dv = q_ref.shape[1] // N_HEADS, v_ref.shape[1] // N_HEADS
        self.seq = seq
        self.r = lax.broadcasted_iota(jnp.int32, (n, 1), 0)
        t = lax.rem(self.r, seq)
        tf = t.astype(F32)
        self.qb, self.kw, self.vb, self.inner, self.cross, self.cross_w = [], [], [], [], [], []
        for h in range(N_HEADS):
            lg = _head_log_decay(h)
            q = q_ref[rows, h * dk:(h + 1) * dk]
            k = k_ref[rows, h * dk:(h + 1) * dk]
            v = v_ref[rows, h * dv:(h + 1) * dv]
            acc = jnp.zeros((n, dv), F32)
            for dlt in range(seq):
                kd = k if dlt == 0 else pltpu.roll(k, dlt, axis=0)
                vd = v if dlt == 0 else pltpu.roll(v, dlt, axis=0)
                sc = jnp.sum(q * kd, axis=-1, keepdims=True) * math.exp(dlt * lg)
                acc = acc + jnp.where(t >= dlt, sc, 0.0) * vd
            self.inner.append(acc)
            self.cross.append(jnp.zeros((n, dv), F32))
            self.cross_w.append(jnp.exp((tf + 1.0) * lg))
            self.qb.append(q.astype(BF16))
            self.kw.append(k * jnp.exp((seq - 1.0 - tf) * lg))
            self.vb.append(v.astype(BF16))

    def advance(self, b, s0_ref, s_ref):
        own = (self.r >= b * self.seq) & (self.r < (b + 1) * self.seq)
        for h in range(N_HEADS):
            s0 = s0_ref[h]
            self.cross[h] = self.cross[h] + jnp.where(own, _dot(self.qb[h], s0.astype(BF16)), 0.0)
            kwb = jnp.where(own, self.kw[h], 0.0).astype(BF16)
            s_ref[h] = s0 * math.exp(self.seq * _head_log_decay(h)) + _dot_tn(kwb, self.vb[h])

    def output(self, h):
        return self.inner[h] + self.cross[h] * self.cross_w[h]


def _ffn_decode_kernel(x_ref, g_ref, wg_ref, wu_ref, wd_ref, fin_ref, q_ref, k_ref, v_ref, s0_hbm,
                       o_ref, ro_ref, s_hbm, act_ref, sin_ref, sout_ref, sem_in, sem_out,
                       *, final_norm, seq, layer):
    i = pl.program_id(0)
    last = pl.num_programs(0) - 1
    n_sub = x_ref.shape[0] // ROW_TILE
    per_group = STEP_ROWS // seq
    per_step = n_sub * per_group
    n_chunks = wg_ref.shape[1] // FF_CHUNK
    dv = v_ref.shape[1] // N_HEADS
    run_after = {(u + 1) * n_chunks // per_group - 1: u for u in range(per_group)}

    def in_copy(n, slot):
        return pltpu.make_async_copy(s0_hbm.at[layer, n], sin_ref.at[slot], sem_in.at[slot])

    def out_copy(n, slot):
        return pltpu.make_async_copy(sout_ref.at[slot], s_hbm.at[n], sem_out.at[slot])

    lead = IN_SLOTS - 1

    @pl.when(i == 0)
    def _():
        for n in range(lead):
            in_copy(n, n % IN_SLOTS).start()

    def advance(group, sub, b):
        l = sub * per_group + b
        n = i * per_step + l
        slot, oslot = l % IN_SLOTS, l % OUT_SLOTS
        in_copy(n, slot).wait()
        if l + lead < per_step:
            in_copy(n + lead, (l + lead) % IN_SLOTS).start()
        else:
            pl.when(i < last)(lambda: in_copy(n + lead, (l + lead) % IN_SLOTS).start())
        if l >= OUT_SLOTS:
            out_copy(n - OUT_SLOTS, oslot).wait()
        else:
            pl.when(i > 0)(lambda: out_copy(n - OUT_SLOTS, oslot).wait())
        group.advance(b, sin_ref.at[slot], sout_ref.at[oslot])
        out_copy(n, oslot).start()

    for sub in range(n_sub):
        rows16 = slice(sub * STEP_ROWS, (sub + 1) * STEP_ROWS)
        group = _DecodeGroup(q_ref, k_ref, v_ref, rows16, seq)

        def after_chunk(j, group=group, sub=sub):
            if j in run_after:
                advance(group, sub, run_after[j])

        _ffn_rows(x_ref, g_ref, wg_ref, wu_ref, wd_ref, fin_ref, o_ref, act_ref,
                  slice(sub * ROW_TILE, (sub + 1) * ROW_TILE), final_norm=final_norm, after_chunk=after_chunk)
        for h in range(N_HEADS):
            ro_ref[rows16, h * dv:(h + 1) * dv] = group.output(h)

    @pl.when(i == last)
    def _():
        for l in range(per_step - OUT_SLOTS, per_step):
            out_copy(i * per_step + l, l % OUT_SLOTS).wait()


def _ffn_decode(x, g, wg, wu, wd, fin, q, k, v, s0, *, layer, final_norm, state_layer, seq):
    m, d = x.shape
    f = wg.shape[2]
    ms, qk = q.shape
    vd = v.shape[1]
    _, bsz, _, dk, dv = s0.shape
    tm = 2 * ROW_TILE
    steps = m // tm
    rows_s = (tm // ROW_TILE) * STEP_ROWS
    assert m % tm == 0 and f % FF_CHUNK == 0 and f // FF_CHUNK >= STEP_ROWS // seq
    assert seq <= SEQ_CHUNK and STEP_ROWS % seq == 0 and ms == bsz * seq and ms == steps * rows_s
    assert (rows_s // seq) % IN_SLOTS == 0 and (rows_s // seq) % OUT_SLOTS == 0
    row = pl.BlockSpec((tm, d), lambda i: (i, 0))
    srow = lambda n: pl.BlockSpec((rows_s, n), lambda i: (i, 0))
    hbm = pl.BlockSpec(memory_space=pl.ANY)
    state = (N_HEADS, dk, dv)
    vmem = _vmem_limit(
        [((tm, d), F32, 4), ((d, f), BF16, 2), ((f, d), BF16, 1), ((tm, f), BF16, 1),
         (state, F32, IN_SLOTS + OUT_SLOTS),
         ((rows_s, vd), F32, 8)],
        temporaries=8 * _nbytes((tm, d), F32))
    return pl.pallas_call(
        functools.partial(_ffn_decode_kernel, final_norm=final_norm, seq=seq, layer=state_layer),
        grid=(steps,),
        in_specs=[row, _const_spec((1, d)), _layer_spec(wg, layer), _layer_spec(wu, layer),
                  _layer_spec(wd, layer), _const_spec((1, d)), srow(qk), srow(qk), srow(vd), hbm],
        out_specs=[row, srow(vd), hbm],
        out_shape=[jax.ShapeDtypeStruct((m, d), F32), jax.ShapeDtypeStruct((ms, vd), F32),
                   jax.ShapeDtypeStruct(s0.shape[1:], F32)],
        scratch_shapes=[pltpu.VMEM((tm, f), BF16), pltpu.VMEM((IN_SLOTS,) + state, F32),
                        pltpu.VMEM((OUT_SLOTS,) + state, F32), pltpu.SemaphoreType.DMA((IN_SLOTS,)),
                        pltpu.SemaphoreType.DMA((OUT_SLOTS,))],
        compiler_params=pltpu.CompilerParams(dimension_semantics=("arbitrary",), vmem_limit_bytes=vmem),
        name="ffn_decode_ret",
    )(x, g, wg, wu, wd, fin, q, k, v, s0)


def _ret_out_kernel(x_ref, o_ref, sg_ref, gn_ref, wout_ref, y_ref, act_ref):
    dv = o_ref.shape[1] // N_HEADS
    for h in range(N_HEADS):
        cols = slice(h * dv, (h + 1) * dv)
        act_ref[:, cols] = _gn_gate(o_ref[:, cols], sg_ref[:, cols], gn_ref[:, cols])
    y_ref[...] = x_ref[...] + _dot(act_ref[...], wout_ref[...])


def _decode_ret_ffn_kernel(x_ref, o_ref, sg_ref, gn_ref, gffn_ref, fin_ref, rwout_hbm, wg_hbm, wu_hbm, wd_hbm,
                           y_ref, rwout_ref, wg_ref, wu_ref, wd_ref, x1_ref, ract_ref, act_ref, sem,
                           *, ret_layer, ffn_layer, final_norm):
    copies = _start_copies(
        [(rwout_hbm.at[ret_layer], rwout_ref), (wg_hbm.at[ffn_layer], wg_ref), (wu_hbm.at[ffn_layer], wu_ref),
         (wd_hbm.at[ffn_layer], wd_ref)], sem)
    copies[0].wait()
    _ret_out_kernel(x_ref, o_ref, sg_ref, gn_ref, rwout_ref, x1_ref, ract_ref)
    for c in copies[1:]:
        c.wait()
    _ffn_rows(x1_ref, gffn_ref, wg_ref, wu_ref, wd_ref, fin_ref, y_ref, act_ref, slice(0, x_ref.shape[0]),
              final_norm=final_norm)


def _decode_ret_ffn(x, o, sg, gn, rwout, g_ffn, wg, wu, wd, fin, *, ret_layer, ffn_layer, final_norm):
    m, d = x.shape
    vd, f = o.shape[1], wg.shape[2]
    assert f % FF_CHUNK == 0
    full = lambda n: pl.BlockSpec((m, n), lambda i: (0, 0))
    hbm = pl.BlockSpec(memory_space=pl.ANY)
    weights = [(vd, d), (d, f), (d, f), (f, d)]
    vmem = _vmem_limit([((m, d), F32, 3), ((m, vd), F32, 2), ((m, vd), BF16, 1), ((m, f), BF16, 1)]
                       + [(w, BF16, 1) for w in weights], temporaries=8 * _nbytes((m, d), F32))
    return pl.pallas_call(
        functools.partial(_decode_ret_ffn_kernel, ret_layer=ret_layer, ffn_layer=ffn_layer,
                          final_norm=final_norm),
        grid=(1,),
        in_specs=[full(d), full(vd), full(vd), _const_spec((1, vd)), _const_spec((1, d)), _const_spec((1, d))]
        + [hbm] * len(weights),
        out_specs=full(d),
        out_shape=jax.ShapeDtypeStruct((m, d), F32),
        scratch_shapes=[pltpu.VMEM(w, BF16) for w in weights]
        + [pltpu.VMEM((m, d), F32), pltpu.VMEM((m, vd), BF16), pltpu.VMEM((m, f), BF16),
           pltpu.SemaphoreType.DMA((len(weights),))],
        compiler_params=pltpu.CompilerParams(dimension_semantics=("arbitrary",), vmem_limit_bytes=vmem),
        name="decode_ret_ffn",
    )(x, o, sg, gn, g_ffn, fin, rwout, wg, wu, wd)


def kernel(x_prompt, x_sample, state_conv, state_ret, norm_mix, norm_ffn, conv_w_in, conv_w, conv_w_out,
           ret_w_in, ret_gn, ret_w_out, ffn_w_gate, ffn_w_up, ffn_w_down, final_norm):
    bp, lp, d = x_prompt.shape
    bs, ls, _ = x_sample.shape
    depth = norm_mix.shape[0]
    dk = ret_w_in.shape[2] // 6 // N_HEADS
    inv = (ROPE_BASE ** (-jnp.arange(dk // 2, dtype=F32) / (dk // 2)))[None, :]
    fin = final_norm[None, :]
    conv_w_in, conv_w_out = conv_w_in.astype(BF16), conv_w_out.astype(BF16)
    assert depth % N_MIXERS == 0

    xp = x_prompt
    xs = x_sample.reshape(bs * ls, d)
    conv_p, conv_s, ret_p, ret_s = [], [], [], []
    for j in range(depth // N_MIXERS):
        ic, ir = N_MIXERS * j, N_MIXERS * j + 1
        last = ir == depth - 1
        g_ffn_c, g_ffn_r = norm_ffn[ic][None, :], norm_ffn[ir][None, :]
        gn = ret_gn[j][None, :]

        jobs = _SideJobs([(ffn_w_gate, ic), (ffn_w_up, ic), (ffn_w_down, ic), (ret_w_in, j), (ret_w_out, j)],
                         rope=(inv, lp))
        xp, st, (wg, wu, wd, rwin, rwout, cos, sin) = _conv_mixer_seq(
            xp, norm_mix[ic][None, :], conv_w_in, conv_w, conv_w_out, layer=j, jobs=jobs)
        conv_p.append(st)
        xs, st = _decode_conv_ffn(xs, state_conv[j], norm_mix[ic][None, :], conv_w_in, conv_w, conv_w_out,
                                  g_ffn_c, wg, wu, wd, fin, conv_layer=j, ffn_layer=0, seq=ls, final_norm=False)
        conv_s.append(st)
        q, k, v, sg = _ret_proj(xs, norm_mix[ir][None, :], rwin, inv, layer=0, seq=ls, pos0=PAST_LEN)
        xp, ret_o, st = _ffn_decode(xp.reshape(bp * lp, d), g_ffn_c, wg, wu, wd, fin, q, k, v, state_ret,
                                    layer=0, final_norm=False, state_layer=j, seq=ls)
        ret_s.append(st)

        jobs = _SideJobs([(ffn_w_gate, ir), (ffn_w_up, ir), (ffn_w_down, ir)])
        xp, st, (wg, wu, wd) = _ret_layer_seq(xp.reshape(bp, lp, d), norm_mix[ir][None, :], rwin, cos, sin, gn,
                                              rwout, layer=0, jobs=jobs)
        ret_p.append(st)
        xs = _decode_ret_ffn(xs, ret_o, sg, gn, rwout, g_ffn_r, wg, wu, wd, fin, ret_layer=0, ffn_layer=0,
                             final_norm=last)
        xp = _ffn(xp.reshape(bp * lp, d), g_ffn_r, wg, wu, wd, fin, layer=0, final_norm=last).reshape(bp, lp, d)
    return (xp, xs.reshape(bs, ls, d), jnp.stack(conv_p), jnp.stack(conv_s), jnp.stack(ret_p),
            jnp.stack(ret_s))
```

```python
import functools
import math

import jax
import jax.numpy as jnp
from jax import lax
from jax.experimental import pallas as pl
from jax.experimental.pallas import tpu as pltpu

F32 = jnp.float32
BF16 = jnp.bfloat16

N_MIXERS = 2
CONV_WIDTH = 3
N_HEADS = 4
PAST_LEN = 16384
RMS_EPS = 1e-6
GN_EPS = 1e-6
ROPE_BASE = 10000.0

V7X_VMEM_LIMIT_CAP = 60000 * 1024
V7X_MXU_DIM = 256
SUBLANES = 8
ROW_TILE = 512
FF_CHUNK = V7X_MXU_DIM
SEQ_CHUNK = V7X_MXU_DIM
BF16_ROWS = 16
STEP_ROWS = BF16_ROWS
CAST_CHUNKS = 16
IN_SLOTS, OUT_SLOTS = 4, 2


def _nbytes(shape, dtype):
    return math.prod(shape) * jnp.dtype(dtype).itemsize


def _vmem_limit(buffers, temporaries):
    need = sum(_nbytes(s, d) * n for s, d, n in buffers) + temporaries
    return min(int(need * 1.25) + (4 << 20), V7X_VMEM_LIMIT_CAP)


def _const_spec(shape):
    return pl.BlockSpec(shape, lambda *_: (0,) * len(shape), pipeline_mode=pl.Buffered(1))


def _layer_spec(w, layer):
    return pl.BlockSpec((None,) + w.shape[1:], lambda *_: (layer, 0, 0), pipeline_mode=pl.Buffered(1))


class _CastJobs:
    def __init__(self, weights):
        self.weights = list(weights)
        self.n = len(self.weights)

    @staticmethod
    def chunk(grid_idx, grid):
        step = grid_idx[0]
        for extent, idx in zip(grid[1:], grid_idx[1:]):
            step = step * extent + idx
        return jnp.minimum(step, CAST_CHUNKS - 1)

    def specs(self, grid):
        assert math.prod(grid) >= CAST_CHUNKS
        ins, outs, shapes = [], [], []
        for w, layer in self.weights:
            rows, cols = w.shape[1:]
            assert rows % (CAST_CHUNKS * BF16_ROWS) == 0
            blk = (None, rows // CAST_CHUNKS, cols)
            ins.append(pl.BlockSpec(blk, lambda *g, layer=layer: (layer, self.chunk(g, grid), 0)))
            outs.append(pl.BlockSpec(blk, lambda *g: (0, self.chunk(g, grid), 0)))
            shapes.append(jax.ShapeDtypeStruct((1, rows, cols), BF16))
        return ins, outs, shapes

    def operands(self):
        return [w for w, _ in self.weights]

    def vmem(self):
        return [((w.shape[1] // CAST_CHUNKS, w.shape[2]), F32, 3) for w, _ in self.weights]

    def run(self, src_refs, dst_refs, step):
        @pl.when(step < CAST_CHUNKS)
        def _():
            for src, dst in zip(src_refs, dst_refs):
                dst[...] = src[...].astype(BF16)


def _rmsnorm(x, g):
    ms = jnp.mean(x * x, axis=-1, keepdims=True)
    return (x * lax.rsqrt(ms + RMS_EPS)) * g


def _dot(a, b):
    return jnp.dot(a, b, preferred_element_type=F32)


def _dot_nt(a, b):
    return lax.dot_general(a, b, (((1,), (1,)), ((), ())), preferred_element_type=F32)


def _dot_tn(a, b):
    return lax.dot_general(a, b, (((0,), (0,)), ((), ())), preferred_element_type=F32)


def _head_log_decay(h):
    return math.log(1.0 - 2.0 ** (-5.0 - h))


def _ffn_rows(x_ref, g_ref, wg_ref, wu_ref, wd_ref, fin_ref, o_ref, act_ref, rows, *, final_norm,
              after_chunk=None):
    x = x_ref[rows, :]
    n = _rmsnorm(x, g_ref[...]).astype(BF16)
    for j in range(wg_ref.shape[1] // FF_CHUNK):
        cols = slice(j * FF_CHUNK, (j + 1) * FF_CHUNK)
        gate = _dot(n, wg_ref[:, cols])
        up = _dot(n, wu_ref[:, cols])
        act_ref[rows, cols] = (jax.nn.silu(gate) * up).astype(BF16)
        if after_chunk is not None:
            after_chunk(j)
    y = x + _dot(act_ref[rows, :], wd_ref[...])
    if final_norm:
        y = _rmsnorm(y, fin_ref[...])
    o_ref[rows, :] = y


def _ffn_kernel(x_ref, g_ref, wg_ref, wu_ref, wd_ref, fin_ref, o_ref, act_ref, *, final_norm):
    for r0 in range(0, x_ref.shape[0], ROW_TILE):
        _ffn_rows(x_ref, g_ref, wg_ref, wu_ref, wd_ref, fin_ref, o_ref, act_ref, slice(r0, r0 + ROW_TILE),
                  final_norm=final_norm)


def _ffn(x, g, wg, wu, wd, fin, *, layer, final_norm):
    m, d = x.shape
    f = wg.shape[2]
    tm = min(2 * ROW_TILE, m)
    assert m % tm == 0 and tm % ROW_TILE == 0 and f % FF_CHUNK == 0
    row = pl.BlockSpec((tm, d), lambda i: (i, 0))
    vmem = _vmem_limit(
        [((tm, d), F32, 4), ((d, f), BF16, 2), ((f, d), BF16, 1), ((tm, f), BF16, 1)],
        temporaries=6 * _nbytes((tm, d), F32))
    return pl.pallas_call(
        functools.partial(_ffn_kernel, final_norm=final_norm),
        grid=(m // tm,),
        in_specs=[row, _const_spec((1, d)), _layer_spec(wg, layer), _layer_spec(wu, layer),
                  _layer_spec(wd, layer), _const_spec((1, d))],
        out_specs=row,
        out_shape=jax.ShapeDtypeStruct((m, d), F32),
        scratch_shapes=[pltpu.VMEM((tm, f), BF16)],
        compiler_params=pltpu.CompilerParams(dimension_semantics=("arbitrary",), vmem_limit_bytes=vmem),
        name="ffn_final" if final_norm else "ffn",
    )(x, g, wg, wu, wd, fin)


def _conv_front(x, g_ref, win_ref):
    d = x.shape[1]
    hn = _rmsnorm(x, g_ref[...]).astype(BF16)
    b = _dot(hn, win_ref[:, 0:d])
    c = _dot(hn, win_ref[:, d:2 * d])
    h = _dot(hn, win_ref[:, 2 * d:3 * d])
    return b, c * h


def _conv_seq_kernel(x_ref, g_ref, win_ref, wc_ref, wout_ref, *refs, jobs):
    job_src, (o_ref, st_ref), job_dst, ubuf_ref = (
        refs[:jobs.n], refs[jobs.n:jobs.n + 2], refs[jobs.n + 2:-1], refs[-1])
    tm, d = x_ref.shape[1:]

    @pl.when(pl.program_id(1) == 0)
    def _():
        ubuf_ref[0:SUBLANES, :] = jnp.zeros((SUBLANES, d), F32)

    jobs.run(job_src, job_dst, pl.program_id(0) * pl.num_programs(1) + pl.program_id(1))
    for r0 in range(0, tm, ROW_TILE):
        x = x_ref[0, r0:r0 + ROW_TILE, :]
        b, u = _conv_front(x, g_ref, win_ref)
        ubuf_ref[SUBLANES + r0:SUBLANES + r0 + ROW_TILE, :] = u
        y = (wc_ref[0:1, :] * ubuf_ref[SUBLANES - 2 + r0:SUBLANES - 2 + r0 + ROW_TILE, :]
             + wc_ref[1:2, :] * ubuf_ref[SUBLANES - 1 + r0:SUBLANES - 1 + r0 + ROW_TILE, :]
             + wc_ref[2:3, :] * u)
        o_ref[0, r0:r0 + ROW_TILE, :] = x + _dot((b * y).astype(BF16), wout_ref[...])
    st_ref[0] = ubuf_ref[SUBLANES + tm - (CONV_WIDTH - 1):SUBLANES + tm, :]
    ubuf_ref[0:SUBLANES, :] = ubuf_ref[tm:tm + SUBLANES, :]


def _conv_mixer_seq(x, g, win, wc, wout, *, layer, jobs):
    bsz, seq, d = x.shape
    tm = 2 * ROW_TILE
    assert seq % tm == 0 and wc.shape[1] == CONV_WIDTH == 3
    grid = (bsz, seq // tm)
    job_in, job_out, job_shapes = jobs.specs(grid)
    row = pl.BlockSpec((1, tm, d), lambda b, t: (b, t, 0))
    vmem = _vmem_limit(
        [((tm, d), F32, 4), ((d, 3 * d), BF16, 1), ((d, d), BF16, 1), ((tm + SUBLANES, d), F32, 1)]
        + jobs.vmem(), temporaries=5 * _nbytes((tm, d), F32))
    y, st, *job_res = pl.pallas_call(
        functools.partial(_conv_seq_kernel, jobs=jobs),
        grid=grid,
        in_specs=[row, _const_spec((1, d)), _layer_spec(win, layer), _layer_spec(wc, layer),
                  _layer_spec(wout, layer)] + job_in,
        out_specs=[row, pl.BlockSpec((1, CONV_WIDTH - 1, d), lambda b, t: (b, 0, 0))] + job_out,
        out_shape=[jax.ShapeDtypeStruct((bsz, seq, d), F32),
                   jax.ShapeDtypeStruct((bsz, CONV_WIDTH - 1, d), F32)] + job_shapes,
        scratch_shapes=[pltpu.VMEM((tm + SUBLANES, d), F32)],
        compiler_params=pltpu.CompilerParams(dimension_semantics=("arbitrary", "arbitrary"),
                                             vmem_limit_bytes=vmem),
        name="conv_mixer_seq",
    )(x, g, win, wc, wout, *jobs.operands())
    return y, st, job_res


def _conv_step_kernel(x_ref, pre_ref, g_ref, win_ref, wc_ref, wout_ref, o_ref, u_ref, *, seq):
    x = x_ref[...]
    m = x.shape[0]
    b, u = _conv_front(x, g_ref, win_ref)
    pre = pre_ref[...]
    t = lax.rem(lax.broadcasted_iota(jnp.int32, (m, 1), 0), seq)
    back1 = jnp.where(t >= 1, pltpu.roll(u, 1, axis=0), pltpu.roll(pre, m - 1, axis=0))
    back2 = jnp.where(t >= 2, pltpu.roll(u, 2, axis=0), pre)
    y = wc_ref[0:1, :] * back2 + wc_ref[1:2, :] * back1 + wc_ref[2:3, :] * u
    o_ref[...] = x + _dot((b * y).astype(BF16), wout_ref[...])
    u_ref[...] = u


def _copies(pairs, sem):
    return [pltpu.make_async_copy(src, dst, sem.at[n]) for n, (src, dst) in enumerate(pairs)]


def _decode_conv_ffn_kernel(x_ref, pre_ref, gmix_ref, wc_ref, gffn_ref, fin_ref, inv_ref,
                            win_hbm, wout_hbm, wg_hbm, wu_hbm, wd_hbm, u_ref, y_ref, cos_ref, sin_ref,
                            win_ref, wout_ref, wg_ref, wu_ref, wd_ref, x1_ref, act_ref, sem,
                            *, seq, conv_layer, ffn_layer, final_norm):
    mixer_w, ffn_w = 2, 3
    copies = _copies(
        [(win_hbm.at[conv_layer], win_ref), (wout_hbm.at[conv_layer], wout_ref), (wg_hbm.at[ffn_layer], wg_ref),
         (wu_hbm.at[ffn_layer], wu_ref), (wd_hbm.at[ffn_layer], wd_ref)], sem)
    for c in copies[:mixer_w]:
        c.start()
    pos = lax.broadcasted_iota(jnp.int32, cos_ref.shape, 0)
    cos_ref[...], sin_ref[...] = _rope_angles(inv_ref, pos)
    for c in copies[:mixer_w]:
        c.wait()
    for c in copies[mixer_w:mixer_w + ffn_w]:
        c.start()
    _conv_step_kernel(x_ref, pre_ref, gmix_ref, win_ref, wc_ref, wout_ref, x1_ref, u_ref, seq=seq)
    for c in copies[mixer_w:mixer_w + ffn_w]:
        c.wait()
    _ffn_rows(x1_ref, gffn_ref, wg_ref, wu_ref, wd_ref, fin_ref, y_ref, act_ref, slice(0, x_ref.shape[0]),
              final_norm=final_norm)


def _decode_conv_ffn(x, buf, g_mix, win, wc, wout, g_ffn, wg, wu, wd, fin, inv, *, conv_layer, ffn_layer, seq,
                     final_norm, rope_len):
    m, d = x.shape
    f = wg.shape[2]
    taps = CONV_WIDTH - 1
    assert wc.shape[1] == CONV_WIDTH == 3 and seq >= taps and m % seq == 0 and f % FF_CHUNK == 0
    pre = jnp.pad(buf, ((0, 0), (0, seq - taps), (0, 0))).reshape(m, d)
    full = pl.BlockSpec((m, d), lambda i: (0, 0))
    table = pl.BlockSpec((rope_len, inv.shape[1]), lambda i: (0, 0))
    hbm = pl.BlockSpec(memory_space=pl.ANY)
    weights = [(d, 3 * d), (d, d), (d, f), (d, f), (f, d)]
    vmem = _vmem_limit([((m, d), F32, 5), ((m, f), BF16, 1), ((rope_len, inv.shape[1]), F32, 2)]
                       + [(w, BF16, 1) for w in weights], temporaries=12 * _nbytes((m, d), F32))
    u, y, cos, sin = pl.pallas_call(
        functools.partial(_decode_conv_ffn_kernel, seq=seq, conv_layer=conv_layer, ffn_layer=ffn_layer,
                          final_norm=final_norm),
        grid=(1,),
        in_specs=[full, full, _const_spec((1, d)), _layer_spec(wc, conv_layer), _const_spec((1, d)),
                  _const_spec((1, d)), _const_spec(inv.shape)] + [hbm] * len(weights),
        out_specs=[full, full, table, table],
        out_shape=[jax.ShapeDtypeStruct((m, d), F32)] * 2
        + [jax.ShapeDtypeStruct((rope_len, inv.shape[1]), F32)] * 2,
        scratch_shapes=[pltpu.VMEM(w, BF16) for w in weights]
        + [pltpu.VMEM((m, d), F32), pltpu.VMEM((m, f), BF16), pltpu.SemaphoreType.DMA((len(weights),))],
        compiler_params=pltpu.CompilerParams(dimension_semantics=("arbitrary",), vmem_limit_bytes=vmem),
        name="decode_conv_ffn",
    )(x, pre, g_mix, wc, g_ffn, fin, inv, win, wout, wg, wu, wd)
    return y, u.reshape(m // seq, seq, d)[:, seq - taps:, :], cos, sin


def _rope_angles(inv_ref, pos):
    ang = pos.astype(F32) * inv_ref[...]
    return jnp.cos(ang), jnp.sin(ang)


def _rotate(z, cos, sin):
    half = z.shape[1] // 2
    x1, x2 = z[:, :half], z[:, half:]
    return jnp.concatenate([x1 * cos - x2 * sin, x1 * sin + x2 * cos], axis=1)


def _proj_head(hn, win_ref, h, cos, sin):
    ret_in = win_ref.shape[1]
    qk, vd = ret_in // 6, ret_in // 3
    dk, dv = qk // N_HEADS, vd // N_HEADS
    q = _rotate(_dot(hn, win_ref[:, h * dk:(h + 1) * dk]), cos, sin)
    k = _rotate(_dot(hn, win_ref[:, qk + h * dk:qk + (h + 1) * dk]), cos, sin) * dk ** -0.5
    v = _dot(hn, win_ref[:, 2 * qk + h * dv:2 * qk + (h + 1) * dv])
    sg = jax.nn.silu(_dot(hn, win_ref[:, 2 * qk + vd + h * dv:2 * qk + vd + (h + 1) * dv]))
    return q, k, v, sg


def _gn_gate(o, sg, gn):
    mu = jnp.mean(o, axis=-1, keepdims=True)
    var = jnp.mean(jnp.square(o - mu), axis=-1, keepdims=True)
    of = ((o - mu) * lax.rsqrt(var + GN_EPS)) * gn
    return (sg * of).astype(BF16)


def _ret_layer_seq_kernel(x_ref, g_ref, win_ref, cos_ref, sin_ref, gn_ref, wout_ref, *refs, jobs):
    job_src, (y_ref, s_ref), job_dst, act_ref = (
        refs[:jobs.n], refs[jobs.n:jobs.n + 2], refs[jobs.n + 2:-1], refs[-1])
    tm = x_ref.shape[1]
    dv = gn_ref.shape[1] // N_HEADS
    c_len = SEQ_CHUNK

    @pl.when(pl.program_id(1) == 0)
    def _():
        s_ref[...] = jnp.zeros(s_ref.shape, F32)

    jobs.run(job_src, job_dst, pl.program_id(0) * pl.num_programs(1) + pl.program_id(1))
    i = lax.broadcasted_iota(jnp.int32, (c_len, c_len), 0)
    j = lax.broadcasted_iota(jnp.int32, (c_len, c_len), 1)
    diff = (i - j).astype(F32)
    idx = lax.broadcasted_iota(jnp.int32, (c_len, 1), 0).astype(F32)
    for c in range(tm // c_len):
        rows = slice(c * c_len, (c + 1) * c_len)
        x = x_ref[0, rows, :]
        hn = _rmsnorm(x, g_ref[...]).astype(BF16)
        cos, sin = cos_ref[rows, :], sin_ref[rows, :]
        for h in range(N_HEADS):
            lg = _head_log_decay(h)
            decay = jnp.where(diff >= 0, jnp.exp(jnp.maximum(diff, 0.0) * lg), 0.0)
            cross_w = jnp.exp((idx + 1.0) * lg)
            state_w = jnp.exp((c_len - 1.0 - idx) * lg)
            chunk_decay = math.exp(c_len * lg)
            q, k, v, sg = _proj_head(hn, win_ref, h, cos, sin)
            qc, vc = q.astype(BF16), v.astype(BF16)
            cols = slice(h * dv, (h + 1) * dv)
            s = s_ref[0, h]
            scores = _dot_nt(qc, k.astype(BF16)) * decay
            o = _dot(scores.astype(BF16), vc) + _dot(qc, s.astype(BF16)) * cross_w
            s_ref[0, h] = s * chunk_decay + _dot_tn((k * state_w).astype(BF16), vc)
            act_ref[rows, cols] = _gn_gate(o, sg, gn_ref[:, cols])
        y_ref[0, rows, :] = x + _dot(act_ref[rows, :], wout_ref[...])


def _ret_layer_seq(x, g, win, cos, sin, gn, wout, *, layer, jobs):
    bsz, seq, d = x.shape
    ret_in, vd = win.shape[2], wout.shape[1]
    dk, dv = ret_in // 6 // N_HEADS, vd // N_HEADS
    tm = 2 * ROW_TILE
    assert seq % tm == 0 and tm % SEQ_CHUNK == 0 and cos.shape == (seq, dk // 2)
    grid = (bsz, seq // tm)
    job_in, job_out, job_shapes = jobs.specs(grid)
    row = pl.BlockSpec((1, tm, d), lambda b, t: (b, t, 0))
    rope = pl.BlockSpec((tm, dk // 2), lambda b, t: (t, 0))
    vmem = _vmem_limit(
        [((tm, d), F32, 4), ((d, ret_in), BF16, 1), ((vd, d), BF16, 1), ((tm, dk), F32, 4),
         ((N_HEADS, dk, dv), F32, 2), ((tm, vd), BF16, 1)] + jobs.vmem(),
        temporaries=8 * _nbytes((tm, d), F32))
    y, st, *job_res = pl.pallas_call(
        functools.partial(_ret_layer_seq_kernel, jobs=jobs),
        grid=grid,
        in_specs=[row, _const_spec((1, d)), _layer_spec(win, layer), rope, rope, _const_spec((1, vd)),
                  _layer_spec(wout, layer)] + job_in,
        out_specs=[row, pl.BlockSpec((1, N_HEADS, dk, dv), lambda b, t: (b, 0, 0, 0))] + job_out,
        out_shape=[jax.ShapeDtypeStruct((bsz, seq, d), F32),
                   jax.ShapeDtypeStruct((bsz, N_HEADS, dk, dv), F32)] + job_shapes,
        scratch_shapes=[pltpu.VMEM((tm, vd), BF16)],
        compiler_params=pltpu.CompilerParams(dimension_semantics=("arbitrary", "arbitrary"),
                                             vmem_limit_bytes=vmem),
        name="ret_layer_seq",
    )(x, g, win, cos, sin, gn, wout, *jobs.operands())
    return y, st, job_res


def _ret_proj_kernel(x_ref, g_ref, win_ref, inv_ref, q_ref, k_ref, v_ref, sg_ref, *, seq, pos0):
    tm = x_ref.shape[0]
    dk, dv = q_ref.shape[1] // N_HEADS, v_ref.shape[1] // N_HEADS
    hn = _rmsnorm(x_ref[...], g_ref[...]).astype(BF16)
    row = lax.broadcasted_iota(jnp.int32, (tm, dk // 2), 0)
    cos, sin = _rope_angles(inv_ref, pos0 + lax.rem(row, seq))
    for h in range(N_HEADS):
        q, k, v, sg = _proj_head(hn, win_ref, h, cos, sin)
        q_ref[:, h * dk:(h + 1) * dk] = q
        k_ref[:, h * dk:(h + 1) * dk] = k
        v_ref[:, h * dv:(h + 1) * dv] = v
        sg_ref[:, h * dv:(h + 1) * dv] = sg


def _ret_proj(x, g, win, inv, *, layer, seq, pos0):
    m, d = x.shape
    ret_in = win.shape[2]
    qk, vd = ret_in // 6, ret_in // 3
    tm = ROW_TILE
    assert m % tm == 0 and tm % seq == 0
    row = lambda n: pl.BlockSpec((tm, n), lambda i: (i, 0))
    vmem = _vmem_limit(
        [((tm, d), F32, 2), ((d, ret_in), BF16, 1), ((tm, qk), F32, 4), ((tm, vd), F32, 4)],
        temporaries=8 * _nbytes((tm, d), F32))
    return pl.pallas_call(
        functools.partial(_ret_proj_kernel, seq=seq, pos0=pos0),
        grid=(m // tm,),
        in_specs=[row(d), _const_spec((1, d)), _layer_spec(win, layer), _const_spec(inv.shape)],
        out_specs=[row(qk), row(qk), row(vd), row(vd)],
        out_shape=[jax.ShapeDtypeStruct((m, qk), F32), jax.ShapeDtypeStruct((m, qk), F32),
                   jax.ShapeDtypeStruct((m, vd), F32), jax.ShapeDtypeStruct((m, vd), F32)],
        compiler_params=pltpu.CompilerParams(dimension_semantics=("arbitrary",), vmem_limit_bytes=vmem),
        name="ret_proj",
    )(x, g, win, inv)


class _DecodeGroup:
    def __init__(self, q_ref, k_ref, v_ref, rows, seq):
        n = rows.stop - rows.start
        dk, dv = q_ref.shape[1] // N_HEADS, v_ref.shape[1] // N_HEADS
        self.seq = seq
        self.r = lax.broadcasted_iota(jnp.int32, (n, 1), 0)
        t = lax.rem(self.r, seq)
        tf = t.astype(F32)
        self.qb, self.kw, self.vb, self.inner, self.cross, self.cross_w = [], [], [], [], [], []
        for h in range(N_HEADS):
            lg = _head_log_decay(h)
            q = q_ref[rows, h * dk:(h + 1) * dk]
            k = k_ref[rows, h * dk:(h + 1) * dk]
            v = v_ref[rows, h * dv:(h + 1) * dv]
            acc = jnp.zeros((n, dv), F32)
            for dlt in range(seq):
                kd = k if dlt == 0 else pltpu.roll(k, dlt, axis=0)
                vd = v if dlt == 0 else pltpu.roll(v, dlt, axis=0)
                sc = jnp.sum(q * kd, axis=-1, keepdims=True) * math.exp(dlt * lg)
                acc = acc + jnp.where(t >= dlt, sc, 0.0) * vd
            self.inner.append(acc)
            self.cross.append(jnp.zeros((n, dv), F32))
            self.cross_w.append(jnp.exp((tf + 1.0) * lg))
            self.qb.append(q.astype(BF16))
            self.kw.append(k * jnp.exp((seq - 1.0 - tf) * lg))
            self.vb.append(v.astype(BF16))

    def advance(self, b, s0_ref, s_ref):
        own = (self.r >= b * self.seq) & (self.r < (b + 1) * self.seq)
        for h in range(N_HEADS):
            s0 = s0_ref[h]
            self.cross[h] = self.cross[h] + jnp.where(own, _dot(self.qb[h], s0.astype(BF16)), 0.0)
            kwb = jnp.where(own, self.kw[h], 0.0).astype(BF16)
            s_ref[h] = s0 * math.exp(self.seq * _head_log_decay(h)) + _dot_tn(kwb, self.vb[h])

    def output(self, h):
        return self.inner[h] + self.cross[h] * self.cross_w[h]


def _ffn_decode_kernel(x_ref, g_ref, wg_ref, wu_ref, wd_ref, fin_ref, q_ref, k_ref, v_ref, s0_hbm,
                       o_ref, ro_ref, s_hbm, act_ref, sin_ref, sout_ref, sem_in, sem_out,
                       *, final_norm, seq, layer):
    i = pl.program_id(0)
    last = pl.num_programs(0) - 1
    n_sub = x_ref.shape[0] // ROW_TILE
    per_group = STEP_ROWS // seq
    per_step = n_sub * per_group
    n_chunks = wg_ref.shape[1] // FF_CHUNK
    dv = v_ref.shape[1] // N_HEADS
    run_after = {(u + 1) * n_chunks // per_group - 1: u for u in range(per_group)}

    def in_copy(n, slot):
        return pltpu.make_async_copy(s0_hbm.at[layer, n], sin_ref.at[slot], sem_in.at[slot])

    def out_copy(n, slot):
        return pltpu.make_async_copy(sout_ref.at[slot], s_hbm.at[n], sem_out.at[slot])

    lead = IN_SLOTS - 1

    @pl.when(i == 0)
    def _():
        for n in range(lead):
            in_copy(n, n % IN_SLOTS).start()

    def advance(group, sub, b):
        l = sub * per_group + b
        n = i * per_step + l
        slot, oslot = l % IN_SLOTS, l % OUT_SLOTS
        in_copy(n, slot).wait()
        if l + lead < per_step:
            in_copy(n + lead, (l + lead) % IN_SLOTS).start()
        else:
            pl.when(i < last)(lambda: in_copy(n + lead, (l + lead) % IN_SLOTS).start())
        if l >= OUT_SLOTS:
            out_copy(n - OUT_SLOTS, oslot).wait()
        else:
            pl.when(i > 0)(lambda: out_copy(n - OUT_SLOTS, oslot).wait())
        group.advance(b, sin_ref.at[slot], sout_ref.at[oslot])
        out_copy(n, oslot).start()

    for sub in range(n_sub):
        rows16 = slice(sub * STEP_ROWS, (sub + 1) * STEP_ROWS)
        group = _DecodeGroup(q_ref, k_ref, v_ref, rows16, seq)

        def after_chunk(j, group=group, sub=sub):
            if j in run_after:
                advance(group, sub, run_after[j])

        _ffn_rows(x_ref, g_ref, wg_ref, wu_ref, wd_ref, fin_ref, o_ref, act_ref,
                  slice(sub * ROW_TILE, (sub + 1) * ROW_TILE), final_norm=final_norm, after_chunk=after_chunk)
        for h in range(N_HEADS):
            ro_ref[rows16, h * dv:(h + 1) * dv] = group.output(h)

    @pl.when(i == last)
    def _():
        for l in range(per_step - OUT_SLOTS, per_step):
            out_copy(i * per_step + l, l % OUT_SLOTS).wait()


def _ffn_decode(x, g, wg, wu, wd, fin, q, k, v, s0, *, layer, final_norm, state_layer, seq):
    m, d = x.shape
    f = wg.shape[2]
    ms, qk = q.shape
    vd = v.shape[1]
    _, bsz, _, dk, dv = s0.shape
    tm = 2 * ROW_TILE
    steps = m // tm
    rows_s = (tm // ROW_TILE) * STEP_ROWS
    assert m % tm == 0 and f % FF_CHUNK == 0 and f // FF_CHUNK >= STEP_ROWS // seq
    assert seq <= SEQ_CHUNK and STEP_ROWS % seq == 0 and ms == bsz * seq and ms == steps * rows_s
    assert (rows_s // seq) % IN_SLOTS == 0 and (rows_s // seq) % OUT_SLOTS == 0
    row = pl.BlockSpec((tm, d), lambda i: (i, 0))
    srow = lambda n: pl.BlockSpec((rows_s, n), lambda i: (i, 0))
    hbm = pl.BlockSpec(memory_space=pl.ANY)
    state = (N_HEADS, dk, dv)
    vmem = _vmem_limit(
        [((tm, d), F32, 4), ((d, f), BF16, 2), ((f, d), BF16, 1), ((tm, f), BF16, 1),
         (state, F32, IN_SLOTS + OUT_SLOTS),
         ((rows_s, vd), F32, 8)],
        temporaries=8 * _nbytes((tm, d), F32))
    return pl.pallas_call(
        functools.partial(_ffn_decode_kernel, final_norm=final_norm, seq=seq, layer=state_layer),
        grid=(steps,),
        in_specs=[row, _const_spec((1, d)), _layer_spec(wg, layer), _layer_spec(wu, layer),
                  _layer_spec(wd, layer), _const_spec((1, d)), srow(qk), srow(qk), srow(vd), hbm],
        out_specs=[row, srow(vd), hbm],
        out_shape=[jax.ShapeDtypeStruct((m, d), F32), jax.ShapeDtypeStruct((ms, vd), F32),
                   jax.ShapeDtypeStruct(s0.shape[1:], F32)],
        scratch_shapes=[pltpu.VMEM((tm, f), BF16), pltpu.VMEM((IN_SLOTS,) + state, F32),
                        pltpu.VMEM((OUT_SLOTS,) + state, F32), pltpu.SemaphoreType.DMA((IN_SLOTS,)),
                        pltpu.SemaphoreType.DMA((OUT_SLOTS,))],
        compiler_params=pltpu.CompilerParams(dimension_semantics=("arbitrary",), vmem_limit_bytes=vmem),
        name="ffn_decode_ret",
    )(x, g, wg, wu, wd, fin, q, k, v, s0)


def _ret_out_kernel(x_ref, o_ref, sg_ref, gn_ref, wout_ref, y_ref, act_ref):
    dv = o_ref.shape[1] // N_HEADS
    for h in range(N_HEADS):
        cols = slice(h * dv, (h + 1) * dv)
        act_ref[:, cols] = _gn_gate(o_ref[:, cols], sg_ref[:, cols], gn_ref[:, cols])
    y_ref[...] = x_ref[...] + _dot(act_ref[...], wout_ref[...])


def _decode_ret_ffn_kernel(x_ref, o_ref, sg_ref, gn_ref, gffn_ref, fin_ref, rwout_hbm, wg_hbm, wu_hbm, wd_hbm,
                           y_ref, rwout_ref, wg_ref, wu_ref, wd_ref, x1_ref, ract_ref, act_ref, sem,
                           *, ret_layer, ffn_layer, final_norm):
    copies = _copies(
        [(rwout_hbm.at[ret_layer], rwout_ref), (wg_hbm.at[ffn_layer], wg_ref), (wu_hbm.at[ffn_layer], wu_ref),
         (wd_hbm.at[ffn_layer], wd_ref)], sem)
    copies[0].start()
    copies[0].wait()
    for c in copies[1:]:
        c.start()
    _ret_out_kernel(x_ref, o_ref, sg_ref, gn_ref, rwout_ref, x1_ref, ract_ref)
    for c in copies[1:]:
        c.wait()
    _ffn_rows(x1_ref, gffn_ref, wg_ref, wu_ref, wd_ref, fin_ref, y_ref, act_ref, slice(0, x_ref.shape[0]),
              final_norm=final_norm)


def _decode_ret_ffn(x, o, sg, gn, rwout, g_ffn, wg, wu, wd, fin, *, ret_layer, ffn_layer, final_norm):
    m, d = x.shape
    vd, f = o.shape[1], wg.shape[2]
    assert f % FF_CHUNK == 0
    full = lambda n: pl.BlockSpec((m, n), lambda i: (0, 0))
    hbm = pl.BlockSpec(memory_space=pl.ANY)
    weights = [(vd, d), (d, f), (d, f), (f, d)]
    vmem = _vmem_limit([((m, d), F32, 3), ((m, vd), F32, 2), ((m, vd), BF16, 1), ((m, f), BF16, 1)]
                       + [(w, BF16, 1) for w in weights], temporaries=8 * _nbytes((m, d), F32))
    return pl.pallas_call(
        functools.partial(_decode_ret_ffn_kernel, ret_layer=ret_layer, ffn_layer=ffn_layer,
                          final_norm=final_norm),
        grid=(1,),
        in_specs=[full(d), full(vd), full(vd), _const_spec((1, vd)), _const_spec((1, d)), _const_spec((1, d))]
        + [hbm] * len(weights),
        out_specs=full(d),
        out_shape=jax.ShapeDtypeStruct((m, d), F32),
        scratch_shapes=[pltpu.VMEM(w, BF16) for w in weights]
        + [pltpu.VMEM((m, d), F32), pltpu.VMEM((m, vd), BF16), pltpu.VMEM((m, f), BF16),
           pltpu.SemaphoreType.DMA((len(weights),))],
        compiler_params=pltpu.CompilerParams(dimension_semantics=("arbitrary",), vmem_limit_bytes=vmem),
        name="decode_ret_ffn",
    )(x, o, sg, gn, g_ffn, fin, rwout, wg, wu, wd)


def kernel(x_prompt, x_sample, state_conv, state_ret, norm_mix, norm_ffn, conv_w_in, conv_w, conv_w_out,
           ret_w_in, ret_gn, ret_w_out, ffn_w_gate, ffn_w_up, ffn_w_down, final_norm):
    bp, lp, d = x_prompt.shape
    bs, ls, _ = x_sample.shape
    depth = norm_mix.shape[0]
    dk = ret_w_in.shape[2] // 6 // N_HEADS
    inv = (ROPE_BASE ** (-jnp.arange(dk // 2, dtype=F32) / (dk // 2)))[None, :]
    fin = final_norm[None, :]
    conv_w_in, conv_w_out = conv_w_in.astype(BF16), conv_w_out.astype(BF16)
    assert depth % N_MIXERS == 0

    xp = x_prompt
    xs = x_sample.reshape(bs * ls, d)
    conv_p, conv_s, ret_p, ret_s = [], [], [], []
    for j in range(depth // N_MIXERS):
        ic, ir = N_MIXERS * j, N_MIXERS * j + 1
        last = ir == depth - 1
        g_ffn_c, g_ffn_r = norm_ffn[ic][None, :], norm_ffn[ir][None, :]
        gn = ret_gn[j][None, :]

        jobs = _CastJobs([(ffn_w_gate, ic), (ffn_w_up, ic), (ffn_w_down, ic), (ret_w_in, j), (ret_w_out, j)])
        xp, st, (wg, wu, wd, rwin, rwout) = _conv_mixer_seq(
            xp, norm_mix[ic][None, :], conv_w_in, conv_w, conv_w_out, layer=j, jobs=jobs)
        conv_p.append(st)
        xs, st, cos, sin = _decode_conv_ffn(
            xs, state_conv[j], norm_mix[ic][None, :], conv_w_in, conv_w, conv_w_out, g_ffn_c, wg, wu, wd, fin, inv,
            conv_layer=j, ffn_layer=0, seq=ls, final_norm=False, rope_len=lp)
        conv_s.append(st)
        q, k, v, sg = _ret_proj(xs, norm_mix[ir][None, :], rwin, inv, layer=0, seq=ls, pos0=PAST_LEN)
        xp, ret_o, st = _ffn_decode(xp.reshape(bp * lp, d), g_ffn_c, wg, wu, wd, fin, q, k, v, state_ret,
                                    layer=0, final_norm=False, state_layer=j, seq=ls)
        ret_s.append(st)

        jobs = _CastJobs([(ffn_w_gate, ir), (ffn_w_up, ir), (ffn_w_down, ir)])
        xp, st, (wg, wu, wd) = _ret_layer_seq(xp.reshape(bp, lp, d), norm_mix[ir][None, :], rwin, cos, sin, gn,
                                              rwout, layer=0, jobs=jobs)
        ret_p.append(st)
        xs = _decode_ret_ffn(xs, ret_o, sg, gn, rwout, g_ffn_r, wg, wu, wd, fin, ret_layer=0, ffn_layer=0,
                             final_norm=last)
        xp = _ffn(xp.reshape(bp * lp, d), g_ffn_r, wg, wu, wd, fin, layer=0, final_norm=last).reshape(bp, lp, d)
    return (xp, xs.reshape(bs, ls, d), jnp.stack(conv_p), jnp.stack(conv_s), jnp.stack(ret_p),
            jnp.stack(ret_s))
```

```python
import functools
import math

import jax
import jax.numpy as jnp
from jax import lax
from jax.experimental import pallas as pl
from jax.experimental.pallas import tpu as pltpu

F32 = jnp.float32
BF16 = jnp.bfloat16

N_MIXERS = 2
CONV_WIDTH = 3
N_HEADS = 4
PAST_LEN = 16384
RMS_EPS = 1e-6
GN_EPS = 1e-6
ROPE_BASE = 10000.0

V7X_VMEM_LIMIT_CAP = 60000 * 1024
V7X_MXU_DIM = 256
SUBLANES = 8
ROW_TILE = 512
FF_CHUNK = V7X_MXU_DIM
CONV_CHUNK = V7X_MXU_DIM
SEQ_CHUNK = V7X_MXU_DIM
BF16_ROWS = 16
STEP_ROWS = BF16_ROWS
CAST_CHUNKS = 16
IN_SLOTS, OUT_SLOTS = 4, 2


def _nbytes(shape, dtype):
    return math.prod(shape) * jnp.dtype(dtype).itemsize


def _vmem_limit(buffers, temporaries):
    need = sum(_nbytes(s, d) * n for s, d, n in buffers) + temporaries
    return min(int(need * 1.25) + (4 << 20), V7X_VMEM_LIMIT_CAP)


def _const_spec(shape):
    return pl.BlockSpec(shape, lambda *_: (0,) * len(shape), pipeline_mode=pl.Buffered(1))


def _layer_spec(w, layer):
    return pl.BlockSpec((None,) + w.shape[1:], lambda *_: (layer, 0, 0), pipeline_mode=pl.Buffered(1))


class _CastJobs:
    def __init__(self, weights):
        self.weights = list(weights)
        self.n = len(self.weights)

    @staticmethod
    def chunk(grid_idx, grid):
        step = grid_idx[0]
        for extent, idx in zip(grid[1:], grid_idx[1:]):
            step = step * extent + idx
        return jnp.minimum(step, CAST_CHUNKS - 1)

    def specs(self, grid):
        assert math.prod(grid) >= CAST_CHUNKS
        ins, outs, shapes = [], [], []
        for w, layer in self.weights:
            rows, cols = w.shape[1:]
            assert rows % (CAST_CHUNKS * BF16_ROWS) == 0
            blk = (None, rows // CAST_CHUNKS, cols)
            ins.append(pl.BlockSpec(blk, lambda *g, layer=layer: (layer, self.chunk(g, grid), 0)))
            outs.append(pl.BlockSpec(blk, lambda *g: (0, self.chunk(g, grid), 0)))
            shapes.append(jax.ShapeDtypeStruct((1, rows, cols), BF16))
        return ins, outs, shapes

    def operands(self):
        return [w for w, _ in self.weights]

    def vmem(self):
        return [((w.shape[1] // CAST_CHUNKS, w.shape[2]), F32, 3) for w, _ in self.weights]

    def run(self, src_refs, dst_refs, step):
        @pl.when(step < CAST_CHUNKS)
        def _():
            for src, dst in zip(src_refs, dst_refs):
                dst[...] = src[...].astype(BF16)


def _rmsnorm(x, g):
    ms = jnp.mean(x * x, axis=-1, keepdims=True)
    return (x * lax.rsqrt(ms + RMS_EPS)) * g


def _dot(a, b):
    return jnp.dot(a, b, preferred_element_type=F32)


def _dot_nt(a, b):
    return lax.dot_general(a, b, (((1,), (1,)), ((), ())), preferred_element_type=F32)


def _dot_tn(a, b):
    return lax.dot_general(a, b, (((0,), (0,)), ((), ())), preferred_element_type=F32)


def _head_log_decay(h):
    return math.log(1.0 - 2.0 ** (-5.0 - h))


def _ffn_rows(x_ref, g_ref, wg_ref, wu_ref, wd_ref, fin_ref, o_ref, act_ref, rows, *, final_norm,
              after_chunk=None):
    x = x_ref[rows, :]
    n = _rmsnorm(x, g_ref[...]).astype(BF16)
    for j in range(wg_ref.shape[1] // FF_CHUNK):
        cols = slice(j * FF_CHUNK, (j + 1) * FF_CHUNK)
        gate = _dot(n, wg_ref[:, cols])
        up = _dot(n, wu_ref[:, cols])
        act_ref[rows, cols] = (jax.nn.silu(gate) * up).astype(BF16)
        if after_chunk is not None:
            after_chunk(j)
    y = x + _dot(act_ref[rows, :], wd_ref[...])
    if final_norm:
        y = _rmsnorm(y, fin_ref[...])
    o_ref[rows, :] = y


def _ffn_kernel(x_ref, g_ref, wg_ref, wu_ref, wd_ref, fin_ref, o_ref, act_ref, *, final_norm):
    for r0 in range(0, x_ref.shape[0], ROW_TILE):
        _ffn_rows(x_ref, g_ref, wg_ref, wu_ref, wd_ref, fin_ref, o_ref, act_ref, slice(r0, r0 + ROW_TILE),
                  final_norm=final_norm)


def _ffn(x, g, wg, wu, wd, fin, *, layer, final_norm):
    m, d = x.shape
    f = wg.shape[2]
    tm = min(2 * ROW_TILE, m)
    assert m % tm == 0 and tm % ROW_TILE == 0 and f % FF_CHUNK == 0
    row = pl.BlockSpec((tm, d), lambda i: (i, 0))
    vmem = _vmem_limit(
        [((tm, d), F32, 4), ((d, f), BF16, 2), ((f, d), BF16, 1), ((tm, f), BF16, 1)],
        temporaries=6 * _nbytes((tm, d), F32))
    return pl.pallas_call(
        functools.partial(_ffn_kernel, final_norm=final_norm),
        grid=(m // tm,),
        in_specs=[row, _const_spec((1, d)), _layer_spec(wg, layer), _layer_spec(wu, layer),
                  _layer_spec(wd, layer), _const_spec((1, d))],
        out_specs=row,
        out_shape=jax.ShapeDtypeStruct((m, d), F32),
        scratch_shapes=[pltpu.VMEM((tm, f), BF16)],
        compiler_params=pltpu.CompilerParams(dimension_semantics=("arbitrary",), vmem_limit_bytes=vmem),
        name="ffn_final" if final_norm else "ffn",
    )(x, g, wg, wu, wd, fin)


def _conv_front(x, g_ref, win_ref):
    d = x.shape[1]
    hn = _rmsnorm(x, g_ref[...]).astype(BF16)
    b = _dot(hn, win_ref[:, 0:d])
    c = _dot(hn, win_ref[:, d:2 * d])
    h = _dot(hn, win_ref[:, 2 * d:3 * d])
    return b, c * h


def _conv_seq_kernel(x_ref, g_ref, win_ref, wc_ref, wout_ref, *refs, jobs):
    job_src, (o_ref, st_ref), job_dst, (ubuf_ref, act_ref) = (
        refs[:jobs.n], refs[jobs.n:jobs.n + 2], refs[jobs.n + 2:-2], refs[-2:])
    tm, d = x_ref.shape[1:]

    @pl.when(pl.program_id(1) == 0)
    def _():
        ubuf_ref[0:SUBLANES, :] = jnp.zeros((SUBLANES, d), F32)

    jobs.run(job_src, job_dst, pl.program_id(0) * pl.num_programs(1) + pl.program_id(1))
    for r0 in range(0, tm, ROW_TILE):
        rows = slice(r0, r0 + ROW_TILE)
        x = x_ref[0, rows, :]
        hn = _rmsnorm(x, g_ref[...]).astype(BF16)
        for c0 in range(0, d, CONV_CHUNK):
            cols = slice(c0, c0 + CONV_CHUNK)
            b = _dot(hn, win_ref[:, c0:c0 + CONV_CHUNK])
            u = (_dot(hn, win_ref[:, d + c0:d + c0 + CONV_CHUNK])
                 * _dot(hn, win_ref[:, 2 * d + c0:2 * d + c0 + CONV_CHUNK]))
            ubuf_ref[SUBLANES + r0:SUBLANES + r0 + ROW_TILE, cols] = u
            y = (wc_ref[0:1, cols] * ubuf_ref[SUBLANES - 2 + r0:SUBLANES - 2 + r0 + ROW_TILE, cols]
                 + wc_ref[1:2, cols] * ubuf_ref[SUBLANES - 1 + r0:SUBLANES - 1 + r0 + ROW_TILE, cols]
                 + wc_ref[2:3, cols] * u)
            act_ref[rows, cols] = (b * y).astype(BF16)
        o_ref[0, rows, :] = x + _dot(act_ref[rows, :], wout_ref[...])
    st_ref[0] = ubuf_ref[SUBLANES + tm - (CONV_WIDTH - 1):SUBLANES + tm, :]
    ubuf_ref[0:SUBLANES, :] = ubuf_ref[tm:tm + SUBLANES, :]


def _conv_mixer_seq(x, g, win, wc, wout, *, layer, jobs):
    bsz, seq, d = x.shape
    tm = 2 * ROW_TILE
    assert seq % tm == 0 and wc.shape[1] == CONV_WIDTH == 3
    grid = (bsz, seq // tm)
    job_in, job_out, job_shapes = jobs.specs(grid)
    row = pl.BlockSpec((1, tm, d), lambda b, t: (b, t, 0))
    vmem = _vmem_limit(
        [((tm, d), F32, 4), ((d, 3 * d), BF16, 1), ((d, d), BF16, 1), ((tm + SUBLANES, d), F32, 1)]
        + jobs.vmem(), temporaries=5 * _nbytes((tm, d), F32))
    y, st, *job_res = pl.pallas_call(
        functools.partial(_conv_seq_kernel, jobs=jobs),
        grid=grid,
        in_specs=[row, _const_spec((1, d)), _layer_spec(win, layer), _layer_spec(wc, layer),
                  _layer_spec(wout, layer)] + job_in,
        out_specs=[row, pl.BlockSpec((1, CONV_WIDTH - 1, d), lambda b, t: (b, 0, 0))] + job_out,
        out_shape=[jax.ShapeDtypeStruct((bsz, seq, d), F32),
                   jax.ShapeDtypeStruct((bsz, CONV_WIDTH - 1, d), F32)] + job_shapes,
        scratch_shapes=[pltpu.VMEM((tm + SUBLANES, d), F32), pltpu.VMEM((tm, d), BF16)],
        compiler_params=pltpu.CompilerParams(dimension_semantics=("arbitrary", "arbitrary"),
                                             vmem_limit_bytes=vmem),
        name="conv_mixer_seq",
    )(x, g, win, wc, wout, *jobs.operands())
    return y, st, job_res


def _conv_step_kernel(x_ref, pre_ref, g_ref, win_ref, wc_ref, wout_ref, o_ref, u_ref, *, seq):
    x = x_ref[...]
    m = x.shape[0]
    b, u = _conv_front(x, g_ref, win_ref)
    pre = pre_ref[...]
    t = lax.rem(lax.broadcasted_iota(jnp.int32, (m, 1), 0), seq)
    back1 = jnp.where(t >= 1, pltpu.roll(u, 1, axis=0), pltpu.roll(pre, m - 1, axis=0))
    back2 = jnp.where(t >= 2, pltpu.roll(u, 2, axis=0), pre)
    y = wc_ref[0:1, :] * back2 + wc_ref[1:2, :] * back1 + wc_ref[2:3, :] * u
    o_ref[...] = x + _dot((b * y).astype(BF16), wout_ref[...])
    u_ref[...] = u


def _copies(pairs, sem):
    return [pltpu.make_async_copy(src, dst, sem.at[n]) for n, (src, dst) in enumerate(pairs)]


def _decode_conv_ffn_kernel(x_ref, pre_ref, gmix_ref, wc_ref, gffn_ref, fin_ref, inv_ref,
                            win_hbm, wout_hbm, wg_hbm, wu_hbm, wd_hbm, u_ref, y_ref, cos_ref, sin_ref,
                            win_ref, wout_ref, wg_ref, wu_ref, wd_ref, x1_ref, act_ref, sem,
                            *, seq, conv_layer, ffn_layer, final_norm):
    mixer_w, ffn_w = 2, 3
    copies = _copies(
        [(win_hbm.at[conv_layer], win_ref), (wout_hbm.at[conv_layer], wout_ref), (wg_hbm.at[ffn_layer], wg_ref),
         (wu_hbm.at[ffn_layer], wu_ref), (wd_hbm.at[ffn_layer], wd_ref)], sem)
    for c in copies[:mixer_w]:
        c.start()
    pos = lax.broadcasted_iota(jnp.int32, cos_ref.shape, 0)
    cos_ref[...], sin_ref[...] = _rope_angles(inv_ref, pos)
    for c in copies[:mixer_w]:
        c.wait()
    for c in copies[mixer_w:mixer_w + ffn_w]:
        c.start()
    _conv_step_kernel(x_ref, pre_ref, gmix_ref, win_ref, wc_ref, wout_ref, x1_ref, u_ref, seq=seq)
    for c in copies[mixer_w:mixer_w + ffn_w]:
        c.wait()
    _ffn_rows(x1_ref, gffn_ref, wg_ref, wu_ref, wd_ref, fin_ref, y_ref, act_ref, slice(0, x_ref.shape[0]),
              final_norm=final_norm)


def _decode_conv_ffn(x, buf, g_mix, win, wc, wout, g_ffn, wg, wu, wd, fin, inv, *, conv_layer, ffn_layer, seq,
                     final_norm, rope_len):
    m, d = x.shape
    f = wg.shape[2]
    taps = CONV_WIDTH - 1
    assert wc.shape[1] == CONV_WIDTH == 3 and seq >= taps and m % seq == 0 and f % FF_CHUNK == 0
    pre = jnp.pad(buf, ((0, 0), (0, seq - taps), (0, 0))).reshape(m, d)
    full = pl.BlockSpec((m, d), lambda i: (0, 0))
    table = pl.BlockSpec((rope_len, inv.shape[1]), lambda i: (0, 0))
    hbm = pl.BlockSpec(memory_space=pl.ANY)
    weights = [(d, 3 * d), (d, d), (d, f), (d, f), (f, d)]
    vmem = _vmem_limit([((m, d), F32, 5), ((m, f), BF16, 1), ((rope_len, inv.shape[1]), F32, 2)]
                       + [(w, BF16, 1) for w in weights], temporaries=12 * _nbytes((m, d), F32))
    u, y, cos, sin = pl.pallas_call(
        functools.partial(_decode_conv_ffn_kernel, seq=seq, conv_layer=conv_layer, ffn_layer=ffn_layer,
                          final_norm=final_norm),
        grid=(1,),
        in_specs=[full, full, _const_spec((1, d)), _layer_spec(wc, conv_layer), _const_spec((1, d)),
                  _const_spec((1, d)), _const_spec(inv.shape)] + [hbm] * len(weights),
        out_specs=[full, full, table, table],
        out_shape=[jax.ShapeDtypeStruct((m, d), F32)] * 2
        + [jax.ShapeDtypeStruct((rope_len, inv.shape[1]), F32)] * 2,
        scratch_shapes=[pltpu.VMEM(w, BF16) for w in weights]
        + [pltpu.VMEM((m, d), F32), pltpu.VMEM((m, f), BF16), pltpu.SemaphoreType.DMA((len(weights),))],
        compiler_params=pltpu.CompilerParams(dimension_semantics=("arbitrary",), vmem_limit_bytes=vmem),
        name="decode_conv_ffn",
    )(x, pre, g_mix, wc, g_ffn, fin, inv, win, wout, wg, wu, wd)
    return y, u.reshape(m // seq, seq, d)[:, seq - taps:, :], cos, sin


def _rope_angles(inv_ref, pos):
    ang = pos.astype(F32) * inv_ref[...]
    return jnp.cos(ang), jnp.sin(ang)


def _rotate(z, cos, sin):
    half = z.shape[1] // 2
    x1, x2 = z[:, :half], z[:, half:]
    return jnp.concatenate([x1 * cos - x2 * sin, x1 * sin + x2 * cos], axis=1)


def _proj_head(hn, win_ref, h, cos, sin):
    ret_in = win_ref.shape[1]
    qk, vd = ret_in // 6, ret_in // 3
    dk, dv = qk // N_HEADS, vd // N_HEADS
    q = _rotate(_dot(hn, win_ref[:, h * dk:(h + 1) * dk]), cos, sin)
    k = _rotate(_dot(hn, win_ref[:, qk + h * dk:qk + (h + 1) * dk]), cos, sin) * dk ** -0.5
    v = _dot(hn, win_ref[:, 2 * qk + h * dv:2 * qk + (h + 1) * dv])
    sg = jax.nn.silu(_dot(hn, win_ref[:, 2 * qk + vd + h * dv:2 * qk + vd + (h + 1) * dv]))
    return q, k, v, sg


def _gn_gate(o, sg, gn):
    mu = jnp.mean(o, axis=-1, keepdims=True)
    var = jnp.mean(jnp.square(o - mu), axis=-1, keepdims=True)
    of = ((o - mu) * lax.rsqrt(var + GN_EPS)) * gn
    return (sg * of).astype(BF16)


def _ret_layer_seq_kernel(x_ref, g_ref, win_ref, cos_ref, sin_ref, gn_ref, wout_ref, *refs, jobs):
    job_src, (y_ref, s_ref), job_dst, act_ref = (
        refs[:jobs.n], refs[jobs.n:jobs.n + 2], refs[jobs.n + 2:-1], refs[-1])
    tm = x_ref.shape[1]
    dv = gn_ref.shape[1] // N_HEADS
    c_len = SEQ_CHUNK

    @pl.when(pl.program_id(1) == 0)
    def _():
        s_ref[...] = jnp.zeros(s_ref.shape, F32)

    jobs.run(job_src, job_dst, pl.program_id(0) * pl.num_programs(1) + pl.program_id(1))
    i = lax.broadcasted_iota(jnp.int32, (c_len, c_len), 0)
    j = lax.broadcasted_iota(jnp.int32, (c_len, c_len), 1)
    diff = (i - j).astype(F32)
    idx = lax.broadcasted_iota(jnp.int32, (c_len, 1), 0).astype(F32)
    for c in range(tm // c_len):
        rows = slice(c * c_len, (c + 1) * c_len)
        x = x_ref[0, rows, :]
        hn = _rmsnorm(x, g_ref[...]).astype(BF16)
        cos, sin = cos_ref[rows, :], sin_ref[rows, :]
        for h in range(N_HEADS):
            lg = _head_log_decay(h)
            decay = jnp.where(diff >= 0, jnp.exp(jnp.maximum(diff, 0.0) * lg), 0.0)
            cross_w = jnp.exp((idx + 1.0) * lg)
            state_w = jnp.exp((c_len - 1.0 - idx) * lg)
            chunk_decay = math.exp(c_len * lg)
            q, k, v, sg = _proj_head(hn, win_ref, h, cos, sin)
            qc, vc = q.astype(BF16), v.astype(BF16)
            cols = slice(h * dv, (h + 1) * dv)
            s = s_ref[0, h]
            scores = _dot_nt(qc, k.astype(BF16)) * decay
            o = _dot(scores.astype(BF16), vc) + _dot(qc, s.astype(BF16)) * cross_w
            s_ref[0, h] = s * chunk_decay + _dot_tn((k * state_w).astype(BF16), vc)
            act_ref[rows, cols] = _gn_gate(o, sg, gn_ref[:, cols])
        y_ref[0, rows, :] = x + _dot(act_ref[rows, :], wout_ref[...])


def _ret_layer_seq(x, g, win, cos, sin, gn, wout, *, layer, jobs):
    bsz, seq, d = x.shape
    ret_in, vd = win.shape[2], wout.shape[1]
    dk, dv = ret_in // 6 // N_HEADS, vd // N_HEADS
    tm = ROW_TILE
    assert seq % tm == 0 and tm % SEQ_CHUNK == 0 and cos.shape == (seq, dk // 2)
    grid = (bsz, seq // tm)
    job_in, job_out, job_shapes = jobs.specs(grid)
    row = pl.BlockSpec((1, tm, d), lambda b, t: (b, t, 0))
    rope = pl.BlockSpec((tm, dk // 2), lambda b, t: (t, 0))
    vmem = _vmem_limit(
        [((tm, d), F32, 4), ((d, ret_in), BF16, 1), ((vd, d), BF16, 1), ((tm, dk), F32, 4),
         ((N_HEADS, dk, dv), F32, 2), ((tm, vd), BF16, 1)] + jobs.vmem(),
        temporaries=8 * _nbytes((tm, d), F32))
    y, st, *job_res = pl.pallas_call(
        functools.partial(_ret_layer_seq_kernel, jobs=jobs),
        grid=grid,
        in_specs=[row, _const_spec((1, d)), _layer_spec(win, layer), rope, rope, _const_spec((1, vd)),
                  _layer_spec(wout, layer)] + job_in,
        out_specs=[row, pl.BlockSpec((1, N_HEADS, dk, dv), lambda b, t: (b, 0, 0, 0))] + job_out,
        out_shape=[jax.ShapeDtypeStruct((bsz, seq, d), F32),
                   jax.ShapeDtypeStruct((bsz, N_HEADS, dk, dv), F32)] + job_shapes,
        scratch_shapes=[pltpu.VMEM((tm, vd), BF16)],
        compiler_params=pltpu.CompilerParams(dimension_semantics=("arbitrary", "arbitrary"),
                                             vmem_limit_bytes=vmem),
        name="ret_layer_seq",
    )(x, g, win, cos, sin, gn, wout, *jobs.operands())
    return y, st, job_res


def _ret_proj_kernel(x_ref, g_ref, win_ref, inv_ref, q_ref, k_ref, v_ref, sg_ref, *, seq, pos0):
    tm = x_ref.shape[0]
    dk, dv = q_ref.shape[1] // N_HEADS, v_ref.shape[1] // N_HEADS
    hn = _rmsnorm(x_ref[...], g_ref[...]).astype(BF16)
    row = lax.broadcasted_iota(jnp.int32, (tm, dk // 2), 0)
    cos, sin = _rope_angles(inv_ref, pos0 + lax.rem(row, seq))
    for h in range(N_HEADS):
        q, k, v, sg = _proj_head(hn, win_ref, h, cos, sin)
        q_ref[:, h * dk:(h + 1) * dk] = q
        k_ref[:, h * dk:(h + 1) * dk] = k
        v_ref[:, h * dv:(h + 1) * dv] = v
        sg_ref[:, h * dv:(h + 1) * dv] = sg


def _ret_proj(x, g, win, inv, *, layer, seq, pos0):
    m, d = x.shape
    ret_in = win.shape[2]
    qk, vd = ret_in // 6, ret_in // 3
    tm = ROW_TILE
    assert m % tm == 0 and tm % seq == 0
    row = lambda n: pl.BlockSpec((tm, n), lambda i: (i, 0))
    vmem = _vmem_limit(
        [((tm, d), F32, 2), ((d, ret_in), BF16, 1), ((tm, qk), F32, 4), ((tm, vd), F32, 4)],
        temporaries=8 * _nbytes((tm, d), F32))
    return pl.pallas_call(
        functools.partial(_ret_proj_kernel, seq=seq, pos0=pos0),
        grid=(m // tm,),
        in_specs=[row(d), _const_spec((1, d)), _layer_spec(win, layer), _const_spec(inv.shape)],
        out_specs=[row(qk), row(qk), row(vd), row(vd)],
        out_shape=[jax.ShapeDtypeStruct((m, qk), F32), jax.ShapeDtypeStruct((m, qk), F32),
                   jax.ShapeDtypeStruct((m, vd), F32), jax.ShapeDtypeStruct((m, vd), F32)],
        compiler_params=pltpu.CompilerParams(dimension_semantics=("arbitrary",), vmem_limit_bytes=vmem),
        name="ret_proj",
    )(x, g, win, inv)


class _DecodeGroup:
    def __init__(self, q_ref, k_ref, v_ref, rows, seq):
        n = rows.stop - rows.start
        dk, dv = q_ref.shape[1] // N_HEADS, v_ref.shape[1] // N_HEADS
        self.seq = seq
        self.r = lax.broadcasted_iota(jnp.int32, (n, 1), 0)
        t = lax.rem(self.r, seq)
        tf = t.astype(F32)
        self.qb, self.kw, self.vb, self.inner, self.cross, self.cross_w = [], [], [], [], [], []
        for h in range(N_HEADS):
            lg = _head_log_decay(h)
            q = q_ref[rows, h * dk:(h + 1) * dk]
            k = k_ref[rows, h * dk:(h + 1) * dk]
            v = v_ref[rows, h * dv:(h + 1) * dv]
            acc = jnp.zeros((n, dv), F32)
            for dlt in range(seq):
                kd = k if dlt == 0 else pltpu.roll(k, dlt, axis=0)
                vd = v if dlt == 0 else pltpu.roll(v, dlt, axis=0)
                sc = jnp.sum(q * kd, axis=-1, keepdims=True) * math.exp(dlt * lg)
                acc = acc + jnp.where(t >= dlt, sc, 0.0) * vd
            self.inner.append(acc)
            self.cross.append(jnp.zeros((n, dv), F32))
            self.cross_w.append(jnp.exp((tf + 1.0) * lg))
            self.qb.append(q.astype(BF16))
            self.kw.append(k * jnp.exp((seq - 1.0 - tf) * lg))
            self.vb.append(v.astype(BF16))

    def advance(self, b, s0_ref, s_ref):
        own = (self.r >= b * self.seq) & (self.r < (b + 1) * self.seq)
        for h in range(N_HEADS):
            s0 = s0_ref[h]
            self.cross[h] = self.cross[h] + jnp.where(own, _dot(self.qb[h], s0.astype(BF16)), 0.0)
            kwb = jnp.where(own, self.kw[h], 0.0).astype(BF16)
            s_ref[h] = s0 * math.exp(self.seq * _head_log_decay(h)) + _dot_tn(kwb, self.vb[h])

    def output(self, h):
        return self.inner[h] + self.cross[h] * self.cross_w[h]


def _ffn_decode_kernel(x_ref, g_ref, wg_ref, wu_ref, wd_ref, fin_ref, q_ref, k_ref, v_ref, s0_hbm,
                       o_ref, ro_ref, s_hbm, act_ref, sin_ref, sout_ref, sem_in, sem_out,
                       *, final_norm, seq, layer):
    i = pl.program_id(0)
    last = pl.num_programs(0) - 1
    n_sub = x_ref.shape[0] // ROW_TILE
    per_group = STEP_ROWS // seq
    per_step = n_sub * per_group
    n_chunks = wg_ref.shape[1] // FF_CHUNK
    dv = v_ref.shape[1] // N_HEADS
    run_after = {(u + 1) * n_chunks // per_group - 1: u for u in range(per_group)}

    def in_copy(n, slot):
        return pltpu.make_async_copy(s0_hbm.at[layer, n], sin_ref.at[slot], sem_in.at[slot])

    def out_copy(n, slot):
        return pltpu.make_async_copy(sout_ref.at[slot], s_hbm.at[n], sem_out.at[slot])

    lead = IN_SLOTS - 1

    @pl.when(i == 0)
    def _():
        for n in range(lead):
            in_copy(n, n % IN_SLOTS).start()

    def advance(group, sub, b):
        l = sub * per_group + b
        n = i * per_step + l
        slot, oslot = l % IN_SLOTS, l % OUT_SLOTS
        in_copy(n, slot).wait()
        if l + lead < per_step:
            in_copy(n + lead, (l + lead) % IN_SLOTS).start()
        else:
            pl.when(i < last)(lambda: in_copy(n + lead, (l + lead) % IN_SLOTS).start())
        if l >= OUT_SLOTS:
            out_copy(n - OUT_SLOTS, oslot).wait()
        else:
            pl.when(i > 0)(lambda: out_copy(n - OUT_SLOTS, oslot).wait())
        group.advance(b, sin_ref.at[slot], sout_ref.at[oslot])
        out_copy(n, oslot).start()

    for sub in range(n_sub):
        rows16 = slice(sub * STEP_ROWS, (sub + 1) * STEP_ROWS)
        group = _DecodeGroup(q_ref, k_ref, v_ref, rows16, seq)

        def after_chunk(j, group=group, sub=sub):
            if j in run_after:
                advance(group, sub, run_after[j])

        _ffn_rows(x_ref, g_ref, wg_ref, wu_ref, wd_ref, fin_ref, o_ref, act_ref,
                  slice(sub * ROW_TILE, (sub + 1) * ROW_TILE), final_norm=final_norm, after_chunk=after_chunk)
        for h in range(N_HEADS):
            ro_ref[rows16, h * dv:(h + 1) * dv] = group.output(h)

    @pl.when(i == last)
    def _():
        for l in range(per_step - OUT_SLOTS, per_step):
            out_copy(i * per_step + l, l % OUT_SLOTS).wait()


def _ffn_decode(x, g, wg, wu, wd, fin, q, k, v, s0, *, layer, final_norm, state_layer, seq):
    m, d = x.shape
    f = wg.shape[2]
    ms, qk = q.shape
    vd = v.shape[1]
    _, bsz, _, dk, dv = s0.shape
    tm = 2 * ROW_TILE
    steps = m // tm
    rows_s = (tm // ROW_TILE) * STEP_ROWS
    assert m % tm == 0 and f % FF_CHUNK == 0 and f // FF_CHUNK >= STEP_ROWS // seq
    assert seq <= SEQ_CHUNK and STEP_ROWS % seq == 0 and ms == bsz * seq and ms == steps * rows_s
    assert (rows_s // seq) % IN_SLOTS == 0 and (rows_s // seq) % OUT_SLOTS == 0
    row = pl.BlockSpec((tm, d), lambda i: (i, 0))
    srow = lambda n: pl.BlockSpec((rows_s, n), lambda i: (i, 0))
    hbm = pl.BlockSpec(memory_space=pl.ANY)
    state = (N_HEADS, dk, dv)
    vmem = _vmem_limit(
        [((tm, d), F32, 4), ((d, f), BF16, 2), ((f, d), BF16, 1), ((tm, f), BF16, 1),
         (state, F32, IN_SLOTS + OUT_SLOTS),
         ((rows_s, vd), F32, 8)],
        temporaries=8 * _nbytes((tm, d), F32))
    return pl.pallas_call(
        functools.partial(_ffn_decode_kernel, final_norm=final_norm, seq=seq, layer=state_layer),
        grid=(steps,),
        in_specs=[row, _const_spec((1, d)), _layer_spec(wg, layer), _layer_spec(wu, layer),
                  _layer_spec(wd, layer), _const_spec((1, d)), srow(qk), srow(qk), srow(vd), hbm],
        out_specs=[row, srow(vd), hbm],
        out_shape=[jax.ShapeDtypeStruct((m, d), F32), jax.ShapeDtypeStruct((ms, vd), F32),
                   jax.ShapeDtypeStruct(s0.shape[1:], F32)],
        scratch_shapes=[pltpu.VMEM((tm, f), BF16), pltpu.VMEM((IN_SLOTS,) + state, F32),
                        pltpu.VMEM((OUT_SLOTS,) + state, F32), pltpu.SemaphoreType.DMA((IN_SLOTS,)),
                        pltpu.SemaphoreType.DMA((OUT_SLOTS,))],
        compiler_params=pltpu.CompilerParams(dimension_semantics=("arbitrary",), vmem_limit_bytes=vmem),
        name="ffn_decode_ret",
    )(x, g, wg, wu, wd, fin, q, k, v, s0)


def _ret_out_kernel(x_ref, o_ref, sg_ref, gn_ref, wout_ref, y_ref, act_ref):
    dv = o_ref.shape[1] // N_HEADS
    for h in range(N_HEADS):
        cols = slice(h * dv, (h + 1) * dv)
        act_ref[:, cols] = _gn_gate(o_ref[:, cols], sg_ref[:, cols], gn_ref[:, cols])
    y_ref[...] = x_ref[...] + _dot(act_ref[...], wout_ref[...])


def _decode_ret_ffn_kernel(x_ref, o_ref, sg_ref, gn_ref, gffn_ref, fin_ref, rwout_hbm, wg_hbm, wu_hbm, wd_hbm,
                           y_ref, rwout_ref, wg_ref, wu_ref, wd_ref, x1_ref, ract_ref, act_ref, sem,
                           *, ret_layer, ffn_layer, final_norm):
    copies = _copies(
        [(rwout_hbm.at[ret_layer], rwout_ref), (wg_hbm.at[ffn_layer], wg_ref), (wu_hbm.at[ffn_layer], wu_ref),
         (wd_hbm.at[ffn_layer], wd_ref)], sem)
    copies[0].start()
    copies[0].wait()
    for c in copies[1:]:
        c.start()
    _ret_out_kernel(x_ref, o_ref, sg_ref, gn_ref, rwout_ref, x1_ref, ract_ref)
    for c in copies[1:]:
        c.wait()
    _ffn_rows(x1_ref, gffn_ref, wg_ref, wu_ref, wd_ref, fin_ref, y_ref, act_ref, slice(0, x_ref.shape[0]),
              final_norm=final_norm)


def _decode_ret_ffn(x, o, sg, gn, rwout, g_ffn, wg, wu, wd, fin, *, ret_layer, ffn_layer, final_norm):
    m, d = x.shape
    vd, f = o.shape[1], wg.shape[2]
    assert f % FF_CHUNK == 0
    full = lambda n: pl.BlockSpec((m, n), lambda i: (0, 0))
    hbm = pl.BlockSpec(memory_space=pl.ANY)
    weights = [(vd, d), (d, f), (d, f), (f, d)]
    vmem = _vmem_limit([((m, d), F32, 3), ((m, vd), F32, 2), ((m, vd), BF16, 1), ((m, f), BF16, 1)]
                       + [(w, BF16, 1) for w in weights], temporaries=8 * _nbytes((m, d), F32))
    return pl.pallas_call(
        functools.partial(_decode_ret_ffn_kernel, ret_layer=ret_layer, ffn_layer=ffn_layer,
                          final_norm=final_norm),
        grid=(1,),
        in_specs=[full(d), full(vd), full(vd), _const_spec((1, vd)), _const_spec((1, d)), _const_spec((1, d))]
        + [hbm] * len(weights),
        out_specs=full(d),
        out_shape=jax.ShapeDtypeStruct((m, d), F32),
        scratch_shapes=[pltpu.VMEM(w, BF16) for w in weights]
        + [pltpu.VMEM((m, d), F32), pltpu.VMEM((m, vd), BF16), pltpu.VMEM((m, f), BF16),
           pltpu.SemaphoreType.DMA((len(weights),))],
        compiler_params=pltpu.CompilerParams(dimension_semantics=("arbitrary",), vmem_limit_bytes=vmem),
        name="decode_ret_ffn",
    )(x, o, sg, gn, g_ffn, fin, rwout, wg, wu, wd)


def kernel(x_prompt, x_sample, state_conv, state_ret, norm_mix, norm_ffn, conv_w_in, conv_w, conv_w_out,
           ret_w_in, ret_gn, ret_w_out, ffn_w_gate, ffn_w_up, ffn_w_down, final_norm):
    bp, lp, d = x_prompt.shape
    bs, ls, _ = x_sample.shape
    depth = norm_mix.shape[0]
    dk = ret_w_in.shape[2] // 6 // N_HEADS
    inv = (ROPE_BASE ** (-jnp.arange(dk // 2, dtype=F32) / (dk // 2)))[None, :]
    fin = final_norm[None, :]
    conv_w_in, conv_w_out = conv_w_in.astype(BF16), conv_w_out.astype(BF16)
    assert depth % N_MIXERS == 0

    xp = x_prompt
    xs = x_sample.reshape(bs * ls, d)
    conv_p, conv_s, ret_p, ret_s = [], [], [], []
    for j in range(depth // N_MIXERS):
        ic, ir = N_MIXERS * j, N_MIXERS * j + 1
        last = ir == depth - 1
        g_ffn_c, g_ffn_r = norm_ffn[ic][None, :], norm_ffn[ir][None, :]
        gn = ret_gn[j][None, :]

        jobs = _CastJobs([(ffn_w_gate, ic), (ffn_w_up, ic), (ffn_w_down, ic), (ret_w_in, j), (ret_w_out, j)])
        xp, st, (wg, wu, wd, rwin, rwout) = _conv_mixer_seq(
            xp, norm_mix[ic][None, :], conv_w_in, conv_w, conv_w_out, layer=j, jobs=jobs)
        conv_p.append(st)
        xs, st, cos, sin = _decode_conv_ffn(
            xs, state_conv[j], norm_mix[ic][None, :], conv_w_in, conv_w, conv_w_out, g_ffn_c, wg, wu, wd, fin, inv,
            conv_layer=j, ffn_layer=0, seq=ls, final_norm=False, rope_len=lp)
        conv_s.append(st)
        q, k, v, sg = _ret_proj(xs, norm_mix[ir][None, :], rwin, inv, layer=0, seq=ls, pos0=PAST_LEN)
        xp, ret_o, st = _ffn_decode(xp.reshape(bp * lp, d), g_ffn_c, wg, wu, wd, fin, q, k, v, state_ret,
                                    layer=0, final_norm=False, state_layer=j, seq=ls)
        ret_s.append(st)

        jobs = _CastJobs([(ffn_w_gate, ir), (ffn_w_up, ir), (ffn_w_down, ir)])
        xp, st, (wg, wu, wd) = _ret_layer_seq(xp.reshape(bp, lp, d), norm_mix[ir][None, :], rwin, cos, sin, gn,
                                              rwout, layer=0, jobs=jobs)
        ret_p.append(st)
        xs = _decode_ret_ffn(xs, ret_o, sg, gn, rwout, g_ffn_r, wg, wu, wd, fin, ret_layer=0, ffn_layer=0,
                             final_norm=last)
        xp = _ffn(xp.reshape(bp * lp, d), g_ffn_r, wg, wu, wd, fin, layer=0, final_norm=last).reshape(bp, lp, d)
    return (xp, xs.reshape(bs, ls, d), jnp.stack(conv_p), jnp.stack(conv_s), jnp.stack(ret_p),
            jnp.stack(ret_s))
```

```python
import functools
import math

import jax
import jax.numpy as jnp
from jax import lax
from jax.experimental import pallas as pl
from jax.experimental.pallas import tpu as pltpu

F32 = jnp.float32
BF16 = jnp.bfloat16

N_MIXERS = 2
CONV_WIDTH = 3
N_HEADS = 4
PAST_LEN = 16384
RMS_EPS = 1e-6
GN_EPS = 1e-6
ROPE_BASE = 10000.0

V7X_VMEM_LIMIT_CAP = 60000 * 1024
V7X_MXU_DIM = 256
SUBLANES = 8
ROW_TILE = 512
FF_CHUNK = V7X_MXU_DIM
CONV_CHUNK = V7X_MXU_DIM
SEQ_CHUNK = V7X_MXU_DIM
BF16_ROWS = 16
STEP_ROWS = BF16_ROWS
CAST_CHUNKS = 16
IN_SLOTS, OUT_SLOTS = 4, 2


def _nbytes(shape, dtype):
    return math.prod(shape) * jnp.dtype(dtype).itemsize


def _vmem_limit(buffers, temporaries):
    need = sum(_nbytes(s, d) * n for s, d, n in buffers) + temporaries
    return min(int(need * 1.25) + (4 << 20), V7X_VMEM_LIMIT_CAP)


def _const_spec(shape):
    return pl.BlockSpec(shape, lambda *_: (0,) * len(shape), pipeline_mode=pl.Buffered(1))


def _layer_spec(w, layer):
    return pl.BlockSpec((None,) + w.shape[1:], lambda *_: (layer, 0, 0), pipeline_mode=pl.Buffered(1))


class _CastJobs:
    def __init__(self, weights):
        self.weights = list(weights)
        self.n = len(self.weights)

    @staticmethod
    def chunk(grid_idx, grid):
        step = grid_idx[0]
        for extent, idx in zip(grid[1:], grid_idx[1:]):
            step = step * extent + idx
        return jnp.minimum(step, CAST_CHUNKS - 1)

    def specs(self, grid):
        assert math.prod(grid) >= CAST_CHUNKS
        ins, outs, shapes = [], [], []
        for w, layer in self.weights:
            rows, cols = w.shape[1:]
            assert rows % (CAST_CHUNKS * BF16_ROWS) == 0
            blk = (None, rows // CAST_CHUNKS, cols)
            ins.append(pl.BlockSpec(blk, lambda *g, layer=layer: (layer, self.chunk(g, grid), 0)))
            outs.append(pl.BlockSpec(blk, lambda *g: (0, self.chunk(g, grid), 0)))
            shapes.append(jax.ShapeDtypeStruct((1, rows, cols), BF16))
        return ins, outs, shapes

    def operands(self):
        return [w for w, _ in self.weights]

    def vmem(self):
        return [((w.shape[1] // CAST_CHUNKS, w.shape[2]), F32, 3) for w, _ in self.weights]

    def run(self, src_refs, dst_refs, step):
        @pl.when(step < CAST_CHUNKS)
        def _():
            for src, dst in zip(src_refs, dst_refs):
                dst[...] = src[...].astype(BF16)


def _rmsnorm(x, g):
    ms = jnp.mean(x * x, axis=-1, keepdims=True)
    return (x * lax.rsqrt(ms + RMS_EPS)) * g


def _dot(a, b):
    return jnp.dot(a, b, preferred_element_type=F32)


def _dot_nt(a, b):
    return lax.dot_general(a, b, (((1,), (1,)), ((), ())), preferred_element_type=F32)


def _dot_tn(a, b):
    return lax.dot_general(a, b, (((0,), (0,)), ((), ())), preferred_element_type=F32)


def _head_log_decay(h):
    return math.log(1.0 - 2.0 ** (-5.0 - h))


def _ffn_rows(x_ref, g_ref, wg_ref, wu_ref, wd_ref, fin_ref, o_ref, act_ref, rows, *, final_norm,
              after_chunk=None):
    x = x_ref[rows, :]
    n = _rmsnorm(x, g_ref[...]).astype(BF16)
    for j in range(wg_ref.shape[1] // FF_CHUNK):
        cols = slice(j * FF_CHUNK, (j + 1) * FF_CHUNK)
        gate = _dot(n, wg_ref[:, cols])
        up = _dot(n, wu_ref[:, cols])
        act_ref[rows, cols] = (jax.nn.silu(gate) * up).astype(BF16)
        if after_chunk is not None:
            after_chunk(j)
    y = x + _dot(act_ref[rows, :], wd_ref[...])
    if final_norm:
        y = _rmsnorm(y, fin_ref[...])
    o_ref[rows, :] = y


def _ffn_kernel(x_ref, g_ref, wg_ref, wu_ref, wd_ref, fin_ref, o_ref, act_ref, *, final_norm):
    for r0 in range(0, x_ref.shape[0], ROW_TILE):
        _ffn_rows(x_ref, g_ref, wg_ref, wu_ref, wd_ref, fin_ref, o_ref, act_ref, slice(r0, r0 + ROW_TILE),
                  final_norm=final_norm)


def _ffn(x, g, wg, wu, wd, fin, *, layer, final_norm):
    m, d = x.shape
    f = wg.shape[2]
    tm = min(2 * ROW_TILE, m)
    assert m % tm == 0 and tm % ROW_TILE == 0 and f % FF_CHUNK == 0
    row = pl.BlockSpec((tm, d), lambda i: (i, 0))
    vmem = _vmem_limit(
        [((tm, d), F32, 4), ((d, f), BF16, 2), ((f, d), BF16, 1), ((tm, f), BF16, 1)],
        temporaries=6 * _nbytes((tm, d), F32))
    return pl.pallas_call(
        functools.partial(_ffn_kernel, final_norm=final_norm),
        grid=(m // tm,),
        in_specs=[row, _const_spec((1, d)), _layer_spec(wg, layer), _layer_spec(wu, layer),
                  _layer_spec(wd, layer), _const_spec((1, d))],
        out_specs=row,
        out_shape=jax.ShapeDtypeStruct((m, d), F32),
        scratch_shapes=[pltpu.VMEM((tm, f), BF16)],
        compiler_params=pltpu.CompilerParams(dimension_semantics=("arbitrary",), vmem_limit_bytes=vmem),
        name="ffn_final" if final_norm else "ffn",
    )(x, g, wg, wu, wd, fin)


def _conv_front(x, g_ref, win_ref):
    d = x.shape[1]
    hn = _rmsnorm(x, g_ref[...]).astype(BF16)
    b = _dot(hn, win_ref[:, 0:d])
    c = _dot(hn, win_ref[:, d:2 * d])
    h = _dot(hn, win_ref[:, 2 * d:3 * d])
    return b, c * h


def _conv_seq_kernel(x_ref, g_ref, win_ref, wc_ref, wout_ref, *refs, jobs):
    job_src, (o_ref, st_ref), job_dst, (ubuf_ref, act_ref) = (
        refs[:jobs.n], refs[jobs.n:jobs.n + 2], refs[jobs.n + 2:-2], refs[-2:])
    tm, d = x_ref.shape[1:]

    @pl.when(pl.program_id(1) == 0)
    def _():
        ubuf_ref[0:SUBLANES, :] = jnp.zeros((SUBLANES, d), F32)

    jobs.run(job_src, job_dst, pl.program_id(0) * pl.num_programs(1) + pl.program_id(1))
    for r0 in range(0, tm, ROW_TILE):
        rows = slice(r0, r0 + ROW_TILE)
        x = x_ref[0, rows, :]
        hn = _rmsnorm(x, g_ref[...]).astype(BF16)
        for c0 in range(0, d, CONV_CHUNK):
            cols = slice(c0, c0 + CONV_CHUNK)
            b = _dot(hn, win_ref[:, c0:c0 + CONV_CHUNK])
            u = (_dot(hn, win_ref[:, d + c0:d + c0 + CONV_CHUNK])
                 * _dot(hn, win_ref[:, 2 * d + c0:2 * d + c0 + CONV_CHUNK]))
            ubuf_ref[SUBLANES + r0:SUBLANES + r0 + ROW_TILE, cols] = u
            y = (wc_ref[0:1, cols] * ubuf_ref[SUBLANES - 2 + r0:SUBLANES - 2 + r0 + ROW_TILE, cols]
                 + wc_ref[1:2, cols] * ubuf_ref[SUBLANES - 1 + r0:SUBLANES - 1 + r0 + ROW_TILE, cols]
                 + wc_ref[2:3, cols] * u)
            act_ref[rows, cols] = (b * y).astype(BF16)
        o_ref[0, rows, :] = x + _dot(act_ref[rows, :], wout_ref[...])
    st_ref[0] = ubuf_ref[SUBLANES + tm - (CONV_WIDTH - 1):SUBLANES + tm, :]
    ubuf_ref[0:SUBLANES, :] = ubuf_ref[tm:tm + SUBLANES, :]


def _conv_mixer_seq(x, g, win, wc, wout, *, layer, jobs):
    bsz, seq, d = x.shape
    tm = 2 * ROW_TILE
    assert seq % tm == 0 and wc.shape[1] == CONV_WIDTH == 3
    grid = (bsz, seq // tm)
    job_in, job_out, job_shapes = jobs.specs(grid)
    row = pl.BlockSpec((1, tm, d), lambda b, t: (b, t, 0))
    vmem = _vmem_limit(
        [((tm, d), F32, 4), ((d, 3 * d), BF16, 1), ((d, d), BF16, 1), ((tm + SUBLANES, d), F32, 1)]
        + jobs.vmem(), temporaries=5 * _nbytes((tm, d), F32))
    y, st, *job_res = pl.pallas_call(
        functools.partial(_conv_seq_kernel, jobs=jobs),
        grid=grid,
        in_specs=[row, _const_spec((1, d)), _layer_spec(win, layer), _layer_spec(wc, layer),
                  _layer_spec(wout, layer)] + job_in,
        out_specs=[row, pl.BlockSpec((1, CONV_WIDTH - 1, d), lambda b, t: (b, 0, 0))] + job_out,
        out_shape=[jax.ShapeDtypeStruct((bsz, seq, d), F32),
                   jax.ShapeDtypeStruct((bsz, CONV_WIDTH - 1, d), F32)] + job_shapes,
        scratch_shapes=[pltpu.VMEM((tm + SUBLANES, d), F32), pltpu.VMEM((tm, d), BF16)],
        compiler_params=pltpu.CompilerParams(dimension_semantics=("arbitrary", "arbitrary"),
                                             vmem_limit_bytes=vmem),
        name="conv_mixer_seq",
    )(x, g, win, wc, wout, *jobs.operands())
    return y, st, job_res


def _conv_step_kernel(x_ref, pre_ref, g_ref, win_ref, wc_ref, wout_ref, o_ref, u_ref, *, seq):
    x = x_ref[...]
    m = x.shape[0]
    b, u = _conv_front(x, g_ref, win_ref)
    pre = pre_ref[...]
    t = lax.rem(lax.broadcasted_iota(jnp.int32, (m, 1), 0), seq)
    back1 = jnp.where(t >= 1, pltpu.roll(u, 1, axis=0), pltpu.roll(pre, m - 1, axis=0))
    back2 = jnp.where(t >= 2, pltpu.roll(u, 2, axis=0), pre)
    y = wc_ref[0:1, :] * back2 + wc_ref[1:2, :] * back1 + wc_ref[2:3, :] * u
    o_ref[...] = x + _dot((b * y).astype(BF16), wout_ref[...])
    u_ref[...] = u


def _copies(pairs, sem):
    return [pltpu.make_async_copy(src, dst, sem.at[n]) for n, (src, dst) in enumerate(pairs)]


def _decode_conv_ffn_kernel(x_ref, pre_ref, gmix_ref, wc_ref, gffn_ref, fin_ref, inv_ref,
                            win_hbm, wout_hbm, wg_hbm, wu_hbm, wd_hbm, u_ref, y_ref, cos_ref, sin_ref,
                            win_ref, wout_ref, wg_ref, wu_ref, wd_ref, x1_ref, act_ref, sem,
                            *, seq, conv_layer, ffn_layer, final_norm):
    mixer_w, ffn_w = 2, 3
    copies = _copies(
        [(win_hbm.at[conv_layer], win_ref), (wout_hbm.at[conv_layer], wout_ref), (wg_hbm.at[ffn_layer], wg_ref),
         (wu_hbm.at[ffn_layer], wu_ref), (wd_hbm.at[ffn_layer], wd_ref)], sem)
    for c in copies[:mixer_w]:
        c.start()
    pos = lax.broadcasted_iota(jnp.int32, cos_ref.shape, 0)
    cos_ref[...], sin_ref[...] = _rope_angles(inv_ref, pos)
    for c in copies[:mixer_w]:
        c.wait()
    for c in copies[mixer_w:mixer_w + ffn_w]:
        c.start()
    _conv_step_kernel(x_ref, pre_ref, gmix_ref, win_ref, wc_ref, wout_ref, x1_ref, u_ref, seq=seq)
    for c in copies[mixer_w:mixer_w + ffn_w]:
        c.wait()
    _ffn_rows(x1_ref, gffn_ref, wg_ref, wu_ref, wd_ref, fin_ref, y_ref, act_ref, slice(0, x_ref.shape[0]),
              final_norm=final_norm)


def _decode_conv_ffn(x, buf, g_mix, win, wc, wout, g_ffn, wg, wu, wd, fin, inv, *, conv_layer, ffn_layer, seq,
                     final_norm, rope_len):
    m, d = x.shape
    f = wg.shape[2]
    taps = CONV_WIDTH - 1
    assert wc.shape[1] == CONV_WIDTH == 3 and seq >= taps and m % seq == 0 and f % FF_CHUNK == 0
    pre = jnp.pad(buf, ((0, 0), (0, seq - taps), (0, 0))).reshape(m, d)
    full = pl.BlockSpec((m, d), lambda i: (0, 0))
    table = pl.BlockSpec((rope_len, inv.shape[1]), lambda i: (0, 0))
    hbm = pl.BlockSpec(memory_space=pl.ANY)
    weights = [(d, 3 * d), (d, d), (d, f), (d, f), (f, d)]
    vmem = _vmem_limit([((m, d), F32, 5), ((m, f), BF16, 1), ((rope_len, inv.shape[1]), F32, 2)]
                       + [(w, BF16, 1) for w in weights], temporaries=12 * _nbytes((m, d), F32))
    u, y, cos, sin = pl.pallas_call(
        functools.partial(_decode_conv_ffn_kernel, seq=seq, conv_layer=conv_layer, ffn_layer=ffn_layer,
                          final_norm=final_norm),
        grid=(1,),
        in_specs=[full, full, _const_spec((1, d)), _layer_spec(wc, conv_layer), _const_spec((1, d)),
                  _const_spec((1, d)), _const_spec(inv.shape)] + [hbm] * len(weights),
        out_specs=[full, full, table, table],
        out_shape=[jax.ShapeDtypeStruct((m, d), F32)] * 2
        + [jax.ShapeDtypeStruct((rope_len, inv.shape[1]), F32)] * 2,
        scratch_shapes=[pltpu.VMEM(w, BF16) for w in weights]
        + [pltpu.VMEM((m, d), F32), pltpu.VMEM((m, f), BF16), pltpu.SemaphoreType.DMA((len(weights),))],
        compiler_params=pltpu.CompilerParams(dimension_semantics=("arbitrary",), vmem_limit_bytes=vmem),
        name="decode_conv_ffn",
    )(x, pre, g_mix, wc, g_ffn, fin, inv, win, wout, wg, wu, wd)
    return y, u.reshape(m // seq, seq, d)[:, seq - taps:, :], cos, sin


def _rope_angles(inv_ref, pos):
    ang = pos.astype(F32) * inv_ref[...]
    return jnp.cos(ang), jnp.sin(ang)


def _rotate(z, cos, sin):
    half = z.shape[1] // 2
    x1, x2 = z[:, :half], z[:, half:]
    return jnp.concatenate([x1 * cos - x2 * sin, x1 * sin + x2 * cos], axis=1)


def _head_columns(ret_in, h):
    qk, vd = ret_in // 6, ret_in // 3
    dk, dv = qk // N_HEADS, vd // N_HEADS
    return (slice(h * dk, (h + 1) * dk), slice(qk + h * dk, qk + (h + 1) * dk),
            slice(2 * qk + h * dv, 2 * qk + (h + 1) * dv), slice(2 * qk + vd + h * dv, 2 * qk + vd + (h + 1) * dv))


def _proj_head(hn, wq, wk, wv, wg, cos, sin):
    q = _rotate(_dot(hn, wq), cos, sin)
    k = _rotate(_dot(hn, wk), cos, sin) * wk.shape[1] ** -0.5
    v = _dot(hn, wv)
    sg = jax.nn.silu(_dot(hn, wg))
    return q, k, v, sg


def _gn_gate(o, sg, gn):
    mu = jnp.mean(o, axis=-1, keepdims=True)
    var = jnp.mean(jnp.square(o - mu), axis=-1, keepdims=True)
    of = ((o - mu) * lax.rsqrt(var + GN_EPS)) * gn
    return (sg * of).astype(BF16)


def _ret_layer_seq_kernel(x_ref, g_ref, win_ref, cos_ref, sin_ref, gn_ref, wout_ref, *refs, jobs):
    job_src, (y_ref, s_ref), job_dst, act_ref = (
        refs[:jobs.n], refs[jobs.n:jobs.n + 2], refs[jobs.n + 2:-1], refs[-1])
    tm = x_ref.shape[1]
    dv = gn_ref.shape[1] // N_HEADS
    c_len = SEQ_CHUNK

    @pl.when(pl.program_id(1) == 0)
    def _():
        s_ref[...] = jnp.zeros(s_ref.shape, F32)

    jobs.run(job_src, job_dst, pl.program_id(0) * pl.num_programs(1) + pl.program_id(1))
    i = lax.broadcasted_iota(jnp.int32, (c_len, c_len), 0)
    j = lax.broadcasted_iota(jnp.int32, (c_len, c_len), 1)
    diff = (i - j).astype(F32)
    idx = lax.broadcasted_iota(jnp.int32, (c_len, 1), 0).astype(F32)
    for c in range(tm // c_len):
        rows = slice(c * c_len, (c + 1) * c_len)
        x = x_ref[0, rows, :]
        hn = _rmsnorm(x, g_ref[...]).astype(BF16)
        cos, sin = cos_ref[rows, :], sin_ref[rows, :]
        for h in range(N_HEADS):
            lg = _head_log_decay(h)
            decay = jnp.where(diff >= 0, jnp.exp(jnp.maximum(diff, 0.0) * lg), 0.0)
            cross_w = jnp.exp((idx + 1.0) * lg)
            state_w = jnp.exp((c_len - 1.0 - idx) * lg)
            chunk_decay = math.exp(c_len * lg)
            q, k, v, sg = _proj_head(hn, *(win_ref[:, c] for c in _head_columns(win_ref.shape[1], h)), cos, sin)
            qc, vc = q.astype(BF16), v.astype(BF16)
            cols = slice(h * dv, (h + 1) * dv)
            s = s_ref[0, h]
            scores = _dot_nt(qc, k.astype(BF16)) * decay
            o = _dot(scores.astype(BF16), vc) + _dot(qc, s.astype(BF16)) * cross_w
            s_ref[0, h] = s * chunk_decay + _dot_tn((k * state_w).astype(BF16), vc)
            act_ref[rows, cols] = _gn_gate(o, sg, gn_ref[:, cols])
        y_ref[0, rows, :] = x + _dot(act_ref[rows, :], wout_ref[...])


def _ret_layer_seq(x, g, win, cos, sin, gn, wout, *, layer, jobs):
    bsz, seq, d = x.shape
    ret_in, vd = win.shape[2], wout.shape[1]
    dk, dv = ret_in // 6 // N_HEADS, vd // N_HEADS
    tm = ROW_TILE
    assert seq % tm == 0 and tm % SEQ_CHUNK == 0 and cos.shape == (seq, dk // 2)
    grid = (bsz, seq // tm)
    job_in, job_out, job_shapes = jobs.specs(grid)
    row = pl.BlockSpec((1, tm, d), lambda b, t: (b, t, 0))
    rope = pl.BlockSpec((tm, dk // 2), lambda b, t: (t, 0))
    vmem = _vmem_limit(
        [((tm, d), F32, 4), ((d, ret_in), BF16, 1), ((vd, d), BF16, 1), ((tm, dk), F32, 4),
         ((N_HEADS, dk, dv), F32, 2), ((tm, vd), BF16, 1)] + jobs.vmem(),
        temporaries=8 * _nbytes((tm, d), F32))
    y, st, *job_res = pl.pallas_call(
        functools.partial(_ret_layer_seq_kernel, jobs=jobs),
        grid=grid,
        in_specs=[row, _const_spec((1, d)), _layer_spec(win, layer), rope, rope, _const_spec((1, vd)),
                  _layer_spec(wout, layer)] + job_in,
        out_specs=[row, pl.BlockSpec((1, N_HEADS, dk, dv), lambda b, t: (b, 0, 0, 0))] + job_out,
        out_shape=[jax.ShapeDtypeStruct((bsz, seq, d), F32),
                   jax.ShapeDtypeStruct((bsz, N_HEADS, dk, dv), F32)] + job_shapes,
        scratch_shapes=[pltpu.VMEM((tm, vd), BF16)],
        compiler_params=pltpu.CompilerParams(dimension_semantics=("arbitrary", "arbitrary"),
                                             vmem_limit_bytes=vmem),
        name="ret_layer_seq",
    )(x, g, win, cos, sin, gn, wout, *jobs.operands())
    return y, st, job_res


def _ret_proj_kernel(x_ref, g_ref, inv_ref, wq_ref, wk_ref, wv_ref, wg_ref, q_ref, k_ref, v_ref, sg_ref,
                     hn_ref, cos_ref, sin_ref, *, seq, pos0):
    @pl.when(pl.program_id(0) == 0)
    def _():
        hn_ref[...] = _rmsnorm(x_ref[...], g_ref[...]).astype(BF16)
        row = lax.broadcasted_iota(jnp.int32, cos_ref.shape, 0)
        cos_ref[...], sin_ref[...] = _rope_angles(inv_ref, pos0 + lax.rem(row, seq))

    q_ref[...], k_ref[...], v_ref[...], sg_ref[...] = _proj_head(
        hn_ref[...], wq_ref[...], wk_ref[...], wv_ref[...], wg_ref[...], cos_ref[...], sin_ref[...])


def _ret_proj(x, g, win, inv, *, layer, seq, pos0):
    m, d = x.shape
    ret_in = win.shape[2]
    qk, vd = ret_in // 6, ret_in // 3
    dk, dv = qk // N_HEADS, vd // N_HEADS
    assert m % seq == 0 and (2 * qk) % dv == 0
    wcol = lambda width, first: pl.BlockSpec((None, d, width), lambda h: (layer, 0, first // width + h))
    out = lambda width: pl.BlockSpec((m, width), lambda h: (0, h))
    vmem = _vmem_limit(
        [((m, d), F32, 1), ((m, d), BF16, 1), ((d, 2 * dk + 2 * dv), BF16, 2), ((m, 2 * dk + 2 * dv), F32, 2)],
        temporaries=8 * _nbytes((m, d), F32))
    return pl.pallas_call(
        functools.partial(_ret_proj_kernel, seq=seq, pos0=pos0),
        grid=(N_HEADS,),
        in_specs=[_const_spec((m, d)), _const_spec((1, d)), _const_spec(inv.shape),
                  wcol(dk, 0), wcol(dk, qk), wcol(dv, 2 * qk), wcol(dv, 2 * qk + vd)],
        out_specs=[out(dk), out(dk), out(dv), out(dv)],
        out_shape=[jax.ShapeDtypeStruct((m, qk), F32), jax.ShapeDtypeStruct((m, qk), F32),
                   jax.ShapeDtypeStruct((m, vd), F32), jax.ShapeDtypeStruct((m, vd), F32)],
        scratch_shapes=[pltpu.VMEM((m, d), BF16), pltpu.VMEM((m, dk // 2), F32), pltpu.VMEM((m, dk // 2), F32)],
        compiler_params=pltpu.CompilerParams(dimension_semantics=("arbitrary",), vmem_limit_bytes=vmem),
        name="ret_proj",
    )(x, g, inv, win, win, win, win)


class _DecodeGroup:
    def __init__(self, q_ref, k_ref, v_ref, rows, seq):
        n = rows.stop - rows.start
        dk, dv = q_ref.shape[1] // N_HEADS, v_ref.shape[1] // N_HEADS
        self.seq = seq
        self.r = lax.broadcasted_iota(jnp.int32, (n, 1), 0)
        t = lax.rem(self.r, seq)
        tf = t.astype(F32)
        self.qb, self.kw, self.vb, self.inner, self.cross, self.cross_w = [], [], [], [], [], []
        for h in range(N_HEADS):
            lg = _head_log_decay(h)
            q = q_ref[rows, h * dk:(h + 1) * dk]
            k = k_ref[rows, h * dk:(h + 1) * dk]
            v = v_ref[rows, h * dv:(h + 1) * dv]
            acc = jnp.zeros((n, dv), F32)
            for dlt in range(seq):
                kd = k if dlt == 0 else pltpu.roll(k, dlt, axis=0)
                vd = v if dlt == 0 else pltpu.roll(v, dlt, axis=0)
                sc = jnp.sum(q * kd, axis=-1, keepdims=True) * math.exp(dlt * lg)
                acc = acc + jnp.where(t >= dlt, sc, 0.0) * vd
            self.inner.append(acc)
            self.cross.append(jnp.zeros((n, dv), F32))
            self.cross_w.append(jnp.exp((tf + 1.0) * lg))
            self.qb.append(q.astype(BF16))
            self.kw.append(k * jnp.exp((seq - 1.0 - tf) * lg))
            self.vb.append(v.astype(BF16))

    def advance(self, b, s0_ref, s_ref):
        own = (self.r >= b * self.seq) & (self.r < (b + 1) * self.seq)
        for h in range(N_HEADS):
            s0 = s0_ref[h]
            self.cross[h] = self.cross[h] + jnp.where(own, _dot(self.qb[h], s0.astype(BF16)), 0.0)
            kwb = jnp.where(own, self.kw[h], 0.0).astype(BF16)
            s_ref[h] = s0 * math.exp(self.seq * _head_log_decay(h)) + _dot_tn(kwb, self.vb[h])

    def output(self, h):
        return self.inner[h] + self.cross[h] * self.cross_w[h]


def _ffn_decode_kernel(x_ref, g_ref, wg_ref, wu_ref, wd_ref, fin_ref, q_ref, k_ref, v_ref, s0_hbm,
                       o_ref, ro_ref, s_hbm, act_ref, sin_ref, sout_ref, sem_in, sem_out,
                       *, final_norm, seq, layer):
    i = pl.program_id(0)
    last = pl.num_programs(0) - 1
    n_sub = x_ref.shape[0] // ROW_TILE
    per_group = STEP_ROWS // seq
    per_step = n_sub * per_group
    n_chunks = wg_ref.shape[1] // FF_CHUNK
    dv = v_ref.shape[1] // N_HEADS
    run_after = {(u + 1) * n_chunks // per_group - 1: u for u in range(per_group)}

    def in_copy(n, slot):
        return pltpu.make_async_copy(s0_hbm.at[layer, n], sin_ref.at[slot], sem_in.at[slot])

    def out_copy(n, slot):
        return pltpu.make_async_copy(sout_ref.at[slot], s_hbm.at[n], sem_out.at[slot])

    lead = IN_SLOTS - 1

    @pl.when(i == 0)
    def _():
        for n in range(lead):
            in_copy(n, n % IN_SLOTS).start()

    def advance(group, sub, b):
        l = sub * per_group + b
        n = i * per_step + l
        slot, oslot = l % IN_SLOTS, l % OUT_SLOTS
        in_copy(n, slot).wait()
        if l + lead < per_step:
            in_copy(n + lead, (l + lead) % IN_SLOTS).start()
        else:
            pl.when(i < last)(lambda: in_copy(n + lead, (l + lead) % IN_SLOTS).start())
        if l >= OUT_SLOTS:
            out_copy(n - OUT_SLOTS, oslot).wait()
        else:
            pl.when(i > 0)(lambda: out_copy(n - OUT_SLOTS, oslot).wait())
        group.advance(b, sin_ref.at[slot], sout_ref.at[oslot])
        out_copy(n, oslot).start()

    for sub in range(n_sub):
        rows16 = slice(sub * STEP_ROWS, (sub + 1) * STEP_ROWS)
        group = _DecodeGroup(q_ref, k_ref, v_ref, rows16, seq)

        def after_chunk(j, group=group, sub=sub):
            if j in run_after:
                advance(group, sub, run_after[j])

        _ffn_rows(x_ref, g_ref, wg_ref, wu_ref, wd_ref, fin_ref, o_ref, act_ref,
                  slice(sub * ROW_TILE, (sub + 1) * ROW_TILE), final_norm=final_norm, after_chunk=after_chunk)
        for h in range(N_HEADS):
            ro_ref[rows16, h * dv:(h + 1) * dv] = group.output(h)

    @pl.when(i == last)
    def _():
        for l in range(per_step - OUT_SLOTS, per_step):
            out_copy(i * per_step + l, l % OUT_SLOTS).wait()


def _ffn_decode(x, g, wg, wu, wd, fin, q, k, v, s0, *, layer, final_norm, state_layer, seq):
    m, d = x.shape
    f = wg.shape[2]
    ms, qk = q.shape
    vd = v.shape[1]
    _, bsz, _, dk, dv = s0.shape
    tm = 2 * ROW_TILE
    steps = m // tm
    rows_s = (tm // ROW_TILE) * STEP_ROWS
    assert m % tm == 0 and f % FF_CHUNK == 0 and f // FF_CHUNK >= STEP_ROWS // seq
    assert seq <= SEQ_CHUNK and STEP_ROWS % seq == 0 and ms == bsz * seq and ms == steps * rows_s
    assert (rows_s // seq) % IN_SLOTS == 0 and (rows_s // seq) % OUT_SLOTS == 0
    row = pl.BlockSpec((tm, d), lambda i: (i, 0))
    srow = lambda n: pl.BlockSpec((rows_s, n), lambda i: (i, 0))
    hbm = pl.BlockSpec(memory_space=pl.ANY)
    state = (N_HEADS, dk, dv)
    vmem = _vmem_limit(
        [((tm, d), F32, 4), ((d, f), BF16, 2), ((f, d), BF16, 1), ((tm, f), BF16, 1),
         (state, F32, IN_SLOTS + OUT_SLOTS),
         ((rows_s, vd), F32, 8)],
        temporaries=8 * _nbytes((tm, d), F32))
    return pl.pallas_call(
        functools.partial(_ffn_decode_kernel, final_norm=final_norm, seq=seq, layer=state_layer),
        grid=(steps,),
        in_specs=[row, _const_spec((1, d)), _layer_spec(wg, layer), _layer_spec(wu, layer),
                  _layer_spec(wd, layer), _const_spec((1, d)), srow(qk), srow(qk), srow(vd), hbm],
        out_specs=[row, srow(vd), hbm],
        out_shape=[jax.ShapeDtypeStruct((m, d), F32), jax.ShapeDtypeStruct((ms, vd), F32),
                   jax.ShapeDtypeStruct(s0.shape[1:], F32)],
        scratch_shapes=[pltpu.VMEM((tm, f), BF16), pltpu.VMEM((IN_SLOTS,) + state, F32),
                        pltpu.VMEM((OUT_SLOTS,) + state, F32), pltpu.SemaphoreType.DMA((IN_SLOTS,)),
                        pltpu.SemaphoreType.DMA((OUT_SLOTS,))],
        compiler_params=pltpu.CompilerParams(dimension_semantics=("arbitrary",), vmem_limit_bytes=vmem),
        name="ffn_decode_ret",
    )(x, g, wg, wu, wd, fin, q, k, v, s0)


def _ret_out_kernel(x_ref, o_ref, sg_ref, gn_ref, wout_ref, y_ref, act_ref):
    dv = o_ref.shape[1] // N_HEADS
    for h in range(N_HEADS):
        cols = slice(h * dv, (h + 1) * dv)
        act_ref[:, cols] = _gn_gate(o_ref[:, cols], sg_ref[:, cols], gn_ref[:, cols])
    y_ref[...] = x_ref[...] + _dot(act_ref[...], wout_ref[...])


def _decode_ret_ffn_kernel(x_ref, o_ref, sg_ref, gn_ref, gffn_ref, fin_ref, rwout_hbm, wg_hbm, wu_hbm, wd_hbm,
                           y_ref, rwout_ref, wg_ref, wu_ref, wd_ref, x1_ref, ract_ref, act_ref, sem,
                           *, ret_layer, ffn_layer, final_norm):
    copies = _copies(
        [(rwout_hbm.at[ret_layer], rwout_ref), (wg_hbm.at[ffn_layer], wg_ref), (wu_hbm.at[ffn_layer], wu_ref),
         (wd_hbm.at[ffn_layer], wd_ref)], sem)
    copies[0].start()
    copies[0].wait()
    for c in copies[1:]:
        c.start()
    _ret_out_kernel(x_ref, o_ref, sg_ref, gn_ref, rwout_ref, x1_ref, ract_ref)
    for c in copies[1:]:
        c.wait()
    _ffn_rows(x1_ref, gffn_ref, wg_ref, wu_ref, wd_ref, fin_ref, y_ref, act_ref, slice(0, x_ref.shape[0]),
              final_norm=final_norm)


def _decode_ret_ffn(x, o, sg, gn, rwout, g_ffn, wg, wu, wd, fin, *, ret_layer, ffn_layer, final_norm):
    m, d = x.shape
    vd, f = o.shape[1], wg.shape[2]
    assert f % FF_CHUNK == 0
    full = lambda n: pl.BlockSpec((m, n), lambda i: (0, 0))
    hbm = pl.BlockSpec(memory_space=pl.ANY)
    weights = [(vd, d), (d, f), (d, f), (f, d)]
    vmem = _vmem_limit([((m, d), F32, 3), ((m, vd), F32, 2), ((m, vd), BF16, 1), ((m, f), BF16, 1)]
                       + [(w, BF16, 1) for w in weights], temporaries=8 * _nbytes((m, d), F32))
    return pl.pallas_call(
        functools.partial(_decode_ret_ffn_kernel, ret_layer=ret_layer, ffn_layer=ffn_layer,
                          final_norm=final_norm),
        grid=(1,),
        in_specs=[full(d), full(vd), full(vd), _const_spec((1, vd)), _const_spec((1, d)), _const_spec((1, d))]
        + [hbm] * len(weights),
        out_specs=full(d),
        out_shape=jax.ShapeDtypeStruct((m, d), F32),
        scratch_shapes=[pltpu.VMEM(w, BF16) for w in weights]
        + [pltpu.VMEM((m, d), F32), pltpu.VMEM((m, vd), BF16), pltpu.VMEM((m, f), BF16),
           pltpu.SemaphoreType.DMA((len(weights),))],
        compiler_params=pltpu.CompilerParams(dimension_semantics=("arbitrary",), vmem_limit_bytes=vmem),
        name="decode_ret_ffn",
    )(x, o, sg, gn, g_ffn, fin, rwout, wg, wu, wd)


def kernel(x_prompt, x_sample, state_conv, state_ret, norm_mix, norm_ffn, conv_w_in, conv_w, conv_w_out,
           ret_w_in, ret_gn, ret_w_out, ffn_w_gate, ffn_w_up, ffn_w_down, final_norm):
    bp, lp, d = x_prompt.shape
    bs, ls, _ = x_sample.shape
    depth = norm_mix.shape[0]
    dk = ret_w_in.shape[2] // 6 // N_HEADS
    inv = (ROPE_BASE ** (-jnp.arange(dk // 2, dtype=F32) / (dk // 2)))[None, :]
    fin = final_norm[None, :]
    conv_w_in, conv_w_out = conv_w_in.astype(BF16), conv_w_out.astype(BF16)
    assert depth % N_MIXERS == 0

    xp = x_prompt
    xs = x_sample.reshape(bs * ls, d)
    conv_p, conv_s, ret_p, ret_s = [], [], [], []
    for j in range(depth // N_MIXERS):
        ic, ir = N_MIXERS * j, N_MIXERS * j + 1
        last = ir == depth - 1
        g_ffn_c, g_ffn_r = norm_ffn[ic][None, :], norm_ffn[ir][None, :]
        gn = ret_gn[j][None, :]

        jobs = _CastJobs([(ffn_w_gate, ic), (ffn_w_up, ic), (ffn_w_down, ic), (ret_w_in, j), (ret_w_out, j)])
        xp, st, (wg, wu, wd, rwin, rwout) = _conv_mixer_seq(
            xp, norm_mix[ic][None, :], conv_w_in, conv_w, conv_w_out, layer=j, jobs=jobs)
        conv_p.append(st)
        xs, st, cos, sin = _decode_conv_ffn(
            xs, state_conv[j], norm_mix[ic][None, :], conv_w_in, conv_w, conv_w_out, g_ffn_c, wg, wu, wd, fin, inv,
            conv_layer=j, ffn_layer=0, seq=ls, final_norm=False, rope_len=lp)
        conv_s.append(st)
        q, k, v, sg = _ret_proj(xs, norm_mix[ir][None, :], rwin, inv, layer=0, seq=ls, pos0=PAST_LEN)
        xp, ret_o, st = _ffn_decode(xp.reshape(bp * lp, d), g_ffn_c, wg, wu, wd, fin, q, k, v, state_ret,
                                    layer=0, final_norm=False, state_layer=j, seq=ls)
        ret_s.append(st)

        jobs = _CastJobs([(ffn_w_gate, ir), (ffn_w_up, ir), (ffn_w_down, ir)])
        xp, st, (wg, wu, wd) = _ret_layer_seq(xp.reshape(bp, lp, d), norm_mix[ir][None, :], rwin, cos, sin, gn,
                                              rwout, layer=0, jobs=jobs)
        ret_p.append(st)
        xs = _decode_ret_ffn(xs, ret_o, sg, gn, rwout, g_ffn_r, wg, wu, wd, fin, ret_layer=0, ffn_layer=0,
                             final_norm=last)
        xp = _ffn(xp.reshape(bp * lp, d), g_ffn_r, wg, wu, wd, fin, layer=0, final_norm=last).reshape(bp, lp, d)
    return (xp, xs.reshape(bs, ls, d), jnp.stack(conv_p), jnp.stack(conv_s), jnp.stack(ret_p),
            jnp.stack(ret_s))
```

```python
import functools
import math

import jax
import jax.numpy as jnp
from jax import lax
from jax.experimental import pallas as pl
from jax.experimental.pallas import tpu as pltpu

F32 = jnp.float32
BF16 = jnp.bfloat16

N_MIXERS = 2
CONV_WIDTH = 3
N_HEADS = 4
PAST_LEN = 16384
RMS_EPS = 1e-6
GN_EPS = 1e-6
ROPE_BASE = 10000.0

V7X_VMEM_LIMIT_CAP = 60000 * 1024
VMEM_SLACK = 2 << 20
V7X_MXU_DIM = 256
SUBLANES = 8
ROW_TILE = 512
FF_CHUNK = V7X_MXU_DIM
CONV_CHUNK = V7X_MXU_DIM
SEQ_CHUNK = V7X_MXU_DIM
BF16_ROWS = 16
STEP_ROWS = BF16_ROWS
CAST_CHUNKS = 16
IN_SLOTS, OUT_SLOTS = 4, 2


def _nbytes(shape, dtype):
    return math.prod(shape) * jnp.dtype(dtype).itemsize


def _vmem_limit(buffers, temporaries):
    need = sum(_nbytes(s, d) * n for s, d, n in buffers) + temporaries
    return min(need + VMEM_SLACK, V7X_VMEM_LIMIT_CAP)


def _const_spec(shape):
    return pl.BlockSpec(shape, lambda *_: (0,) * len(shape), pipeline_mode=pl.Buffered(1))


def _layer_spec(w, layer):
    return pl.BlockSpec((None,) + w.shape[1:], lambda *_: (layer, 0, 0), pipeline_mode=pl.Buffered(1))


class _CastJobs:
    def __init__(self, weights):
        self.weights = list(weights)
        self.n = len(self.weights)

    @staticmethod
    def chunk(grid_idx, grid):
        step = grid_idx[0]
        for extent, idx in zip(grid[1:], grid_idx[1:]):
            step = step * extent + idx
        return jnp.minimum(step, CAST_CHUNKS - 1)

    def specs(self, grid):
        assert math.prod(grid) >= CAST_CHUNKS
        ins, outs, shapes = [], [], []
        for w, layer in self.weights:
            rows, cols = w.shape[1:]
            assert rows % (CAST_CHUNKS * BF16_ROWS) == 0
            blk = (None, rows // CAST_CHUNKS, cols)
            ins.append(pl.BlockSpec(blk, lambda *g, layer=layer: (layer, self.chunk(g, grid), 0)))
            outs.append(pl.BlockSpec(blk, lambda *g: (0, self.chunk(g, grid), 0)))
            shapes.append(jax.ShapeDtypeStruct((1, rows, cols), BF16))
        return ins, outs, shapes

    def operands(self):
        return [w for w, _ in self.weights]

    def vmem(self):
        return [((w.shape[1] // CAST_CHUNKS, w.shape[2]), F32, 3) for w, _ in self.weights]

    def run(self, src_refs, dst_refs, step):
        @pl.when(step < CAST_CHUNKS)
        def _():
            for src, dst in zip(src_refs, dst_refs):
                dst[...] = src[...].astype(BF16)


def _rmsnorm(x, g):
    ms = jnp.mean(x * x, axis=-1, keepdims=True)
    return (x * lax.rsqrt(ms + RMS_EPS)) * g


def _dot(a, b):
    return jnp.dot(a, b, preferred_element_type=F32)


def _dot_nt(a, b):
    return lax.dot_general(a, b, (((1,), (1,)), ((), ())), preferred_element_type=F32)


def _dot_tn(a, b):
    return lax.dot_general(a, b, (((0,), (0,)), ((), ())), preferred_element_type=F32)


def _head_log_decay(h):
    return math.log(1.0 - 2.0 ** (-5.0 - h))


def _ffn_rows(x_ref, g_ref, wg_ref, wu_ref, wd_ref, fin_ref, o_ref, act_ref, rows, *, final_norm,
              after_chunk=None):
    x = x_ref[rows, :]
    n = _rmsnorm(x, g_ref[...]).astype(BF16)
    for j in range(wg_ref.shape[1] // FF_CHUNK):
        cols = slice(j * FF_CHUNK, (j + 1) * FF_CHUNK)
        gate = _dot(n, wg_ref[:, cols])
        up = _dot(n, wu_ref[:, cols])
        act_ref[rows, cols] = (jax.nn.silu(gate) * up).astype(BF16)
        if after_chunk is not None:
            after_chunk(j)
    y = x + _dot(act_ref[rows, :], wd_ref[...])
    if final_norm:
        y = _rmsnorm(y, fin_ref[...])
    o_ref[rows, :] = y


def _ffn_kernel(x_ref, g_ref, wg_ref, wu_ref, wd_ref, fin_ref, o_ref, act_ref, *, final_norm):
    for r0 in range(0, x_ref.shape[0], ROW_TILE):
        _ffn_rows(x_ref, g_ref, wg_ref, wu_ref, wd_ref, fin_ref, o_ref, act_ref, slice(r0, r0 + ROW_TILE),
                  final_norm=final_norm)


def _ffn(x, g, wg, wu, wd, fin, *, layer, final_norm):
    m, d = x.shape
    f = wg.shape[2]
    tm = min(2 * ROW_TILE, m)
    assert m % tm == 0 and tm % ROW_TILE == 0 and f % FF_CHUNK == 0
    row = pl.BlockSpec((tm, d), lambda i: (i, 0))
    vmem = _vmem_limit(
        [((tm, d), F32, 4), ((d, f), BF16, 2), ((f, d), BF16, 1), ((tm, f), BF16, 1)],
        temporaries=_nbytes((tm, d), F32))
    return pl.pallas_call(
        functools.partial(_ffn_kernel, final_norm=final_norm),
        grid=(m // tm,),
        in_specs=[row, _const_spec((1, d)), _layer_spec(wg, layer), _layer_spec(wu, layer),
                  _layer_spec(wd, layer), _const_spec((1, d))],
        out_specs=row,
        out_shape=jax.ShapeDtypeStruct((m, d), F32),
        scratch_shapes=[pltpu.VMEM((tm, f), BF16)],
        compiler_params=pltpu.CompilerParams(dimension_semantics=("arbitrary",), vmem_limit_bytes=vmem),
        name="ffn_final" if final_norm else "ffn",
    )(x, g, wg, wu, wd, fin)


def _conv_front(x, g_ref, win_ref):
    d = x.shape[1]
    hn = _rmsnorm(x, g_ref[...]).astype(BF16)
    b = _dot(hn, win_ref[:, 0:d])
    c = _dot(hn, win_ref[:, d:2 * d])
    h = _dot(hn, win_ref[:, 2 * d:3 * d])
    return b, c * h


def _conv_seq_kernel(x_ref, g_ref, win_ref, wc_ref, wout_ref, *refs, jobs):
    job_src, (o_ref, st_ref), job_dst, (ubuf_ref, act_ref) = (
        refs[:jobs.n], refs[jobs.n:jobs.n + 2], refs[jobs.n + 2:-2], refs[-2:])
    tm, d = x_ref.shape[1:]

    @pl.when(pl.program_id(1) == 0)
    def _():
        ubuf_ref[0:SUBLANES, :] = jnp.zeros((SUBLANES, d), F32)

    jobs.run(job_src, job_dst, pl.program_id(0) * pl.num_programs(1) + pl.program_id(1))
    for r0 in range(0, tm, ROW_TILE):
        rows = slice(r0, r0 + ROW_TILE)
        x = x_ref[0, rows, :]
        hn = _rmsnorm(x, g_ref[...]).astype(BF16)
        for c0 in range(0, d, CONV_CHUNK):
            cols = slice(c0, c0 + CONV_CHUNK)
            b = _dot(hn, win_ref[:, c0:c0 + CONV_CHUNK])
            u = (_dot(hn, win_ref[:, d + c0:d + c0 + CONV_CHUNK])
                 * _dot(hn, win_ref[:, 2 * d + c0:2 * d + c0 + CONV_CHUNK]))
            ubuf_ref[SUBLANES + r0:SUBLANES + r0 + ROW_TILE, cols] = u
            y = (wc_ref[0:1, cols] * ubuf_ref[SUBLANES - 2 + r0:SUBLANES - 2 + r0 + ROW_TILE, cols]
                 + wc_ref[1:2, cols] * ubuf_ref[SUBLANES - 1 + r0:SUBLANES - 1 + r0 + ROW_TILE, cols]
                 + wc_ref[2:3, cols] * u)
            act_ref[rows, cols] = (b * y).astype(BF16)
        o_ref[0, rows, :] = x + _dot(act_ref[rows, :], wout_ref[...])
    st_ref[0] = ubuf_ref[SUBLANES + tm - (CONV_WIDTH - 1):SUBLANES + tm, :]
    ubuf_ref[0:SUBLANES, :] = ubuf_ref[tm:tm + SUBLANES, :]


def _conv_mixer_seq(x, g, win, wc, wout, *, layer, jobs):
    bsz, seq, d = x.shape
    tm = 2 * ROW_TILE
    assert seq % tm == 0 and wc.shape[1] == CONV_WIDTH == 3
    grid = (bsz, seq // tm)
    job_in, job_out, job_shapes = jobs.specs(grid)
    row = pl.BlockSpec((1, tm, d), lambda b, t: (b, t, 0))
    vmem = _vmem_limit(
        [((tm, d), F32, 4), ((d, 3 * d), BF16, 1), ((d, d), BF16, 1), ((tm + SUBLANES, d), F32, 1)]
        + jobs.vmem(), temporaries=2 * _nbytes((tm, d), F32))
    y, st, *job_res = pl.pallas_call(
        functools.partial(_conv_seq_kernel, jobs=jobs),
        grid=grid,
        in_specs=[row, _const_spec((1, d)), _layer_spec(win, layer), _layer_spec(wc, layer),
                  _layer_spec(wout, layer)] + job_in,
        out_specs=[row, pl.BlockSpec((1, CONV_WIDTH - 1, d), lambda b, t: (b, 0, 0))] + job_out,
        out_shape=[jax.ShapeDtypeStruct((bsz, seq, d), F32),
                   jax.ShapeDtypeStruct((bsz, CONV_WIDTH - 1, d), F32)] + job_shapes,
        scratch_shapes=[pltpu.VMEM((tm + SUBLANES, d), F32), pltpu.VMEM((tm, d), BF16)],
        compiler_params=pltpu.CompilerParams(dimension_semantics=("arbitrary", "arbitrary"),
                                             vmem_limit_bytes=vmem),
        name="conv_mixer_seq",
    )(x, g, win, wc, wout, *jobs.operands())
    return y, st, job_res


def _conv_step_kernel(x_ref, pre_ref, g_ref, win_ref, wc_ref, wout_ref, o_ref, u_ref, *, seq):
    x = x_ref[...]
    m = x.shape[0]
    b, u = _conv_front(x, g_ref, win_ref)
    pre = pre_ref[...]
    t = lax.rem(lax.broadcasted_iota(jnp.int32, (m, 1), 0), seq)
    back1 = jnp.where(t >= 1, pltpu.roll(u, 1, axis=0), pltpu.roll(pre, m - 1, axis=0))
    back2 = jnp.where(t >= 2, pltpu.roll(u, 2, axis=0), pre)
    y = wc_ref[0:1, :] * back2 + wc_ref[1:2, :] * back1 + wc_ref[2:3, :] * u
    o_ref[...] = x + _dot((b * y).astype(BF16), wout_ref[...])
    u_ref[...] = u


def _copies(pairs, sem):
    return [pltpu.make_async_copy(src, dst, sem.at[n]) for n, (src, dst) in enumerate(pairs)]


def _decode_conv_ffn_kernel(x_ref, pre_ref, gmix_ref, wc_ref, gffn_ref, fin_ref, inv_ref,
                            win_hbm, wout_hbm, wg_hbm, wu_hbm, wd_hbm, u_ref, y_ref, cos_ref, sin_ref,
                            win_ref, wout_ref, wg_ref, wu_ref, wd_ref, x1_ref, act_ref, sem,
                            *, seq, conv_layer, ffn_layer, final_norm):
    mixer_w, ffn_w = 2, 3
    copies = _copies(
        [(win_hbm.at[conv_layer], win_ref), (wout_hbm.at[conv_layer], wout_ref), (wg_hbm.at[ffn_layer], wg_ref),
         (wu_hbm.at[ffn_layer], wu_ref), (wd_hbm.at[ffn_layer], wd_ref)], sem)
    for c in copies[:mixer_w]:
        c.start()
    pos = lax.broadcasted_iota(jnp.int32, cos_ref.shape, 0)
    cos_ref[...], sin_ref[...] = _rope_angles(inv_ref, pos)
    for c in copies[:mixer_w]:
        c.wait()
    for c in copies[mixer_w:mixer_w + ffn_w]:
        c.start()
    _conv_step_kernel(x_ref, pre_ref, gmix_ref, win_ref, wc_ref, wout_ref, x1_ref, u_ref, seq=seq)
    for c in copies[mixer_w:mixer_w + ffn_w]:
        c.wait()
    _ffn_rows(x1_ref, gffn_ref, wg_ref, wu_ref, wd_ref, fin_ref, y_ref, act_ref, slice(0, x_ref.shape[0]),
              final_norm=final_norm)


def _decode_conv_ffn(x, buf, g_mix, win, wc, wout, g_ffn, wg, wu, wd, fin, inv, *, conv_layer, ffn_layer, seq,
                     final_norm, rope_len):
    m, d = x.shape
    f = wg.shape[2]
    taps = CONV_WIDTH - 1
    assert wc.shape[1] == CONV_WIDTH == 3 and seq >= taps and m % seq == 0 and f % FF_CHUNK == 0
    pre = jnp.pad(buf, ((0, 0), (0, seq - taps), (0, 0))).reshape(m, d)
    full = pl.BlockSpec((m, d), lambda i: (0, 0))
    table = pl.BlockSpec((rope_len, inv.shape[1]), lambda i: (0, 0))
    hbm = pl.BlockSpec(memory_space=pl.ANY)
    weights = [(d, 3 * d), (d, d), (d, f), (d, f), (f, d)]
    vmem = _vmem_limit([((m, d), F32, 5), ((m, f), BF16, 1), ((rope_len, inv.shape[1]), F32, 2)]
                       + [(w, BF16, 1) for w in weights], temporaries=4 * _nbytes((m, d), F32))
    u, y, cos, sin = pl.pallas_call(
        functools.partial(_decode_conv_ffn_kernel, seq=seq, conv_layer=conv_layer, ffn_layer=ffn_layer,
                          final_norm=final_norm),
        grid=(1,),
        in_specs=[full, full, _const_spec((1, d)), _layer_spec(wc, conv_layer), _const_spec((1, d)),
                  _const_spec((1, d)), _const_spec(inv.shape)] + [hbm] * len(weights),
        out_specs=[full, full, table, table],
        out_shape=[jax.ShapeDtypeStruct((m, d), F32)] * 2
        + [jax.ShapeDtypeStruct((rope_len, inv.shape[1]), F32)] * 2,
        scratch_shapes=[pltpu.VMEM(w, BF16) for w in weights]
        + [pltpu.VMEM((m, d), F32), pltpu.VMEM((m, f), BF16), pltpu.SemaphoreType.DMA((len(weights),))],
        compiler_params=pltpu.CompilerParams(dimension_semantics=("arbitrary",), vmem_limit_bytes=vmem),
        name="decode_conv_ffn",
    )(x, pre, g_mix, wc, g_ffn, fin, inv, win, wout, wg, wu, wd)
    return y, u.reshape(m // seq, seq, d)[:, seq - taps:, :], cos, sin


def _rope_angles(inv_ref, pos):
    ang = pos.astype(F32) * inv_ref[...]
    return jnp.cos(ang), jnp.sin(ang)


def _rotate(z, cos, sin):
    half = z.shape[1] // 2
    x1, x2 = z[:, :half], z[:, half:]
    return jnp.concatenate([x1 * cos - x2 * sin, x1 * sin + x2 * cos], axis=1)


def _proj_head(hn, win_ref, h, cos, sin):
    ret_in = win_ref.shape[1]
    qk, vd = ret_in // 6, ret_in // 3
    dk, dv = qk // N_HEADS, vd // N_HEADS
    q = _rotate(_dot(hn, win_ref[:, h * dk:(h + 1) * dk]), cos, sin)
    k = _rotate(_dot(hn, win_ref[:, qk + h * dk:qk + (h + 1) * dk]), cos, sin) * dk ** -0.5
    v = _dot(hn, win_ref[:, 2 * qk + h * dv:2 * qk + (h + 1) * dv])
    sg = jax.nn.silu(_dot(hn, win_ref[:, 2 * qk + vd + h * dv:2 * qk + vd + (h + 1) * dv]))
    return q, k, v, sg


def _gn_gate(o, sg, gn):
    mu = jnp.mean(o, axis=-1, keepdims=True)
    var = jnp.mean(jnp.square(o - mu), axis=-1, keepdims=True)
    of = ((o - mu) * lax.rsqrt(var + GN_EPS)) * gn
    return (sg * of).astype(BF16)


def _ret_layer_seq_kernel(x_ref, g_ref, win_ref, cos_ref, sin_ref, gn_ref, wout_ref, *refs, jobs):
    job_src, (y_ref, s_ref), job_dst, act_ref = (
        refs[:jobs.n], refs[jobs.n:jobs.n + 2], refs[jobs.n + 2:-1], refs[-1])
    tm = x_ref.shape[1]
    dv = gn_ref.shape[1] // N_HEADS
    c_len = SEQ_CHUNK

    @pl.when(pl.program_id(1) == 0)
    def _():
        s_ref[...] = jnp.zeros(s_ref.shape, F32)

    jobs.run(job_src, job_dst, pl.program_id(0) * pl.num_programs(1) + pl.program_id(1))
    i = lax.broadcasted_iota(jnp.int32, (c_len, c_len), 0)
    j = lax.broadcasted_iota(jnp.int32, (c_len, c_len), 1)
    diff = (i - j).astype(F32)
    idx = lax.broadcasted_iota(jnp.int32, (c_len, 1), 0).astype(F32)
    for c in range(tm // c_len):
        rows = slice(c * c_len, (c + 1) * c_len)
        x = x_ref[0, rows, :]
        hn = _rmsnorm(x, g_ref[...]).astype(BF16)
        cos, sin = cos_ref[rows, :], sin_ref[rows, :]
        for h in range(N_HEADS):
            lg = _head_log_decay(h)
            decay = jnp.where(diff >= 0, jnp.exp(jnp.maximum(diff, 0.0) * lg), 0.0)
            cross_w = jnp.exp((idx + 1.0) * lg)
            state_w = jnp.exp((c_len - 1.0 - idx) * lg)
            chunk_decay = math.exp(c_len * lg)
            q, k, v, sg = _proj_head(hn, win_ref, h, cos, sin)
            qc, vc = q.astype(BF16), v.astype(BF16)
            cols = slice(h * dv, (h + 1) * dv)
            s = s_ref[0, h]
            scores = _dot_nt(qc, k.astype(BF16)) * decay
            o = _dot(scores.astype(BF16), vc) + _dot(qc, s.astype(BF16)) * cross_w
            s_ref[0, h] = s * chunk_decay + _dot_tn((k * state_w).astype(BF16), vc)
            act_ref[rows, cols] = _gn_gate(o, sg, gn_ref[:, cols])
        y_ref[0, rows, :] = x + _dot(act_ref[rows, :], wout_ref[...])


def _ret_layer_seq(x, g, win, cos, sin, gn, wout, *, layer, jobs):
    bsz, seq, d = x.shape
    ret_in, vd = win.shape[2], wout.shape[1]
    dk, dv = ret_in // 6 // N_HEADS, vd // N_HEADS
    tm = ROW_TILE
    assert seq % tm == 0 and tm % SEQ_CHUNK == 0 and cos.shape == (seq, dk // 2)
    grid = (bsz, seq // tm)
    job_in, job_out, job_shapes = jobs.specs(grid)
    row = pl.BlockSpec((1, tm, d), lambda b, t: (b, t, 0))
    rope = pl.BlockSpec((tm, dk // 2), lambda b, t: (t, 0))
    vmem = _vmem_limit(
        [((tm, d), F32, 4), ((d, ret_in), BF16, 1), ((vd, d), BF16, 1), ((tm, dk), F32, 4),
         ((N_HEADS, dk, dv), F32, 2), ((tm, vd), BF16, 1)] + jobs.vmem(),
        temporaries=7 * _nbytes((tm, d), F32))
    y, st, *job_res = pl.pallas_call(
        functools.partial(_ret_layer_seq_kernel, jobs=jobs),
        grid=grid,
        in_specs=[row, _const_spec((1, d)), _layer_spec(win, layer), rope, rope, _const_spec((1, vd)),
                  _layer_spec(wout, layer)] + job_in,
        out_specs=[row, pl.BlockSpec((1, N_HEADS, dk, dv), lambda b, t: (b, 0, 0, 0))] + job_out,
        out_shape=[jax.ShapeDtypeStruct((bsz, seq, d), F32),
                   jax.ShapeDtypeStruct((bsz, N_HEADS, dk, dv), F32)] + job_shapes,
        scratch_shapes=[pltpu.VMEM((tm, vd), BF16)],
        compiler_params=pltpu.CompilerParams(dimension_semantics=("arbitrary", "arbitrary"),
                                             vmem_limit_bytes=vmem),
        name="ret_layer_seq",
    )(x, g, win, cos, sin, gn, wout, *jobs.operands())
    return y, st, job_res


def _ret_proj_kernel(x_ref, g_ref, win_ref, inv_ref, q_ref, k_ref, v_ref, sg_ref, *, seq, pos0):
    tm = x_ref.shape[0]
    dk, dv = q_ref.shape[1] // N_HEADS, v_ref.shape[1] // N_HEADS
    hn = _rmsnorm(x_ref[...], g_ref[...]).astype(BF16)
    row = lax.broadcasted_iota(jnp.int32, (tm, dk // 2), 0)
    cos, sin = _rope_angles(inv_ref, pos0 + lax.rem(row, seq))
    for h in range(N_HEADS):
        q, k, v, sg = _proj_head(hn, win_ref, h, cos, sin)
        q_ref[:, h * dk:(h + 1) * dk] = q
        k_ref[:, h * dk:(h + 1) * dk] = k
        v_ref[:, h * dv:(h + 1) * dv] = v
        sg_ref[:, h * dv:(h + 1) * dv] = sg


def _ret_proj(x, g, win, inv, *, layer, seq, pos0):
    m, d = x.shape
    ret_in = win.shape[2]
    qk, vd = ret_in // 6, ret_in // 3
    tm = ROW_TILE
    assert m % tm == 0 and tm % seq == 0
    row = lambda n: pl.BlockSpec((tm, n), lambda i: (i, 0))
    vmem = _vmem_limit(
        [((tm, d), F32, 2), ((d, ret_in), BF16, 1), ((tm, qk), F32, 4), ((tm, vd), F32, 4)],
        temporaries=2 * _nbytes((tm, d), F32))
    return pl.pallas_call(
        functools.partial(_ret_proj_kernel, seq=seq, pos0=pos0),
        grid=(m // tm,),
        in_specs=[row(d), _const_spec((1, d)), _layer_spec(win, layer), _const_spec(inv.shape)],
        out_specs=[row(qk), row(qk), row(vd), row(vd)],
        out_shape=[jax.ShapeDtypeStruct((m, qk), F32), jax.ShapeDtypeStruct((m, qk), F32),
                   jax.ShapeDtypeStruct((m, vd), F32), jax.ShapeDtypeStruct((m, vd), F32)],
        compiler_params=pltpu.CompilerParams(dimension_semantics=("arbitrary",), vmem_limit_bytes=vmem),
        name="ret_proj",
    )(x, g, win, inv)


class _DecodeGroup:
    def __init__(self, q_ref, k_ref, v_ref, rows, seq):
        n = rows.stop - rows.start
        dk, dv = q_ref.shape[1] // N_HEADS, v_ref.shape[1] // N_HEADS
        self.seq = seq
        self.r = lax.broadcasted_iota(jnp.int32, (n, 1), 0)
        t = lax.rem(self.r, seq)
        tf = t.astype(F32)
        self.qb, self.kw, self.vb, self.inner, self.cross, self.cross_w = [], [], [], [], [], []
        for h in range(N_HEADS):
            lg = _head_log_decay(h)
            q = q_ref[rows, h * dk:(h + 1) * dk]
            k = k_ref[rows, h * dk:(h + 1) * dk]
            v = v_ref[rows, h * dv:(h + 1) * dv]
            acc = jnp.zeros((n, dv), F32)
            for dlt in range(seq):
                kd = k if dlt == 0 else pltpu.roll(k, dlt, axis=0)
                vd = v if dlt == 0 else pltpu.roll(v, dlt, axis=0)
                sc = jnp.sum(q * kd, axis=-1, keepdims=True) * math.exp(dlt * lg)
                acc = acc + jnp.where(t >= dlt, sc, 0.0) * vd
            self.inner.append(acc)
            self.cross.append(jnp.zeros((n, dv), F32))
            self.cross_w.append(jnp.exp((tf + 1.0) * lg))
            self.qb.append(q.astype(BF16))
            self.kw.append(k * jnp.exp((seq - 1.0 - tf) * lg))
            self.vb.append(v.astype(BF16))

    def advance(self, b, s0_ref, s_ref):
        own = (self.r >= b * self.seq) & (self.r < (b + 1) * self.seq)
        for h in range(N_HEADS):
            s0 = s0_ref[h]
            self.cross[h] = self.cross[h] + jnp.where(own, _dot(self.qb[h], s0.astype(BF16)), 0.0)
            kwb = jnp.where(own, self.kw[h], 0.0).astype(BF16)
            s_ref[h] = s0 * math.exp(self.seq * _head_log_decay(h)) + _dot_tn(kwb, self.vb[h])

    def output(self, h):
        return self.inner[h] + self.cross[h] * self.cross_w[h]


def _ffn_decode_kernel(x_ref, g_ref, wg_ref, wu_ref, wd_ref, fin_ref, q_ref, k_ref, v_ref, s0_hbm,
                       o_ref, ro_ref, s_hbm, act_ref, sin_ref, sout_ref, sem_in, sem_out,
                       *, final_norm, seq, layer):
    i = pl.program_id(0)
    last = pl.num_programs(0) - 1
    n_sub = x_ref.shape[0] // ROW_TILE
    per_group = STEP_ROWS // seq
    per_step = n_sub * per_group
    n_chunks = wg_ref.shape[1] // FF_CHUNK
    dv = v_ref.shape[1] // N_HEADS
    run_after = {(u + 1) * n_chunks // per_group - 1: u for u in range(per_group)}

    def in_copy(n, slot):
        return pltpu.make_async_copy(s0_hbm.at[layer, n], sin_ref.at[slot], sem_in.at[slot])

    def out_copy(n, slot):
        return pltpu.make_async_copy(sout_ref.at[slot], s_hbm.at[n], sem_out.at[slot])

    lead = IN_SLOTS - 1

    @pl.when(i == 0)
    def _():
        for n in range(lead):
            in_copy(n, n % IN_SLOTS).start()

    def advance(group, sub, b):
        l = sub * per_group + b
        n = i * per_step + l
        slot, oslot = l % IN_SLOTS, l % OUT_SLOTS
        in_copy(n, slot).wait()
        if l + lead < per_step:
            in_copy(n + lead, (l + lead) % IN_SLOTS).start()
        else:
            pl.when(i < last)(lambda: in_copy(n + lead, (l + lead) % IN_SLOTS).start())
        if l >= OUT_SLOTS:
            out_copy(n - OUT_SLOTS, oslot).wait()
        else:
            pl.when(i > 0)(lambda: out_copy(n - OUT_SLOTS, oslot).wait())
        group.advance(b, sin_ref.at[slot], sout_ref.at[oslot])
        out_copy(n, oslot).start()

    for sub in range(n_sub):
        rows16 = slice(sub * STEP_ROWS, (sub + 1) * STEP_ROWS)
        group = _DecodeGroup(q_ref, k_ref, v_ref, rows16, seq)

        def after_chunk(j, group=group, sub=sub):
            if j in run_after:
                advance(group, sub, run_after[j])

        _ffn_rows(x_ref, g_ref, wg_ref, wu_ref, wd_ref, fin_ref, o_ref, act_ref,
                  slice(sub * ROW_TILE, (sub + 1) * ROW_TILE), final_norm=final_norm, after_chunk=after_chunk)
        for h in range(N_HEADS):
            ro_ref[rows16, h * dv:(h + 1) * dv] = group.output(h)

    @pl.when(i == last)
    def _():
        for l in range(per_step - OUT_SLOTS, per_step):
            out_copy(i * per_step + l, l % OUT_SLOTS).wait()


def _ffn_decode(x, g, wg, wu, wd, fin, q, k, v, s0, *, layer, final_norm, state_layer, seq):
    m, d = x.shape
    f = wg.shape[2]
    ms, qk = q.shape
    vd = v.shape[1]
    _, bsz, _, dk, dv = s0.shape
    tm = 2 * ROW_TILE
    steps = m // tm
    rows_s = (tm // ROW_TILE) * STEP_ROWS
    assert m % tm == 0 and f % FF_CHUNK == 0 and f // FF_CHUNK >= STEP_ROWS // seq
    assert seq <= SEQ_CHUNK and STEP_ROWS % seq == 0 and ms == bsz * seq and ms == steps * rows_s
    assert (rows_s // seq) % IN_SLOTS == 0 and (rows_s // seq) % OUT_SLOTS == 0
    row = pl.BlockSpec((tm, d), lambda i: (i, 0))
    srow = lambda n: pl.BlockSpec((rows_s, n), lambda i: (i, 0))
    hbm = pl.BlockSpec(memory_space=pl.ANY)
    state = (N_HEADS, dk, dv)
    vmem = _vmem_limit(
        [((tm, d), F32, 4), ((d, f), BF16, 2), ((f, d), BF16, 1), ((tm, f), BF16, 1),
         (state, F32, IN_SLOTS + OUT_SLOTS),
         ((rows_s, vd), F32, 8)],
        temporaries=2 * _nbytes((tm, d), F32))
    return pl.pallas_call(
        functools.partial(_ffn_decode_kernel, final_norm=final_norm, seq=seq, layer=state_layer),
        grid=(steps,),
        in_specs=[row, _const_spec((1, d)), _layer_spec(wg, layer), _layer_spec(wu, layer),
                  _layer_spec(wd, layer), _const_spec((1, d)), srow(qk), srow(qk), srow(vd), hbm],
        out_specs=[row, srow(vd), hbm],
        out_shape=[jax.ShapeDtypeStruct((m, d), F32), jax.ShapeDtypeStruct((ms, vd), F32),
                   jax.ShapeDtypeStruct(s0.shape[1:], F32)],
        scratch_shapes=[pltpu.VMEM((tm, f), BF16), pltpu.VMEM((IN_SLOTS,) + state, F32),
                        pltpu.VMEM((OUT_SLOTS,) + state, F32), pltpu.SemaphoreType.DMA((IN_SLOTS,)),
                        pltpu.SemaphoreType.DMA((OUT_SLOTS,))],
        compiler_params=pltpu.CompilerParams(dimension_semantics=("arbitrary",), vmem_limit_bytes=vmem),
        name="ffn_decode_ret",
    )(x, g, wg, wu, wd, fin, q, k, v, s0)


def _ret_out_kernel(x_ref, o_ref, sg_ref, gn_ref, wout_ref, y_ref, act_ref):
    dv = o_ref.shape[1] // N_HEADS
    for h in range(N_HEADS):
        cols = slice(h * dv, (h + 1) * dv)
        act_ref[:, cols] = _gn_gate(o_ref[:, cols], sg_ref[:, cols], gn_ref[:, cols])
    y_ref[...] = x_ref[...] + _dot(act_ref[...], wout_ref[...])


def _decode_ret_ffn_kernel(x_ref, o_ref, sg_ref, gn_ref, gffn_ref, fin_ref, rwout_hbm, wg_hbm, wu_hbm, wd_hbm,
                           y_ref, rwout_ref, wg_ref, wu_ref, wd_ref, x1_ref, ract_ref, act_ref, sem,
                           *, ret_layer, ffn_layer, final_norm):
    copies = _copies(
        [(rwout_hbm.at[ret_layer], rwout_ref), (wg_hbm.at[ffn_layer], wg_ref), (wu_hbm.at[ffn_layer], wu_ref),
         (wd_hbm.at[ffn_layer], wd_ref)], sem)
    copies[0].start()
    copies[0].wait()
    for c in copies[1:]:
        c.start()
    _ret_out_kernel(x_ref, o_ref, sg_ref, gn_ref, rwout_ref, x1_ref, ract_ref)
    for c in copies[1:]:
        c.wait()
    _ffn_rows(x1_ref, gffn_ref, wg_ref, wu_ref, wd_ref, fin_ref, y_ref, act_ref, slice(0, x_ref.shape[0]),
              final_norm=final_norm)


def _decode_ret_ffn(x, o, sg, gn, rwout, g_ffn, wg, wu, wd, fin, *, ret_layer, ffn_layer, final_norm):
    m, d = x.shape
    vd, f = o.shape[1], wg.shape[2]
    assert f % FF_CHUNK == 0
    full = lambda n: pl.BlockSpec((m, n), lambda i: (0, 0))
    hbm = pl.BlockSpec(memory_space=pl.ANY)
    weights = [(vd, d), (d, f), (d, f), (f, d)]
    vmem = _vmem_limit([((m, d), F32, 3), ((m, vd), F32, 2), ((m, vd), BF16, 1), ((m, f), BF16, 1)]
                       + [(w, BF16, 1) for w in weights], temporaries=3 * _nbytes((m, d), F32))
    return pl.pallas_call(
        functools.partial(_decode_ret_ffn_kernel, ret_layer=ret_layer, ffn_layer=ffn_layer,
                          final_norm=final_norm),
        grid=(1,),
        in_specs=[full(d), full(vd), full(vd), _const_spec((1, vd)), _const_spec((1, d)), _const_spec((1, d))]
        + [hbm] * len(weights),
        out_specs=full(d),
        out_shape=jax.ShapeDtypeStruct((m, d), F32),
        scratch_shapes=[pltpu.VMEM(w, BF16) for w in weights]
        + [pltpu.VMEM((m, d), F32), pltpu.VMEM((m, vd), BF16), pltpu.VMEM((m, f), BF16),
           pltpu.SemaphoreType.DMA((len(weights),))],
        compiler_params=pltpu.CompilerParams(dimension_semantics=("arbitrary",), vmem_limit_bytes=vmem),
        name="decode_ret_ffn",
    )(x, o, sg, gn, g_ffn, fin, rwout, wg, wu, wd)


def kernel(x_prompt, x_sample, state_conv, state_ret, norm_mix, norm_ffn, conv_w_in, conv_w, conv_w_out,
           ret_w_in, ret_gn, ret_w_out, ffn_w_gate, ffn_w_up, ffn_w_down, final_norm):
    bp, lp, d = x_prompt.shape
    bs, ls, _ = x_sample.shape
    depth = norm_mix.shape[0]
    dk = ret_w_in.shape[2] // 6 // N_HEADS
    inv = (ROPE_BASE ** (-jnp.arange(dk // 2, dtype=F32) / (dk // 2)))[None, :]
    fin = final_norm[None, :]
    conv_w_in, conv_w_out = conv_w_in.astype(BF16), conv_w_out.astype(BF16)
    assert depth % N_MIXERS == 0

    xp = x_prompt
    xs = x_sample.reshape(bs * ls, d)
    conv_p, conv_s, ret_p, ret_s = [], [], [], []
    for j in range(depth // N_MIXERS):
        ic, ir = N_MIXERS * j, N_MIXERS * j + 1
        last = ir == depth - 1
        g_ffn_c, g_ffn_r = norm_ffn[ic][None, :], norm_ffn[ir][None, :]
        gn = ret_gn[j][None, :]

        jobs = _CastJobs([(ffn_w_gate, ic), (ffn_w_up, ic), (ffn_w_down, ic), (ret_w_in, j), (ret_w_out, j)])
        xp, st, (wg, wu, wd, rwin, rwout) = _conv_mixer_seq(
            xp, norm_mix[ic][None, :], conv_w_in, conv_w, conv_w_out, layer=j, jobs=jobs)
        conv_p.append(st)
        xs, st, cos, sin = _decode_conv_ffn(
            xs, state_conv[j], norm_mix[ic][None, :], conv_w_in, conv_w, conv_w_out, g_ffn_c, wg, wu, wd, fin, inv,
            conv_layer=j, ffn_layer=0, seq=ls, final_norm=False, rope_len=lp)
        conv_s.append(st)
        q, k, v, sg = _ret_proj(xs, norm_mix[ir][None, :], rwin, inv, layer=0, seq=ls, pos0=PAST_LEN)
        xp, ret_o, st = _ffn_decode(xp.reshape(bp * lp, d), g_ffn_c, wg, wu, wd, fin, q, k, v, state_ret,
                                    layer=0, final_norm=False, state_layer=j, seq=ls)
        ret_s.append(st)

        jobs = _CastJobs([(ffn_w_gate, ir), (ffn_w_up, ir), (ffn_w_down, ir)])
        xp, st, (wg, wu, wd) = _ret_layer_seq(xp.reshape(bp, lp, d), norm_mix[ir][None, :], rwin, cos, sin, gn,
                                              rwout, layer=0, jobs=jobs)
        ret_p.append(st)
        xs = _decode_ret_ffn(xs, ret_o, sg, gn, rwout, g_ffn_r, wg, wu, wd, fin, ret_layer=0, ffn_layer=0,
                             final_norm=last)
        xp = _ffn(xp.reshape(bp * lp, d), g_ffn_r, wg, wu, wd, fin, layer=0, final_norm=last).reshape(bp, lp, d)
    return (xp, xs.reshape(bs, ls, d), jnp.stack(conv_p), jnp.stack(conv_s), jnp.stack(ret_p),
            jnp.stack(ret_s))
```

```python
import functools
import math

import jax
import jax.numpy as jnp
from jax import lax
from jax.experimental import pallas as pl
from jax.experimental.pallas import tpu as pltpu

F32 = jnp.float32
BF16 = jnp.bfloat16

N_MIXERS = 2
CONV_WIDTH = 3
N_HEADS = 4
PAST_LEN = 16384
RMS_EPS = 1e-6
GN_EPS = 1e-6
ROPE_BASE = 10000.0

V7X_VMEM_LIMIT_CAP = 60000 * 1024
V7X_MXU_DIM = 256
SUBLANES = 8
ROW_TILE = 512
FF_CHUNK = V7X_MXU_DIM
CONV_CHUNK = V7X_MXU_DIM
SEQ_CHUNK = V7X_MXU_DIM
BF16_ROWS = 16
STEP_ROWS = BF16_ROWS
CAST_CHUNKS = 16
IN_SLOTS, OUT_SLOTS = 4, 2
STATE_DMA_PRIORITY = 1


def _nbytes(shape, dtype):
    return math.prod(shape) * jnp.dtype(dtype).itemsize


def _vmem_limit(buffers, temporaries):
    need = sum(_nbytes(s, d) * n for s, d, n in buffers) + temporaries
    return min(int(need * 1.25) + (4 << 20), V7X_VMEM_LIMIT_CAP)


def _const_spec(shape):
    return pl.BlockSpec(shape, lambda *_: (0,) * len(shape), pipeline_mode=pl.Buffered(1))


def _layer_spec(w, layer):
    return pl.BlockSpec((None,) + w.shape[1:], lambda *_: (layer, 0, 0), pipeline_mode=pl.Buffered(1))


class _CastJobs:
    def __init__(self, weights):
        self.weights = list(weights)
        self.n = len(self.weights)

    @staticmethod
    def chunk(grid_idx, grid):
        step = grid_idx[0]
        for extent, idx in zip(grid[1:], grid_idx[1:]):
            step = step * extent + idx
        return jnp.minimum(step, CAST_CHUNKS - 1)

    def specs(self, grid):
        assert math.prod(grid) >= CAST_CHUNKS
        ins, outs, shapes = [], [], []
        for w, layer in self.weights:
            rows, cols = w.shape[1:]
            assert rows % (CAST_CHUNKS * BF16_ROWS) == 0
            blk = (None, rows // CAST_CHUNKS, cols)
            ins.append(pl.BlockSpec(blk, lambda *g, layer=layer: (layer, self.chunk(g, grid), 0)))
            outs.append(pl.BlockSpec(blk, lambda *g: (0, self.chunk(g, grid), 0)))
            shapes.append(jax.ShapeDtypeStruct((1, rows, cols), BF16))
        return ins, outs, shapes

    def operands(self):
        return [w for w, _ in self.weights]

    def vmem(self):
        return [((w.shape[1] // CAST_CHUNKS, w.shape[2]), F32, 3) for w, _ in self.weights]

    def run(self, src_refs, dst_refs, step):
        @pl.when(step < CAST_CHUNKS)
        def _():
            for src, dst in zip(src_refs, dst_refs):
                dst[...] = src[...].astype(BF16)


def _rmsnorm(x, g):
    ms = jnp.mean(x * x, axis=-1, keepdims=True)
    return (x * lax.rsqrt(ms + RMS_EPS)) * g


def _dot(a, b):
    return jnp.dot(a, b, preferred_element_type=F32)


def _dot_nt(a, b):
    return lax.dot_general(a, b, (((1,), (1,)), ((), ())), preferred_element_type=F32)


def _dot_tn(a, b):
    return lax.dot_general(a, b, (((0,), (0,)), ((), ())), preferred_element_type=F32)


def _head_log_decay(h):
    return math.log(1.0 - 2.0 ** (-5.0 - h))


def _ffn_rows(x_ref, g_ref, wg_ref, wu_ref, wd_ref, fin_ref, o_ref, act_ref, rows, *, final_norm,
              after_chunk=None):
    x = x_ref[rows, :]
    n = _rmsnorm(x, g_ref[...]).astype(BF16)
    for j in range(wg_ref.shape[1] // FF_CHUNK):
        cols = slice(j * FF_CHUNK, (j + 1) * FF_CHUNK)
        gate = _dot(n, wg_ref[:, cols])
        up = _dot(n, wu_ref[:, cols])
        act_ref[rows, cols] = (jax.nn.silu(gate) * up).astype(BF16)
        if after_chunk is not None:
            after_chunk(j)
    y = x + _dot(act_ref[rows, :], wd_ref[...])
    if final_norm:
        y = _rmsnorm(y, fin_ref[...])
    o_ref[rows, :] = y


def _ffn_kernel(x_ref, g_ref, wg_ref, wu_ref, wd_ref, fin_ref, o_ref, act_ref, *, final_norm):
    for r0 in range(0, x_ref.shape[0], ROW_TILE):
        _ffn_rows(x_ref, g_ref, wg_ref, wu_ref, wd_ref, fin_ref, o_ref, act_ref, slice(r0, r0 + ROW_TILE),
                  final_norm=final_norm)


def _ffn(x, g, wg, wu, wd, fin, *, layer, final_norm):
    m, d = x.shape
    f = wg.shape[2]
    tm = min(2 * ROW_TILE, m)
    assert m % tm == 0 and tm % ROW_TILE == 0 and f % FF_CHUNK == 0
    row = pl.BlockSpec((tm, d), lambda i: (i, 0))
    vmem = _vmem_limit(
        [((tm, d), F32, 4), ((d, f), BF16, 2), ((f, d), BF16, 1), ((tm, f), BF16, 1)],
        temporaries=6 * _nbytes((tm, d), F32))
    return pl.pallas_call(
        functools.partial(_ffn_kernel, final_norm=final_norm),
        grid=(m // tm,),
        in_specs=[row, _const_spec((1, d)), _layer_spec(wg, layer), _layer_spec(wu, layer),
                  _layer_spec(wd, layer), _const_spec((1, d))],
        out_specs=row,
        out_shape=jax.ShapeDtypeStruct((m, d), F32),
        scratch_shapes=[pltpu.VMEM((tm, f), BF16)],
        compiler_params=pltpu.CompilerParams(dimension_semantics=("arbitrary",), vmem_limit_bytes=vmem),
        name="ffn_final" if final_norm else "ffn",
    )(x, g, wg, wu, wd, fin)


def _conv_front(x, g_ref, win_ref):
    d = x.shape[1]
    hn = _rmsnorm(x, g_ref[...]).astype(BF16)
    b = _dot(hn, win_ref[:, 0:d])
    c = _dot(hn, win_ref[:, d:2 * d])
    h = _dot(hn, win_ref[:, 2 * d:3 * d])
    return b, c * h


def _conv_seq_kernel(x_ref, g_ref, win_ref, wc_ref, wout_ref, *refs, jobs):
    job_src, (o_ref, st_ref), job_dst, (ubuf_ref, act_ref) = (
        refs[:jobs.n], refs[jobs.n:jobs.n + 2], refs[jobs.n + 2:-2], refs[-2:])
    tm, d = x_ref.shape[1:]

    @pl.when(pl.program_id(1) == 0)
    def _():
        ubuf_ref[0:SUBLANES, :] = jnp.zeros((SUBLANES, d), F32)

    jobs.run(job_src, job_dst, pl.program_id(0) * pl.num_programs(1) + pl.program_id(1))
    for r0 in range(0, tm, ROW_TILE):
        rows = slice(r0, r0 + ROW_TILE)
        x = x_ref[0, rows, :]
        hn = _rmsnorm(x, g_ref[...]).astype(BF16)
        for c0 in range(0, d, CONV_CHUNK):
            cols = slice(c0, c0 + CONV_CHUNK)
            b = _dot(hn, win_ref[:, c0:c0 + CONV_CHUNK])
            u = (_dot(hn, win_ref[:, d + c0:d + c0 + CONV_CHUNK])
                 * _dot(hn, win_ref[:, 2 * d + c0:2 * d + c0 + CONV_CHUNK]))
            ubuf_ref[SUBLANES + r0:SUBLANES + r0 + ROW_TILE, cols] = u
            y = (wc_ref[0:1, cols] * ubuf_ref[SUBLANES - 2 + r0:SUBLANES - 2 + r0 + ROW_TILE, cols]
                 + wc_ref[1:2, cols] * ubuf_ref[SUBLANES - 1 + r0:SUBLANES - 1 + r0 + ROW_TILE, cols]
                 + wc_ref[2:3, cols] * u)
            act_ref[rows, cols] = (b * y).astype(BF16)
        o_ref[0, rows, :] = x + _dot(act_ref[rows, :], wout_ref[...])
    st_ref[0] = ubuf_ref[SUBLANES + tm - (CONV_WIDTH - 1):SUBLANES + tm, :]
    ubuf_ref[0:SUBLANES, :] = ubuf_ref[tm:tm + SUBLANES, :]


def _conv_mixer_seq(x, g, win, wc, wout, *, layer, jobs):
    bsz, seq, d = x.shape
    tm = 2 * ROW_TILE
    assert seq % tm == 0 and wc.shape[1] == CONV_WIDTH == 3
    grid = (bsz, seq // tm)
    job_in, job_out, job_shapes = jobs.specs(grid)
    row = pl.BlockSpec((1, tm, d), lambda b, t: (b, t, 0))
    vmem = _vmem_limit(
        [((tm, d), F32, 4), ((d, 3 * d), BF16, 1), ((d, d), BF16, 1), ((tm + SUBLANES, d), F32, 1)]
        + jobs.vmem(), temporaries=5 * _nbytes((tm, d), F32))
    y, st, *job_res = pl.pallas_call(
        functools.partial(_conv_seq_kernel, jobs=jobs),
        grid=grid,
        in_specs=[row, _const_spec((1, d)), _layer_spec(win, layer), _layer_spec(wc, layer),
                  _layer_spec(wout, layer)] + job_in,
        out_specs=[row, pl.BlockSpec((1, CONV_WIDTH - 1, d), lambda b, t: (b, 0, 0))] + job_out,
        out_shape=[jax.ShapeDtypeStruct((bsz, seq, d), F32),
                   jax.ShapeDtypeStruct((bsz, CONV_WIDTH - 1, d), F32)] + job_shapes,
        scratch_shapes=[pltpu.VMEM((tm + SUBLANES, d), F32), pltpu.VMEM((tm, d), BF16)],
        compiler_params=pltpu.CompilerParams(dimension_semantics=("arbitrary", "arbitrary"),
                                             vmem_limit_bytes=vmem),
        name="conv_mixer_seq",
    )(x, g, win, wc, wout, *jobs.operands())
    return y, st, job_res


def _conv_step_kernel(x_ref, pre_ref, g_ref, win_ref, wc_ref, wout_ref, o_ref, u_ref, *, seq):
    x = x_ref[...]
    m = x.shape[0]
    b, u = _conv_front(x, g_ref, win_ref)
    pre = pre_ref[...]
    t = lax.rem(lax.broadcasted_iota(jnp.int32, (m, 1), 0), seq)
    back1 = jnp.where(t >= 1, pltpu.roll(u, 1, axis=0), pltpu.roll(pre, m - 1, axis=0))
    back2 = jnp.where(t >= 2, pltpu.roll(u, 2, axis=0), pre)
    y = wc_ref[0:1, :] * back2 + wc_ref[1:2, :] * back1 + wc_ref[2:3, :] * u
    o_ref[...] = x + _dot((b * y).astype(BF16), wout_ref[...])
    u_ref[...] = u


def _copies(pairs, sem):
    return [pltpu.make_async_copy(src, dst, sem.at[n]) for n, (src, dst) in enumerate(pairs)]


def _decode_conv_ffn_kernel(x_ref, pre_ref, gmix_ref, wc_ref, gffn_ref, fin_ref, inv_ref,
                            win_hbm, wout_hbm, wg_hbm, wu_hbm, wd_hbm, u_ref, y_ref, cos_ref, sin_ref,
                            win_ref, wout_ref, wg_ref, wu_ref, wd_ref, x1_ref, act_ref, sem,
                            *, seq, conv_layer, ffn_layer, final_norm):
    mixer_w, ffn_w = 2, 3
    copies = _copies(
        [(win_hbm.at[conv_layer], win_ref), (wout_hbm.at[conv_layer], wout_ref), (wg_hbm.at[ffn_layer], wg_ref),
         (wu_hbm.at[ffn_layer], wu_ref), (wd_hbm.at[ffn_layer], wd_ref)], sem)
    for c in copies[:mixer_w]:
        c.start()
    pos = lax.broadcasted_iota(jnp.int32, cos_ref.shape, 0)
    cos_ref[...], sin_ref[...] = _rope_angles(inv_ref, pos)
    for c in copies[:mixer_w]:
        c.wait()
    for c in copies[mixer_w:mixer_w + ffn_w]:
        c.start()
    _conv_step_kernel(x_ref, pre_ref, gmix_ref, win_ref, wc_ref, wout_ref, x1_ref, u_ref, seq=seq)
    for c in copies[mixer_w:mixer_w + ffn_w]:
        c.wait()
    _ffn_rows(x1_ref, gffn_ref, wg_ref, wu_ref, wd_ref, fin_ref, y_ref, act_ref, slice(0, x_ref.shape[0]),
              final_norm=final_norm)


def _decode_conv_ffn(x, buf, g_mix, win, wc, wout, g_ffn, wg, wu, wd, fin, inv, *, conv_layer, ffn_layer, seq,
                     final_norm, rope_len):
    m, d = x.shape
    f = wg.shape[2]
    taps = CONV_WIDTH - 1
    assert wc.shape[1] == CONV_WIDTH == 3 and seq >= taps and m % seq == 0 and f % FF_CHUNK == 0
    pre = jnp.pad(buf, ((0, 0), (0, seq - taps), (0, 0))).reshape(m, d)
    full = pl.BlockSpec((m, d), lambda i: (0, 0))
    table = pl.BlockSpec((rope_len, inv.shape[1]), lambda i: (0, 0))
    hbm = pl.BlockSpec(memory_space=pl.ANY)
    weights = [(d, 3 * d), (d, d), (d, f), (d, f), (f, d)]
    vmem = _vmem_limit([((m, d), F32, 5), ((m, f), BF16, 1), ((rope_len, inv.shape[1]), F32, 2)]
                       + [(w, BF16, 1) for w in weights], temporaries=12 * _nbytes((m, d), F32))
    u, y, cos, sin = pl.pallas_call(
        functools.partial(_decode_conv_ffn_kernel, seq=seq, conv_layer=conv_layer, ffn_layer=ffn_layer,
                          final_norm=final_norm),
        grid=(1,),
        in_specs=[full, full, _const_spec((1, d)), _layer_spec(wc, conv_layer), _const_spec((1, d)),
                  _const_spec((1, d)), _const_spec(inv.shape)] + [hbm] * len(weights),
        out_specs=[full, full, table, table],
        out_shape=[jax.ShapeDtypeStruct((m, d), F32)] * 2
        + [jax.ShapeDtypeStruct((rope_len, inv.shape[1]), F32)] * 2,
        scratch_shapes=[pltpu.VMEM(w, BF16) for w in weights]
        + [pltpu.VMEM((m, d), F32), pltpu.VMEM((m, f), BF16), pltpu.SemaphoreType.DMA((len(weights),))],
        compiler_params=pltpu.CompilerParams(dimension_semantics=("arbitrary",), vmem_limit_bytes=vmem),
        name="decode_conv_ffn",
    )(x, pre, g_mix, wc, g_ffn, fin, inv, win, wout, wg, wu, wd)
    return y, u.reshape(m // seq, seq, d)[:, seq - taps:, :], cos, sin


def _rope_angles(inv_ref, pos):
    ang = pos.astype(F32) * inv_ref[...]
    return jnp.cos(ang), jnp.sin(ang)


def _rotate(z, cos, sin):
    half = z.shape[1] // 2
    x1, x2 = z[:, :half], z[:, half:]
    return jnp.concatenate([x1 * cos - x2 * sin, x1 * sin + x2 * cos], axis=1)


def _proj_head(hn, win_ref, h, cos, sin):
    ret_in = win_ref.shape[1]
    qk, vd = ret_in // 6, ret_in // 3
    dk, dv = qk // N_HEADS, vd // N_HEADS
    q = _rotate(_dot(hn, win_ref[:, h * dk:(h + 1) * dk]), cos, sin)
    k = _rotate(_dot(hn, win_ref[:, qk + h * dk:qk + (h + 1) * dk]), cos, sin) * dk ** -0.5
    v = _dot(hn, win_ref[:, 2 * qk + h * dv:2 * qk + (h + 1) * dv])
    sg = jax.nn.silu(_dot(hn, win_ref[:, 2 * qk + vd + h * dv:2 * qk + vd + (h + 1) * dv]))
    return q, k, v, sg


def _gn_gate(o, sg, gn):
    mu = jnp.mean(o, axis=-1, keepdims=True)
    var = jnp.mean(jnp.square(o - mu), axis=-1, keepdims=True)
    of = ((o - mu) * lax.rsqrt(var + GN_EPS)) * gn
    return (sg * of).astype(BF16)


def _ret_layer_seq_kernel(x_ref, g_ref, win_ref, cos_ref, sin_ref, gn_ref, wout_ref, *refs, jobs):
    job_src, (y_ref, s_ref), job_dst, act_ref = (
        refs[:jobs.n], refs[jobs.n:jobs.n + 2], refs[jobs.n + 2:-1], refs[-1])
    tm = x_ref.shape[1]
    dv = gn_ref.shape[1] // N_HEADS
    c_len = SEQ_CHUNK

    @pl.when(pl.program_id(1) == 0)
    def _():
        s_ref[...] = jnp.zeros(s_ref.shape, F32)

    jobs.run(job_src, job_dst, pl.program_id(0) * pl.num_programs(1) + pl.program_id(1))
    i = lax.broadcasted_iota(jnp.int32, (c_len, c_len), 0)
    j = lax.broadcasted_iota(jnp.int32, (c_len, c_len), 1)
    diff = (i - j).astype(F32)
    idx = lax.broadcasted_iota(jnp.int32, (c_len, 1), 0).astype(F32)
    for c in range(tm // c_len):
        rows = slice(c * c_len, (c + 1) * c_len)
        x = x_ref[0, rows, :]
        hn = _rmsnorm(x, g_ref[...]).astype(BF16)
        cos, sin = cos_ref[rows, :], sin_ref[rows, :]
        for h in range(N_HEADS):
            lg = _head_log_decay(h)
            decay = jnp.where(diff >= 0, jnp.exp(jnp.maximum(diff, 0.0) * lg), 0.0)
            cross_w = jnp.exp((idx + 1.0) * lg)
            state_w = jnp.exp((c_len - 1.0 - idx) * lg)
            chunk_decay = math.exp(c_len * lg)
            q, k, v, sg = _proj_head(hn, win_ref, h, cos, sin)
            qc, vc = q.astype(BF16), v.astype(BF16)
            cols = slice(h * dv, (h + 1) * dv)
            s = s_ref[0, h]
            scores = _dot_nt(qc, k.astype(BF16)) * decay
            o = _dot(scores.astype(BF16), vc) + _dot(qc, s.astype(BF16)) * cross_w
            s_ref[0, h] = s * chunk_decay + _dot_tn((k * state_w).astype(BF16), vc)
            act_ref[rows, cols] = _gn_gate(o, sg, gn_ref[:, cols])
        y_ref[0, rows, :] = x + _dot(act_ref[rows, :], wout_ref[...])


def _ret_layer_seq(x, g, win, cos, sin, gn, wout, *, layer, jobs):
    bsz, seq, d = x.shape
    ret_in, vd = win.shape[2], wout.shape[1]
    dk, dv = ret_in // 6 // N_HEADS, vd // N_HEADS
    tm = ROW_TILE
    assert seq % tm == 0 and tm % SEQ_CHUNK == 0 and cos.shape == (seq, dk // 2)
    grid = (bsz, seq // tm)
    job_in, job_out, job_shapes = jobs.specs(grid)
    row = pl.BlockSpec((1, tm, d), lambda b, t: (b, t, 0))
    rope = pl.BlockSpec((tm, dk // 2), lambda b, t: (t, 0))
    vmem = _vmem_limit(
        [((tm, d), F32, 4), ((d, ret_in), BF16, 1), ((vd, d), BF16, 1), ((tm, dk), F32, 4),
         ((N_HEADS, dk, dv), F32, 2), ((tm, vd), BF16, 1)] + jobs.vmem(),
        temporaries=8 * _nbytes((tm, d), F32))
    y, st, *job_res = pl.pallas_call(
        functools.partial(_ret_layer_seq_kernel, jobs=jobs),
        grid=grid,
        in_specs=[row, _const_spec((1, d)), _layer_spec(win, layer), rope, rope, _const_spec((1, vd)),
                  _layer_spec(wout, layer)] + job_in,
        out_specs=[row, pl.BlockSpec((1, N_HEADS, dk, dv), lambda b, t: (b, 0, 0, 0))] + job_out,
        out_shape=[jax.ShapeDtypeStruct((bsz, seq, d), F32),
                   jax.ShapeDtypeStruct((bsz, N_HEADS, dk, dv), F32)] + job_shapes,
        scratch_shapes=[pltpu.VMEM((tm, vd), BF16)],
        compiler_params=pltpu.CompilerParams(dimension_semantics=("arbitrary", "arbitrary"),
                                             vmem_limit_bytes=vmem),
        name="ret_layer_seq",
    )(x, g, win, cos, sin, gn, wout, *jobs.operands())
    return y, st, job_res


def _ret_proj_kernel(x_ref, g_ref, win_ref, inv_ref, q_ref, k_ref, v_ref, sg_ref, *, seq, pos0):
    tm = x_ref.shape[0]
    dk, dv = q_ref.shape[1] // N_HEADS, v_ref.shape[1] // N_HEADS
    hn = _rmsnorm(x_ref[...], g_ref[...]).astype(BF16)
    row = lax.broadcasted_iota(jnp.int32, (tm, dk // 2), 0)
    cos, sin = _rope_angles(inv_ref, pos0 + lax.rem(row, seq))
    for h in range(N_HEADS):
        q, k, v, sg = _proj_head(hn, win_ref, h, cos, sin)
        q_ref[:, h * dk:(h + 1) * dk] = q
        k_ref[:, h * dk:(h + 1) * dk] = k
        v_ref[:, h * dv:(h + 1) * dv] = v
        sg_ref[:, h * dv:(h + 1) * dv] = sg


def _ret_proj(x, g, win, inv, *, layer, seq, pos0):
    m, d = x.shape
    ret_in = win.shape[2]
    qk, vd = ret_in // 6, ret_in // 3
    tm = ROW_TILE
    assert m % tm == 0 and tm % seq == 0
    row = lambda n: pl.BlockSpec((tm, n), lambda i: (i, 0))
    vmem = _vmem_limit(
        [((tm, d), F32, 2), ((d, ret_in), BF16, 1), ((tm, qk), F32, 4), ((tm, vd), F32, 4)],
        temporaries=8 * _nbytes((tm, d), F32))
    return pl.pallas_call(
        functools.partial(_ret_proj_kernel, seq=seq, pos0=pos0),
        grid=(m // tm,),
        in_specs=[row(d), _const_spec((1, d)), _layer_spec(win, layer), _const_spec(inv.shape)],
        out_specs=[row(qk), row(qk), row(vd), row(vd)],
        out_shape=[jax.ShapeDtypeStruct((m, qk), F32), jax.ShapeDtypeStruct((m, qk), F32),
                   jax.ShapeDtypeStruct((m, vd), F32), jax.ShapeDtypeStruct((m, vd), F32)],
        compiler_params=pltpu.CompilerParams(dimension_semantics=("arbitrary",), vmem_limit_bytes=vmem),
        name="ret_proj",
    )(x, g, win, inv)


class _DecodeGroup:
    def __init__(self, q_ref, k_ref, v_ref, rows, seq):
        n = rows.stop - rows.start
        dk, dv = q_ref.shape[1] // N_HEADS, v_ref.shape[1] // N_HEADS
        self.seq = seq
        self.r = lax.broadcasted_iota(jnp.int32, (n, 1), 0)
        t = lax.rem(self.r, seq)
        tf = t.astype(F32)
        self.qb, self.kw, self.vb, self.inner, self.cross, self.cross_w = [], [], [], [], [], []
        for h in range(N_HEADS):
            lg = _head_log_decay(h)
            q = q_ref[rows, h * dk:(h + 1) * dk]
            k = k_ref[rows, h * dk:(h + 1) * dk]
            v = v_ref[rows, h * dv:(h + 1) * dv]
            acc = jnp.zeros((n, dv), F32)
            for dlt in range(seq):
                kd = k if dlt == 0 else pltpu.roll(k, dlt, axis=0)
                vd = v if dlt == 0 else pltpu.roll(v, dlt, axis=0)
                sc = jnp.sum(q * kd, axis=-1, keepdims=True) * math.exp(dlt * lg)
                acc = acc + jnp.where(t >= dlt, sc, 0.0) * vd
            self.inner.append(acc)
            self.cross.append(jnp.zeros((n, dv), F32))
            self.cross_w.append(jnp.exp((tf + 1.0) * lg))
            self.qb.append(q.astype(BF16))
            self.kw.append(k * jnp.exp((seq - 1.0 - tf) * lg))
            self.vb.append(v.astype(BF16))

    def advance(self, b, s0_ref, s_ref):
        own = (self.r >= b * self.seq) & (self.r < (b + 1) * self.seq)
        for h in range(N_HEADS):
            s0 = s0_ref[h]
            self.cross[h] = self.cross[h] + jnp.where(own, _dot(self.qb[h], s0.astype(BF16)), 0.0)
            kwb = jnp.where(own, self.kw[h], 0.0).astype(BF16)
            s_ref[h] = s0 * math.exp(self.seq * _head_log_decay(h)) + _dot_tn(kwb, self.vb[h])

    def output(self, h):
        return self.inner[h] + self.cross[h] * self.cross_w[h]


def _ffn_decode_kernel(x_ref, g_ref, wg_ref, wu_ref, wd_ref, fin_ref, q_ref, k_ref, v_ref, s0_hbm,
                       o_ref, ro_ref, s_hbm, act_ref, sin_ref, sout_ref, sem_in, sem_out,
                       *, final_norm, seq, layer):
    i = pl.program_id(0)
    last = pl.num_programs(0) - 1
    n_sub = x_ref.shape[0] // ROW_TILE
    per_group = STEP_ROWS // seq
    per_step = n_sub * per_group
    n_chunks = wg_ref.shape[1] // FF_CHUNK
    dv = v_ref.shape[1] // N_HEADS
    run_after = {(u + 1) * n_chunks // per_group - 1: u for u in range(per_group)}

    def in_copy(n, slot):
        return pltpu.make_async_copy(s0_hbm.at[layer, n], sin_ref.at[slot], sem_in.at[slot])

    def out_copy(n, slot):
        return pltpu.make_async_copy(sout_ref.at[slot], s_hbm.at[n], sem_out.at[slot])

    lead = IN_SLOTS - 1

    @pl.when(i == 0)
    def _():
        for n in range(lead):
            in_copy(n, n % IN_SLOTS).start(priority=STATE_DMA_PRIORITY)

    def advance(group, sub, b):
        l = sub * per_group + b
        n = i * per_step + l
        slot, oslot = l % IN_SLOTS, l % OUT_SLOTS
        in_copy(n, slot).wait()
        if l + lead < per_step:
            in_copy(n + lead, (l + lead) % IN_SLOTS).start(priority=STATE_DMA_PRIORITY)
        else:
            pl.when(i < last)(
                lambda: in_copy(n + lead, (l + lead) % IN_SLOTS).start(priority=STATE_DMA_PRIORITY))
        if l >= OUT_SLOTS:
            out_copy(n - OUT_SLOTS, oslot).wait()
        else:
            pl.when(i > 0)(lambda: out_copy(n - OUT_SLOTS, oslot).wait())
        group.advance(b, sin_ref.at[slot], sout_ref.at[oslot])
        out_copy(n, oslot).start(priority=STATE_DMA_PRIORITY)

    for sub in range(n_sub):
        rows16 = slice(sub * STEP_ROWS, (sub + 1) * STEP_ROWS)
        group = _DecodeGroup(q_ref, k_ref, v_ref, rows16, seq)

        def after_chunk(j, group=group, sub=sub):
            if j in run_after:
                advance(group, sub, run_after[j])

        _ffn_rows(x_ref, g_ref, wg_ref, wu_ref, wd_ref, fin_ref, o_ref, act_ref,
                  slice(sub * ROW_TILE, (sub + 1) * ROW_TILE), final_norm=final_norm, after_chunk=after_chunk)
        for h in range(N_HEADS):
            ro_ref[rows16, h * dv:(h + 1) * dv] = group.output(h)

    @pl.when(i == last)
    def _():
        for l in range(per_step - OUT_SLOTS, per_step):
            out_copy(i * per_step + l, l % OUT_SLOTS).wait()


def _ffn_decode(x, g, wg, wu, wd, fin, q, k, v, s0, *, layer, final_norm, state_layer, seq):
    m, d = x.shape
    f = wg.shape[2]
    ms, qk = q.shape
    vd = v.shape[1]
    _, bsz, _, dk, dv = s0.shape
    tm = 2 * ROW_TILE
    steps = m // tm
    rows_s = (tm // ROW_TILE) * STEP_ROWS
    assert m % tm == 0 and f % FF_CHUNK == 0 and f // FF_CHUNK >= STEP_ROWS // seq
    assert seq <= SEQ_CHUNK and STEP_ROWS % seq == 0 and ms == bsz * seq and ms == steps * rows_s
    assert (rows_s // seq) % IN_SLOTS == 0 and (rows_s // seq) % OUT_SLOTS == 0
    row = pl.BlockSpec((tm, d), lambda i: (i, 0))
    srow = lambda n: pl.BlockSpec((rows_s, n), lambda i: (i, 0))
    hbm = pl.BlockSpec(memory_space=pl.ANY)
    state = (N_HEADS, dk, dv)
    vmem = _vmem_limit(
        [((tm, d), F32, 4), ((d, f), BF16, 2), ((f, d), BF16, 1), ((tm, f), BF16, 1),
         (state, F32, IN_SLOTS + OUT_SLOTS),
         ((rows_s, vd), F32, 8)],
        temporaries=8 * _nbytes((tm, d), F32))
    return pl.pallas_call(
        functools.partial(_ffn_decode_kernel, final_norm=final_norm, seq=seq, layer=state_layer),
        grid=(steps,),
        in_specs=[row, _const_spec((1, d)), _layer_spec(wg, layer), _layer_spec(wu, layer),
                  _layer_spec(wd, layer), _const_spec((1, d)), srow(qk), srow(qk), srow(vd), hbm],
        out_specs=[row, srow(vd), hbm],
        out_shape=[jax.ShapeDtypeStruct((m, d), F32), jax.ShapeDtypeStruct((ms, vd), F32),
                   jax.ShapeDtypeStruct(s0.shape[1:], F32)],
        scratch_shapes=[pltpu.VMEM((tm, f), BF16), pltpu.VMEM((IN_SLOTS,) + state, F32),
                        pltpu.VMEM((OUT_SLOTS,) + state, F32), pltpu.SemaphoreType.DMA((IN_SLOTS,)),
                        pltpu.SemaphoreType.DMA((OUT_SLOTS,))],
        compiler_params=pltpu.CompilerParams(dimension_semantics=("arbitrary",), vmem_limit_bytes=vmem),
        name="ffn_decode_ret",
    )(x, g, wg, wu, wd, fin, q, k, v, s0)


def _ret_out_kernel(x_ref, o_ref, sg_ref, gn_ref, wout_ref, y_ref, act_ref):
    dv = o_ref.shape[1] // N_HEADS
    for h in range(N_HEADS):
        cols = slice(h * dv, (h + 1) * dv)
        act_ref[:, cols] = _gn_gate(o_ref[:, cols], sg_ref[:, cols], gn_ref[:, cols])
    y_ref[...] = x_ref[...] + _dot(act_ref[...], wout_ref[...])


def _decode_ret_ffn_kernel(x_ref, o_ref, sg_ref, gn_ref, gffn_ref, fin_ref, rwout_hbm, wg_hbm, wu_hbm, wd_hbm,
                           y_ref, rwout_ref, wg_ref, wu_ref, wd_ref, x1_ref, ract_ref, act_ref, sem,
                           *, ret_layer, ffn_layer, final_norm):
    copies = _copies(
        [(rwout_hbm.at[ret_layer], rwout_ref), (wg_hbm.at[ffn_layer], wg_ref), (wu_hbm.at[ffn_layer], wu_ref),
         (wd_hbm.at[ffn_layer], wd_ref)], sem)
    copies[0].start()
    copies[0].wait()
    for c in copies[1:]:
        c.start()
    _ret_out_kernel(x_ref, o_ref, sg_ref, gn_ref, rwout_ref, x1_ref, ract_ref)
    for c in copies[1:]:
        c.wait()
    _ffn_rows(x1_ref, gffn_ref, wg_ref, wu_ref, wd_ref, fin_ref, y_ref, act_ref, slice(0, x_ref.shape[0]),
              final_norm=final_norm)


def _decode_ret_ffn(x, o, sg, gn, rwout, g_ffn, wg, wu, wd, fin, *, ret_layer, ffn_layer, final_norm):
    m, d = x.shape
    vd, f = o.shape[1], wg.shape[2]
    assert f % FF_CHUNK == 0
    full = lambda n: pl.BlockSpec((m, n), lambda i: (0, 0))
    hbm = pl.BlockSpec(memory_space=pl.ANY)
    weights = [(vd, d), (d, f), (d, f), (f, d)]
    vmem = _vmem_limit([((m, d), F32, 3), ((m, vd), F32, 2), ((m, vd), BF16, 1), ((m, f), BF16, 1)]
                       + [(w, BF16, 1) for w in weights], temporaries=8 * _nbytes((m, d), F32))
    return pl.pallas_call(
        functools.partial(_decode_ret_ffn_kernel, ret_layer=ret_layer, ffn_layer=ffn_layer,
                          final_norm=final_norm),
        grid=(1,),
        in_specs=[full(d), full(vd), full(vd), _const_spec((1, vd)), _const_spec((1, d)), _const_spec((1, d))]
        + [hbm] * len(weights),
        out_specs=full(d),
        out_shape=jax.ShapeDtypeStruct((m, d), F32),
        scratch_shapes=[pltpu.VMEM(w, BF16) for w in weights]
        + [pltpu.VMEM((m, d), F32), pltpu.VMEM((m, vd), BF16), pltpu.VMEM((m, f), BF16),
           pltpu.SemaphoreType.DMA((len(weights),))],
        compiler_params=pltpu.CompilerParams(dimension_semantics=("arbitrary",), vmem_limit_bytes=vmem),
        name="decode_ret_ffn",
    )(x, o, sg, gn, g_ffn, fin, rwout, wg, wu, wd)


def kernel(x_prompt, x_sample, state_conv, state_ret, norm_mix, norm_ffn, conv_w_in, conv_w, conv_w_out,
           ret_w_in, ret_gn, ret_w_out, ffn_w_gate, ffn_w_up, ffn_w_down, final_norm):
    bp, lp, d = x_prompt.shape
    bs, ls, _ = x_sample.shape
    depth = norm_mix.shape[0]
    dk = ret_w_in.shape[2] // 6 // N_HEADS
    inv = (ROPE_BASE ** (-jnp.arange(dk // 2, dtype=F32) / (dk // 2)))[None, :]
    fin = final_norm[None, :]
    conv_w_in, conv_w_out = conv_w_in.astype(BF16), conv_w_out.astype(BF16)
    assert depth % N_MIXERS == 0

    xp = x_prompt
    xs = x_sample.reshape(bs * ls, d)
    conv_p, conv_s, ret_p, ret_s = [], [], [], []
    for j in range(depth // N_MIXERS):
        ic, ir = N_MIXERS * j, N_MIXERS * j + 1
        last = ir == depth - 1
        g_ffn_c, g_ffn_r = norm_ffn[ic][None, :], norm_ffn[ir][None, :]
        gn = ret_gn[j][None, :]

        jobs = _CastJobs([(ffn_w_gate, ic), (ffn_w_up, ic), (ffn_w_down, ic), (ret_w_in, j), (ret_w_out, j)])
        xp, st, (wg, wu, wd, rwin, rwout) = _conv_mixer_seq(
            xp, norm_mix[ic][None, :], conv_w_in, conv_w, conv_w_out, layer=j, jobs=jobs)
        conv_p.append(st)
        xs, st, cos, sin = _decode_conv_ffn(
            xs, state_conv[j], norm_mix[ic][None, :], conv_w_in, conv_w, conv_w_out, g_ffn_c, wg, wu, wd, fin, inv,
            conv_layer=j, ffn_layer=0, seq=ls, final_norm=False, rope_len=lp)
        conv_s.append(st)
        q, k, v, sg = _ret_proj(xs, norm_mix[ir][None, :], rwin, inv, layer=0, seq=ls, pos0=PAST_LEN)
        xp, ret_o, st = _ffn_decode(xp.reshape(bp * lp, d), g_ffn_c, wg, wu, wd, fin, q, k, v, state_ret,
                                    layer=0, final_norm=False, state_layer=j, seq=ls)
        ret_s.append(st)

        jobs = _CastJobs([(ffn_w_gate, ir), (ffn_w_up, ir), (ffn_w_down, ir)])
        xp, st, (wg, wu, wd) = _ret_layer_seq(xp.reshape(bp, lp, d), norm_mix[ir][None, :], rwin, cos, sin, gn,
                                              rwout, layer=0, jobs=jobs)
        ret_p.append(st)
        xs = _decode_ret_ffn(xs, ret_o, sg, gn, rwout, g_ffn_r, wg, wu, wd, fin, ret_layer=0, ffn_layer=0,
                             final_norm=last)
        xp = _ffn(xp.reshape(bp * lp, d), g_ffn_r, wg, wu, wd, fin, layer=0, final_norm=last).reshape(bp, lp, d)
    return (xp, xs.reshape(bs, ls, d), jnp.stack(conv_p), jnp.stack(conv_s), jnp.stack(ret_p),
            jnp.stack(ret_s))
```
